```python
import math
import jax, jax.numpy as jnp
from jax import lax
import numpy as np

D_MODEL = 1024
BATCH = 16
SEQ = 2048
DEPTH = 1
DEC_BATCH = 32
DEC_SEQ = 8
PAST_LEN = 16384
PAGE_SIZE = 128

H_A = 4
DK_A = 128
DV_A = 128
W_A = H_A * DV_A
H_B = 8
HD_B = 64
W_B = H_B * HD_B
D_MIX = W_A + W_B
CONV_W = 4
MLSTM_CHUNK = 64
DIL_PATTERNS = ((128, 1), (512, 4), (2048, 16))
WIN_MAX = 2048
N_BUCKETS = 32
MAX_DIST = WIN_MAX
PEER_HEADS = 8
PEER_DQ = 256
N_SUBKEYS = 128
N_EXPERTS = N_SUBKEYS * N_SUBKEYS
PEER_TOPK = 16
PEER_BLOCK = 256
EPS = 1e-6
NEG = -1e30
SPLITS = (W_B, 2 * W_B, 3 * W_B, 3 * W_B + 2 * W_A, 3 * W_B + 3 * W_A, 3 * W_B + 4 * W_A, 3 * W_B + 4 * W_A + H_A)
W_IN = 3 * W_B + 4 * W_A + 2 * H_A

kernel_name = 'hymba_mlstm_dilated_peer_step'


def rmsnorm(x, g):
    xf = x.astype(jnp.float32)
    y = xf * lax.rsqrt(jnp.mean(xf * xf, axis=-1, keepdims=True) + EPS)
    return (y * g.astype(jnp.float32)).astype(x.dtype)


def t5_bucket(dist):
    max_exact = N_BUCKETS // 2
    d = np.maximum(dist, 1).astype(np.float32)
    large = max_exact + (np.log(d / max_exact) / math.log(MAX_DIST / max_exact) * (N_BUCKETS - max_exact)).astype(np.int32)
    large = np.minimum(large, N_BUCKETS - 1)
    return np.where(dist < max_exact, dist, large).astype(np.int32)


def last_rows(a, n):
    t = a.shape[1]
    if t >= n:
        return a[:, t - n:]
    return jnp.pad(a, ((0, 0), (n - t, 0)) + ((0, 0),) * (a.ndim - 2))


def dilated_branch_prompt(q, k, v, rel_bias, window, dil):
    B, T, H, hd = q.shape
    steps = window // dil
    n = T // dil
    nb = -(-n // steps)
    npad = nb * steps

    def to_band(a):
        a = a.reshape(B, n, dil, H, hd)
        a = jnp.pad(a, ((0, 0), (0, npad - n), (0, 0), (0, 0), (0, 0)))
        return a.reshape(B, nb, steps, dil, H, hd)

    def with_prev(a):
        prev = jnp.pad(a, ((0, 0), (1, 0), (0, 0), (0, 0), (0, 0), (0, 0)))[:, :nb]
        return jnp.concatenate([prev, a], axis=2)

    qb = to_band(q)
    kk = with_prev(to_band(k))
    vv = with_prev(to_band(v))
    qi = np.arange(steps)[:, None]
    kj = np.arange(2 * steps)[None, :]
    jstep = steps + qi - kj
    band = (jstep >= 0) & (jstep <= steps)
    mask = band[None] & ~((np.arange(nb)[:, None, None] == 0) & (kj[None] < steps))
    bias = jnp.moveaxis(rel_bias[t5_bucket(np.clip(jstep, 0, steps) * dil)].astype(jnp.float32), -1, 0)
    logits = jnp.einsum('bnqrhd,bnkrhd->bnrhqk', qb, kk) * (hd ** -0.5) + bias
    logits = jnp.where(mask[None, :, None, None], logits, NEG)
    mx = jnp.max(logits, axis=-1, keepdims=True)
    p = jnp.exp(logits - mx)
    den = jnp.sum(p, axis=-1)
    o = jnp.einsum('bnrhqk,bnkrhd->bnqrhd', p, vv) / jnp.moveaxis(den, -1, 2)[..., None]
    lse = jnp.moveaxis(mx[..., 0] + jnp.log(den), -1, 2)
    o = o.reshape(B, npad * dil, H, hd)[:, :T]
    lse = lse.reshape(B, npad * dil, H)[:, :T]
    return o, lse


def dilated_branch_sample(q, k_all, v_all, rel_bias, window, dil, buf_len):
    B, S, H, hd = q.shape
    steps = window // dil
    j = np.arange(steps + 1)
    idx = buf_len + np.arange(S)[:, None] - j[None, :] * dil
    valid = idx >= 0
    idx_c = np.maximum(idx, 0)
    kg = k_all[:, idx_c]
    vg = v_all[:, idx_c]
    bias = rel_bias[t5_bucket(j * dil)].astype(jnp.float32).T
    logits = jnp.einsum('bshd,bsjhd->bshj', q, kg) * (hd ** -0.5) + bias
    logits = jnp.where(valid[None, :, None, :], logits, NEG)
    mx = jnp.max(logits, axis=-1, keepdims=True)
    p = jnp.exp(logits - mx)
    den = jnp.sum(p, axis=-1)
    o = jnp.einsum('bshj,bsjhd->bshd', p, vg) / den[..., None]
    lse = mx[..., 0] + jnp.log(den)
    return o, lse


def combine_branches(branches):
    outs = jnp.stack([o for o, _ in branches], axis=0)
    lses = jnp.stack([l for _, l in branches], axis=0)
    wts = jax.nn.softmax(lses, axis=0)
    return jnp.einsum('pbth,pbthd->bthd', wts, outs)


def short_conv(u, buf, conv_w, conv_b):
    t = u.shape[1]
    ext = jnp.concatenate([buf.astype(u.dtype), u], axis=1)
    y = conv_b
    for j in range(CONV_W):
        y = y + ext[:, j:j + t] * conv_w[j]
    return jax.nn.silu(y), ext[:, ext.shape[1] - (CONV_W - 1):]


def mlstm(q, k, v, ig, lf, C0, n0, m0):
    B, T, H, _ = q.shape
    L = math.gcd(T, MLSTM_CHUNK)
    nc = T // L
    f32 = jnp.float32

    def chunks(a):
        a = a.astype(f32).reshape((B, nc, L, H) + a.shape[3:])
        return jnp.moveaxis(a, (1, 3), (0, 2))

    causal = jnp.asarray(np.tril(np.ones((L, L), dtype=bool)))

    def step(carry, xs):
        C, n, m = carry
        qc, kc, vc, igc, lfc = xs
        b = jnp.cumsum(lfc, axis=-1)
        g = b + m[..., None]
        D = jnp.where(causal, b[..., :, None] - b[..., None, :] + igc[..., None, :], -jnp.inf)
        mt = jnp.maximum(g, jnp.max(D, axis=-1))
        wD = jnp.exp(D - mt[..., None])
        wg = jnp.exp(g - mt)
        s = jnp.einsum('bhtd,bhsd->bhts', qc, kc) * wD
        num = wg[..., None] * jnp.einsum('bhtd,bhde->bhte', qc, C) + jnp.einsum('bhts,bhse->bhte', s, vc)
        den = wg * jnp.einsum('bhtd,bhd->bht', qc, n) + jnp.sum(s, axis=-1)
        h = num / jnp.maximum(jnp.abs(den), jnp.exp(-mt))[..., None]
        bL = b[..., -1]
        ws_log = bL[..., None] - b + igc
        m_new = jnp.maximum(bL + m, jnp.max(ws_log, axis=-1))
        ws = jnp.exp(ws_log - m_new[..., None])
        wc = jnp.exp(bL + m - m_new)
        C_new = wc[..., None, None] * C + jnp.einsum('bhsd,bhse->bhde', ws[..., None] * kc, vc)
        n_new = wc[..., None] * n + jnp.einsum('bhs,bhsd->bhd', ws, kc)
        return (C_new, n_new, m_new), h

    init = (C0.astype(f32), n0.astype(f32), m0.astype(f32))
    (C, n, m), hs = lax.scan(step, init, (chunks(q), chunks(k), chunks(v), chunks(ig), chunks(lf)))
    h = jnp.moveaxis(hs, (0, 2), (1, 3)).reshape(B, T, H, hs.shape[-1])
    return h, C, n, m


def peer(h, w_q, sub_keys, u_tab, v_tab):
    B, T, D = h.shape
    xt = h.reshape(B * T, D)
    ntok = xt.shape[0]
    npad = (-ntok) % PEER_BLOCK
    xb = jnp.pad(xt, ((0, npad), (0, 0))).reshape(-1, PEER_BLOCK, D)
    K = PEER_TOPK

    def block(xblk):
        qh = (xblk @ w_q).reshape(PEER_BLOCK, PEER_HEADS, 2, PEER_DQ // 2)
        s = jnp.einsum('nhpd,pkd->nhpk', qh, sub_keys).astype(jnp.float32)
        sv, si = lax.top_k(s, K)
        cand = (sv[:, :, 0, :, None] + sv[:, :, 1, None, :]).reshape(PEER_BLOCK, PEER_HEADS, K * K)
        cidx = (si[:, :, 0, :, None] * N_SUBKEYS + si[:, :, 1, None, :]).reshape(PEER_BLOCK, PEER_HEADS, K * K)
        fv, fpos = lax.top_k(cand, K)
        eidx = jnp.take_along_axis(cidx, fpos, axis=-1)
        gate = jax.nn.softmax(fv, axis=-1)
        act = jax.nn.gelu(jnp.einsum('nd,nhkd->nhk', xblk, u_tab[eidx]), approximate=False)
        return jnp.einsum('nhk,nhkd->nd', (gate * act).astype(xblk.dtype), v_tab[eidx])

    out = lax.map(block, xb).reshape(-1, D)[:ntok]
    return out.reshape(B, T, D)


def hybrid_layer(x, c, k_buf, v_buf, conv_buf, C0, n0, m0, rel_bias, w_ada, b_ada, norm1_g, norm2_g, w_in,
                 b_gate, conv_w, conv_b, q_norm_g, k_norm_g, att_out_g, mlstm_out_g, w_out, peer_wq,
                 peer_keys, peer_u, peer_v, buf_len):
    B, T, _ = x.shape
    f32 = jnp.float32
    mod = jax.nn.silu(c) @ w_ada + b_ada
    shift1, scale1, gate1, shift2, scale2, gate2 = jnp.split(mod[:, None, :], 6, axis=-1)
    h = rmsnorm(x, norm1_g) * (1 + scale1) + shift1
    aq, ak, av, mqk, mv, mo, mi, mf = jnp.split(h @ w_in, SPLITS, axis=-1)

    aq = rmsnorm(aq.reshape(B, T, H_B, HD_B).astype(f32), q_norm_g)
    ak = rmsnorm(ak.reshape(B, T, H_B, HD_B).astype(f32), k_norm_g)
    av = av.reshape(B, T, H_B, HD_B).astype(f32)
    if k_buf is None:
        att = combine_branches([dilated_branch_prompt(aq, ak, av, rel_bias, w, d) for w, d in DIL_PATTERNS])
        k_new = last_rows(ak, buf_len)
        v_new = last_rows(av, buf_len)
    else:
        k_all = jnp.concatenate([k_buf.astype(f32), ak], axis=1)
        v_all = jnp.concatenate([v_buf.astype(f32), av], axis=1)
        L0 = k_buf.shape[1]
        att = combine_branches([dilated_branch_sample(aq, k_all, v_all, rel_bias, w, d, L0) for w, d in DIL_PATTERNS])
        k_new = last_rows(k_all, buf_len)
        v_new = last_rows(v_all, buf_len)
    att = rmsnorm(att, att_out_g)

    if conv_buf is None:
        conv_buf = jnp.zeros((B, CONV_W - 1, 2 * W_A), x.dtype)
        C0 = jnp.zeros((B, H_A, DK_A, DV_A), f32)
        n0 = jnp.zeros((B, H_A, DK_A), f32)
        m0 = jnp.zeros((B, H_A), f32)
    qk, conv_new = short_conv(mqk, conv_buf, conv_w, conv_b)
    qk = qk.astype(f32).reshape(B, T, 2, H_A, DK_A)
    mq = qk[:, :, 0]
    mk = qk[:, :, 1] * (DK_A ** -0.5)
    mv = mv.reshape(B, T, H_A, DV_A)
    ig = mi.astype(f32) + b_gate[:H_A].astype(f32)
    lf = jax.nn.log_sigmoid(mf.astype(f32) + b_gate[H_A:].astype(f32))
    hA, C_new, n_new, m_new = mlstm(mq, mk, mv, ig, lf, C0, n0, m0)
    hA = rmsnorm(hA, mlstm_out_g) * jax.nn.sigmoid(mo.astype(f32)).reshape(B, T, H_A, DV_A)

    mix = jnp.concatenate([hA.reshape(B, T, W_A), att.reshape(B, T, W_B)], axis=-1).astype(x.dtype) @ w_out
    x = x + gate1 * mix
    h2 = rmsnorm(x, norm2_g) * (1 + scale2) + shift2
    x = x + gate2 * peer(h2, peer_wq, peer_keys, peer_u, peer_v)
    return x, (k_new, v_new, conv_new, C_new, n_new, m_new)


def setup_inputs(seed: int = 0) -> dict:
    key = jax.random.key(seed)
    ks = iter(jax.random.split(key, 40))

    def rnd(shape, scale):
        return jax.random.normal(next(ks), shape, jnp.float32) * scale

    wbuf = min(WIN_MAX, PAST_LEN)
    D = D_MODEL
    inputs = {
        'x_prompt': rnd((BATCH, SEQ, D), 1.0),
        'x_sample': rnd((DEC_BATCH, DEC_SEQ, D), 1.0),
        'cache_k': rnd((DEPTH, DEC_BATCH, wbuf, H_B, HD_B), 1.0),
        'cache_v': rnd((DEPTH, DEC_BATCH, wbuf, H_B, HD_B), 1.0),
        'state_conv': rnd((DEPTH, DEC_BATCH, CONV_W - 1, 2 * W_A), 1.0),
        'state_C': rnd((DEPTH, DEC_BATCH, H_A, DK_A, DV_A), 0.5),
        'state_n': rnd((DEPTH, DEC_BATCH, H_A, DK_A), 0.5),
        'state_m': rnd((DEPTH, DEC_BATCH, H_A), 1.0),
        'c_prompt': rnd((BATCH, D), 1.0),
        'c_sample': rnd((DEC_BATCH, D), 1.0),
        'rel_bias': rnd((N_BUCKETS, H_B), 0.5),
        'w_ada': rnd((DEPTH, D, 6 * D), 0.5 * D ** -0.5),
        'b_ada': rnd((DEPTH, 6 * D), 0.01),
        'norm1_g': 1.0 + rnd((DEPTH, D), 0.02),
        'norm2_g': 1.0 + rnd((DEPTH, D), 0.02),
        'w_in': rnd((DEPTH, D, W_IN), D ** -0.5),
        'b_gate': jnp.concatenate([rnd((DEPTH, H_A), 0.1),
                                   jnp.linspace(3.0, 6.0, H_A, dtype=jnp.float32) + rnd((DEPTH, H_A), 0.1)], axis=-1),
        'conv_w': rnd((DEPTH, CONV_W, 2 * W_A), CONV_W ** -0.5),
        'conv_b': rnd((DEPTH, 2 * W_A), 0.01),
        'q_norm_g': 1.0 + rnd((DEPTH, HD_B), 0.02),
        'k_norm_g': 1.0 + rnd((DEPTH, HD_B), 0.02),
        'att_out_g': 1.0 + rnd((DEPTH, H_B, HD_B), 0.02),
        'mlstm_out_g': 1.0 + rnd((DEPTH, H_A, DV_A), 0.02),
        'w_out': rnd((DEPTH, D_MIX, D), D_MIX ** -0.5),
        'peer_wq': rnd((DEPTH, D, PEER_HEADS * PEER_DQ), D ** -0.5),
        'peer_keys': rnd((DEPTH, 2, N_SUBKEYS, PEER_DQ // 2), (PEER_DQ // 2) ** -0.5),
        'peer_u': rnd((DEPTH, N_EXPERTS, D), D ** -0.5),
        'peer_v': rnd((DEPTH, N_EXPERTS, D), PEER_HEADS ** -0.5),
    }
    return inputs


def reference(x_prompt, x_sample, cache_k, cache_v, state_conv, state_C, state_n, state_m, c_prompt, c_sample,
              rel_bias, w_ada, b_ada, norm1_g, norm2_g, w_in, b_gate, conv_w, conv_b, q_norm_g, k_norm_g,
              att_out_g, mlstm_out_g, w_out, peer_wq, peer_keys, peer_u, peer_v):
    buf_len = cache_k.shape[2]
    yp = x_prompt
    ys = x_sample
    sp_list = []
    ss_list = []
    for l in range(DEPTH):
        yp, sp = hybrid_layer(yp, c_prompt, None, None, None, None, None, None, rel_bias,
                              w_ada[l], b_ada[l], norm1_g[l], norm2_g[l], w_in[l], b_gate[l], conv_w[l], conv_b[l],
                              q_norm_g[l], k_norm_g[l], att_out_g[l], mlstm_out_g[l], w_out[l], peer_wq[l],
                              peer_keys[l], peer_u[l], peer_v[l], buf_len)
        ys, ss = hybrid_layer(ys, c_sample, cache_k[l], cache_v[l], state_conv[l], state_C[l], state_n[l],
                              state_m[l], rel_bias,
                              w_ada[l], b_ada[l], norm1_g[l], norm2_g[l], w_in[l], b_gate[l], conv_w[l], conv_b[l],
                              q_norm_g[l], k_norm_g[l], att_out_g[l], mlstm_out_g[l], w_out[l], peer_wq[l],
                              peer_keys[l], peer_u[l], peer_v[l], buf_len)
        sp_list.append(sp)
        ss_list.append(ss)
    k_p, v_p, conv_p, C_p, n_p, m_p = [jnp.stack([s[i] for s in sp_list], axis=0) for i in range(6)]
    k_s, v_s, conv_s, C_s, n_s, m_s = [jnp.stack([s[i] for s in ss_list], axis=0) for i in range(6)]
    return (yp, ys, k_p, v_p, conv_p, C_p, n_p, m_p, k_s, v_s, conv_s, C_s, n_s, m_s)
```

```python
import functools
import math

import numpy as np
import jax
import jax.numpy as jnp
from jax import lax
from jax.experimental import pallas as pl
from jax.experimental.pallas import tpu as pltpu

F32 = jnp.float32
BF16 = jnp.bfloat16
HIGHEST = lax.Precision.HIGHEST

D_MODEL = 1024
H_A = 4
DK_A = 128
W_A = H_A * DK_A
H_B = 8
HD_B = 64
W_B = H_B * HD_B
CONV_W = 4
MLSTM_CHUNK = 64
DIL_PATTERNS = ((128, 1), (512, 4), (2048, 16))
ATT_STEPS = 128
N_BUCKETS = 32
MAX_DIST = 2048
PEER_HEADS = 8
N_SUBKEYS = 128
PEER_TOPK = 16
PEER_PAIRS = PEER_HEADS * PEER_TOPK
EPS = 1e-6
NEG = -1e30
W_MAIN = 3 * W_B + 4 * W_A
LANES = 128
VMEM_LIMIT = 56 * 1024 * 1024


def _cparams(sem):
    return pltpu.CompilerParams(dimension_semantics=sem, vmem_limit_bytes=VMEM_LIMIT)


def _nt_dot(a, b, precision=None):
    return lax.dot_general(a, b, (((1,), (1,)), ((), ())), preferred_element_type=F32, precision=precision)


def _t5_bucket(dist):
    max_exact = N_BUCKETS // 2
    d = np.maximum(dist, 1).astype(np.float32)
    large = max_exact + (np.log(d / max_exact) / math.log(MAX_DIST / max_exact) * (N_BUCKETS - max_exact)).astype(np.int32)
    large = np.minimum(large, N_BUCKETS - 1)
    return np.where(dist < max_exact, dist, large).astype(np.int32)


def _ada_kernel(c_ref, w_ref, b_ref, o_ref):
    c = c_ref[...]
    s = c * jax.nn.sigmoid(c)
    o_ref[...] = jnp.dot(s, w_ref[...], preferred_element_type=F32, precision=HIGHEST) + b_ref[...]


def _ada(c_all, w_ada, b_ada):
    n = c_all.shape[0]
    return pl.pallas_call(
        _ada_kernel,
        grid=(6,),
        in_specs=[pl.BlockSpec((n, D_MODEL), lambda j: (0, 0)),
                  pl.BlockSpec((D_MODEL, D_MODEL), lambda j: (0, j)),
                  pl.BlockSpec((1, D_MODEL), lambda j: (0, j))],
        out_specs=pl.BlockSpec((n, D_MODEL), lambda j: (0, j)),
        out_shape=jax.ShapeDtypeStruct((n, 6 * D_MODEL), F32),
        compiler_params=_cparams(("arbitrary",)),
        name="ada",
    )(c_all, w_ada, b_ada.reshape(1, -1))


def _group_mean_sq(a, bd):
    sq = a * a
    hi = sq.astype(BF16)
    lo = (sq - hi.astype(F32)).astype(BF16)
    return jnp.dot(hi, bd, preferred_element_type=F32) + jnp.dot(lo, bd, preferred_element_type=F32)


def _in_kernel(x_ref, sc_ref, sh_ref, g1_ref, wm_ref, wgc_ref, wgr_ref, bd_ref, qg_ref, kg_ref,
               q_ref, k_ref, v_ref, mqk_ref, mv_ref, mo_ref, gcol_ref, grow_ref):
    x = x_ref[...]
    ms = jnp.mean(x * x, axis=-1, keepdims=True)
    h = x * lax.rsqrt(ms + EPS) * g1_ref[...]
    h = h * (1.0 + sc_ref[...]) + sh_ref[...]
    y = jnp.dot(h.astype(BF16), wm_ref[...], preferred_element_type=F32)
    bd = bd_ref[...]
    aq = y[:, 0:W_B]
    ak = y[:, W_B:2 * W_B]
    q_ref[...] = aq * lax.rsqrt(_group_mean_sq(aq, bd) + EPS) * qg_ref[...] * (HD_B ** -0.5)
    k_ref[...] = ak * lax.rsqrt(_group_mean_sq(ak, bd) + EPS) * kg_ref[...]
    v_ref[...] = y[:, 2 * W_B:3 * W_B]
    o = 3 * W_B
    mqk_ref[...] = y[:, o:o + 2 * W_A]
    mv_ref[...] = y[:, o + 2 * W_A:o + 3 * W_A]
    mo_ref[...] = y[:, o + 3 * W_A:o + 4 * W_A]
    gcol_ref[...] = jnp.dot(h, wgc_ref[...], preferred_element_type=F32, precision=HIGHEST)
    grow_ref[...] = _nt_dot(wgr_ref[...], h, precision=HIGHEST)


def _in_proj(x2, scale, shift, per_token_mod, toks_per_seq, tm, g1, wm, wgc, wgr, bd, qg, kg):
    n = x2.shape[0]
    nt = n // tm
    if per_token_mod:
        mod_spec = pl.BlockSpec((tm, D_MODEL), lambda i: (i, 0))
    else:
        tiles_per_seq = toks_per_seq // tm
        mod_spec = pl.BlockSpec((None, 1, D_MODEL), lambda i: (i // tiles_per_seq, 0, 0))
    const = lambda shape: pl.BlockSpec(shape, lambda i: (0,) * len(shape))
    tok = lambda w: pl.BlockSpec((tm, w), lambda i: (i, 0))
    outs = pl.pallas_call(
        _in_kernel,
        grid=(nt,),
        in_specs=[tok(D_MODEL), mod_spec, mod_spec, const((1, D_MODEL)), const((D_MODEL, W_MAIN)),
                  const((D_MODEL, LANES)), const((8, D_MODEL)), const((W_B, W_B)), const((1, W_B)), const((1, W_B))],
        out_specs=[tok(W_B), tok(W_B), tok(W_B), tok(2 * W_A), tok(W_A), tok(W_A), tok(LANES),
                   pl.BlockSpec((8, tm), lambda i: (0, i))],
        out_shape=[jax.ShapeDtypeStruct((n, W_B), F32)] * 3 + [jax.ShapeDtypeStruct((n, 2 * W_A), F32)]
                  + [jax.ShapeDtypeStruct((n, W_A), F32)] * 2 + [jax.ShapeDtypeStruct((n, LANES), F32),
                                                                 jax.ShapeDtypeStruct((8, n), F32)],
        compiler_params=_cparams(("arbitrary",)),
        name="in_proj",
    )(x2, scale, shift, g1, wm, wgc, wgr, bd, qg, kg)
    return outs


def _log_sigmoid(x):
    return jnp.minimum(x, 0.0) - jnp.log1p(jnp.exp(-jnp.abs(x)))


def _mlstm_kernel(bg_ref, m0_ref, mq_ref, mk_ref, cwq_ref, cwk_ref, cbq_ref, cbk_ref, bufq_ref, bufk_ref,
                  v_ref, gcol_ref, grow_ref, c0_ref, n0_ref,
                  h_ref, c_ref, n_ref, m_ref, sq, sk, *, T, L, rows_whole):
    b = pl.program_id(0)
    hd = pl.program_id(1)
    nc = T // L

    def conv(u_ref, buf_ref, w_ref, cb_ref, s_ref, out_ref):
        s_ref[0:8, :] = jnp.zeros((8, LANES), F32)
        s_ref[5:8, :] = buf_ref[...]
        s_ref[8:8 + T, :] = u_ref[...]
        y = cb_ref[...]
        for j in range(CONV_W):
            y = y + s_ref[5 + j:5 + j + T, :] * w_ref[j:j + 1, :]
        out_ref[...] = y * jax.nn.sigmoid(y)

    conv(mq_ref, bufq_ref, cwq_ref, cbq_ref, sq, h_ref)
    conv(mk_ref, bufk_ref, cwk_ref, cbk_ref, sk, sq.at[8:8 + T, :])

    ig_b = bg_ref[hd]
    f_b = bg_ref[H_A + hd]
    lane = lax.broadcasted_iota(jnp.int32, (L, LANES), 1)
    ri = lax.broadcasted_iota(jnp.int32, (L, L), 0)
    ci = lax.broadcasted_iota(jnp.int32, (L, L), 1)
    causal = ri >= ci
    tri = causal.astype(F32)
    tri_t = (ri <= ci).astype(F32)

    def chunk(c, carry):
        C, n, m = carry
        r0 = pl.multiple_of(c * L, L)
        qc = h_ref[pl.ds(r0, L), :]
        kc = sq[pl.ds(r0 + 8, L), :] * (DK_A ** -0.5)
        vc = v_ref[pl.ds(r0, L), :]
        g = gcol_ref[pl.ds(r0, L), :]
        ig_col = jnp.sum(jnp.where(lane == hd, g, 0.0), axis=1, keepdims=True) + ig_b
        f_col = jnp.sum(jnp.where(lane == hd + H_A, g, 0.0), axis=1, keepdims=True) + f_b
        lf_col = _log_sigmoid(f_col)
        rc = c + (b * nc if rows_whole else 0)
        ig_row = grow_ref[hd, pl.ds(rc, 1), :] + ig_b
        lf_row = _log_sigmoid(grow_ref[hd + H_A, pl.ds(rc, 1), :] + f_b)
        b_col = jnp.dot(tri, jnp.broadcast_to(lf_col, (L, LANES)), preferred_element_type=F32,
                        precision=HIGHEST)[:, 0:1]
        b_row = jnp.dot(jnp.broadcast_to(lf_row, (8, L)), tri_t, preferred_element_type=F32,
                        precision=HIGHEST)[0:1, :]
        g_col = b_col + m
        dm = jnp.where(causal, b_col - b_row + ig_row, -jnp.inf)
        mt = jnp.maximum(g_col, jnp.max(dm, axis=1, keepdims=True))
        wd = jnp.exp(dm - mt)
        wg = jnp.exp(g_col - mt)
        qb = qc.astype(BF16)
        kb = kc.astype(BF16)
        vb = vc.astype(BF16)
        s = _nt_dot(qb, kb) * wd
        num = wg * jnp.dot(qb, C.astype(BF16), preferred_element_type=F32) \
            + jnp.dot(s.astype(BF16), vb, preferred_element_type=F32)
        den = wg * jnp.sum(qc * n, axis=1, keepdims=True) + jnp.sum(s, axis=1, keepdims=True)
        h_ref[pl.ds(r0, L), :] = num / jnp.maximum(jnp.abs(den), jnp.exp(-mt))
        bl = b_col[L - 1:L, :]
        ws_log_col = bl - b_col + ig_col
        ws_log_row = bl - b_row + ig_row
        m_new = jnp.maximum(bl + m, jnp.max(ws_log_row, axis=1, keepdims=True))
        ws_col = jnp.exp(ws_log_col - m_new)
        wc = jnp.exp(bl + m - m_new)
        kw = ws_col * kc
        upd = lax.dot_general(kw.astype(BF16), vb, (((0,), (0,)), ((), ())), preferred_element_type=F32)
        return wc * C + upd, wc * n + jnp.sum(kw, axis=0, keepdims=True), m_new

    m0 = jnp.full((1, 1), m0_ref[b, hd], F32)
    C, n, m = lax.fori_loop(0, nc, chunk, (c0_ref[...], n0_ref[...], m0))
    c_ref[...] = C
    n_ref[...] = n
    m_ref[...] = jnp.broadcast_to(m, (1, LANES))


def _mlstm(mqk, mv, gcol, grow, conv_w, conv_b, conv_buf, b_gate, C0, n0, m0, B, T):
    L = math.gcd(T, MLSTM_CHUNK)
    nc = T // L
    n = B * T
    rows_whole = nc % 8 != 0
    grow3 = grow.reshape(8, n // L, L)
    if rows_whole:
        grow_spec = pl.BlockSpec((8, n // L, L), lambda b, h: (0, 0, 0))
    else:
        grow_spec = pl.BlockSpec((8, nc, L), lambda b, h: (0, b, 0))
    smem = pl.BlockSpec(memory_space=pltpu.SMEM)
    colq = lambda rows: pl.BlockSpec((rows, LANES), lambda b, h: (0, h))
    colk = lambda rows: pl.BlockSpec((rows, LANES), lambda b, h: (0, h + H_A))
    st = lambda r, c: pl.BlockSpec((None, None, r, c), lambda b, h: (b, h, 0, 0))
    kern = functools.partial(_mlstm_kernel, T=T, L=L, rows_whole=rows_whole)
    return pl.pallas_call(
        kern,
        grid=(B, H_A),
        in_specs=[smem, smem,
                  pl.BlockSpec((T, LANES), lambda b, h: (b, h)), pl.BlockSpec((T, LANES), lambda b, h: (b, h + H_A)),
                  colq(CONV_W), colk(CONV_W), colq(1), colk(1),
                  pl.BlockSpec((None, CONV_W - 1, LANES), lambda b, h: (b, 0, h)),
                  pl.BlockSpec((None, CONV_W - 1, LANES), lambda b, h: (b, 0, h + H_A)),
                  pl.BlockSpec((T, LANES), lambda b, h: (b, h)),
                  pl.BlockSpec((T, LANES), lambda b, h: (b, 0)),
                  grow_spec, st(DK_A, DK_A), st(1, DK_A)],
        out_specs=[pl.BlockSpec((T, LANES), lambda b, h: (b, h)), st(DK_A, DK_A), st(1, DK_A), st(1, LANES)],
        out_shape=[jax.ShapeDtypeStruct((n, W_A), F32), jax.ShapeDtypeStruct((B, H_A, DK_A, DK_A), F32),
                   jax.ShapeDtypeStruct((B, H_A, 1, DK_A), F32), jax.ShapeDtypeStruct((B, H_A, 1, LANES), F32)],
        scratch_shapes=[pltpu.VMEM((T + 8, LANES), F32), pltpu.VMEM((T + 8, LANES), F32)],
        compiler_params=_cparams(("arbitrary", "arbitrary")),
        name="mlstm",
    )(b_gate, m0, mqk, mqk, conv_w, conv_w, conv_b.reshape(1, -1), conv_b.reshape(1, -1), conv_buf, conv_buf,
      mv, gcol, grow3, C0, n0.reshape(B, H_A, 1, DK_A))


def _att_prompt_kernel(q_ref, k_ref, v_ref, bias_ref, o_ref, num_s, mx_s, dn_s, *, T):
    lane = lax.broadcasted_iota(jnp.int32, (ATT_STEPS, LANES), 1)
    first = lane < HD_B
    nblk = T // ATT_STEPS

    for br, (_, dil) in enumerate(DIL_PATTERNS):
        span = ATT_STEPS * dil
        has_prev = T > span

        def rows(start, dil=dil):
            return pl.ds(start, ATT_STEPS, stride=dil) if dil > 1 else pl.ds(start, ATT_STEPS)

        def blk(i, carry, br=br, dil=dil, span=span, has_prev=has_prev, rows=rows):
            r = i % dil
            nb = i // dil
            start = nb * span + r
            qb = q_ref[rows(start), :]
            kcur = k_ref[rows(start), :].astype(BF16)
            vcur = v_ref[rows(start), :].astype(BF16)
            if has_prev:
                pstart = jnp.maximum(start - span, 0)
                kprev = k_ref[rows(pstart), :].astype(BF16)
                vprev = v_ref[rows(pstart), :].astype(BF16)
            res = []
            for hh in range(2):
                qh = jnp.where(first if hh == 0 else ~first, qb, 0.0).astype(BF16)
                lc = _nt_dot(qh, kcur) + bias_ref[br, hh, :, ATT_STEPS:2 * ATT_STEPS]
                mx = jnp.max(lc, axis=1, keepdims=True)
                if has_prev:
                    lp = _nt_dot(qh, kprev) + bias_ref[br, hh, :, 0:ATT_STEPS]
                    lp = jnp.where(nb > 0, lp, NEG)
                    mx = jnp.maximum(mx, jnp.max(lp, axis=1, keepdims=True))
                pc = jnp.exp(lc - mx)
                den = jnp.sum(pc, axis=1, keepdims=True)
                num = jnp.dot(pc.astype(BF16), vcur, preferred_element_type=F32)
                if has_prev:
                    pp = jnp.exp(lp - mx)
                    den = den + jnp.sum(pp, axis=1, keepdims=True)
                    num = num + jnp.dot(pp.astype(BF16), vprev, preferred_element_type=F32)
                res.append((num, mx, den))
            num_s[br, rows(start), :] = jnp.where(first, res[0][0], res[1][0])
            mx_s[br, rows(start), :] = jnp.where(first, res[0][1], res[1][1])
            dn_s[br, rows(start), :] = jnp.where(first, res[0][2], res[1][2])
            return carry

        lax.fori_loop(0, nblk, blk, 0)

    def comb(i, carry):
        rs = pl.ds(pl.multiple_of(i * ATT_STEPS, ATT_STEPS), ATT_STEPS)
        m = jnp.maximum(jnp.maximum(mx_s[0, rs, :], mx_s[1, rs, :]), mx_s[2, rs, :])
        num = jnp.zeros((ATT_STEPS, LANES), F32)
        den = jnp.zeros((ATT_STEPS, LANES), F32)
        for br in range(3):
            e = jnp.exp(mx_s[br, rs, :] - m)
            num = num + e * num_s[br, rs, :]
            den = den + e * dn_s[br, rs, :]
        o_ref[rs, :] = num / den
        return carry

    lax.fori_loop(0, nblk, comb, 0)


def _att_prompt(q, k, v, bias, B, T):
    n = B * T
    blk = pl.BlockSpec((T, LANES), lambda b, p: (b, p))
    kern = functools.partial(_att_prompt_kernel, T=T)
    return pl.pallas_call(
        kern,
        grid=(B, H_B // 2),
        in_specs=[blk, blk, blk, pl.BlockSpec((3, 2, ATT_STEPS, 2 * ATT_STEPS), lambda b, p: (0, p, 0, 0))],
        out_specs=blk,
        out_shape=jax.ShapeDtypeStruct((n, W_B), F32),
        scratch_shapes=[pltpu.VMEM((3, T, LANES), F32)] * 3,
        compiler_params=_cparams(("arbitrary", "arbitrary")),
        name="att_prompt",
    )(q, k, v, bias)


def _prompt_bias(rel_bias):
    qi = np.arange(ATT_STEPS)[:, None]
    kj = np.arange(2 * ATT_STEPS)[None, :]
    jstep = ATT_STEPS + qi - kj
    band = (jstep >= 0) & (jstep <= ATT_STEPS)
    tabs = []
    for _, dil in DIL_PATTERNS:
        bucket = _t5_bucket(np.clip(jstep, 0, ATT_STEPS) * dil)
        t = jnp.moveaxis(rel_bias[bucket].astype(F32), -1, 0)
        tabs.append(jnp.where(band[None], t, NEG))
    return jnp.stack(tabs, axis=0)


def _att_sample_kernel(q_ref, kn_ref, vn_ref, ck_ref, cv_ref, bc_ref, bn_ref, o_ref, ko_ref, vo_ref, *, S, WBUF):
    lane = lax.broadcasted_iota(jnp.int32, (S, LANES), 1)
    first = lane < HD_B
    for hp in range(H_B // 2):
        cs = slice(hp * LANES, (hp + 1) * LANES)
        kc = ck_ref[:, cs].astype(BF16)
        vc = cv_ref[:, cs].astype(BF16)
        kn = kn_ref[:, cs].astype(BF16)
        vn = vn_ref[:, cs].astype(BF16)
        qp = q_ref[:, cs]
        res = []
        for hh in range(2):
            h = 2 * hp + hh
            qh = jnp.where(first if hh == 0 else ~first, qp, 0.0).astype(BF16)
            lc = _nt_dot(qh, kc) + bc_ref[h]
            ln = _nt_dot(qh, kn) + bn_ref[h]
            mx = jnp.maximum(jnp.max(lc, axis=1, keepdims=True), jnp.max(ln, axis=1, keepdims=True))
            pc = jnp.exp(lc - mx)
            pn = jnp.exp(ln - mx)
            den = jnp.sum(pc, axis=1, keepdims=True) + jnp.sum(pn, axis=1, keepdims=True)
            num = jnp.dot(pc.astype(BF16), vc, preferred_element_type=F32) \
                + jnp.dot(pn.astype(BF16), vn, preferred_element_type=F32)
            res.append(num / den)
        o_ref[:, cs] = jnp.where(first, res[0], res[1])
    ko_ref[0:WBUF - S, :] = ck_ref[S:WBUF, :]
    ko_ref[WBUF - S:WBUF, :] = kn_ref[...]
    vo_ref[0:WBUF - S, :] = cv_ref[S:WBUF, :]
    vo_ref[WBUF - S:WBUF, :] = vn_ref[...]


def _att_sample(q, kn, vn, ck, cv, bias_c, bias_n, B, S, WBUF):
    tokb = pl.BlockSpec((S, W_B), lambda b: (b, 0))
    cache = pl.BlockSpec((None, WBUF, W_B), lambda b: (b, 0, 0))
    kern = functools.partial(_att_sample_kernel, S=S, WBUF=WBUF)
    return pl.pallas_call(
        kern,
        grid=(B,),
        in_specs=[tokb, tokb, tokb, cache, cache,
                  pl.BlockSpec((H_B, S, WBUF), lambda b: (0, 0, 0)), pl.BlockSpec((H_B, S, S), lambda b: (0, 0, 0))],
        out_specs=[tokb, cache, cache],
        out_shape=[jax.ShapeDtypeStruct((B * S, W_B), F32), jax.ShapeDtypeStruct((B, WBUF, W_B), F32),
                   jax.ShapeDtypeStruct((B, WBUF, W_B), F32)],
        compiler_params=_cparams(("arbitrary",)),
        name="att_sample",
    )(q, kn, vn, ck, cv, bias_c, bias_n)


def _sample_bias(rel_bias, S, WBUF):
    dist = np.arange(WBUF + S)
    mult = np.zeros(WBUF + S, np.int64)
    for w, dil in DIL_PATTERNS:
        mult += ((dist % dil == 0) & (dist <= w)).astype(np.int64)
    logm = np.where(mult > 0, np.log(np.maximum(mult, 1)), 0.0).astype(np.float32)
    tab = rel_bias[_t5_bucket(dist)].astype(F32).T + logm[None, :]
    tab = jnp.where((mult > 0)[None, :], tab, NEG)
    s = np.arange(S)[:, None]
    dc = WBUF + s - np.arange(WBUF)[None, :]
    dn = s - np.arange(S)[None, :]
    bias_c = tab[:, dc]
    bias_n = jnp.where((dn >= 0)[None], tab[:, np.maximum(dn, 0)], NEG)
    return bias_c, bias_n


def _out_kernel(ha_ref, mo_ref, att_ref, x_ref, g1_ref, sc_ref, sh_ref, mg_ref, ag_ref, g2_ref, bd_ref, wo_ref,
                y_ref, h2_ref):
    ha = ha_ref[...]
    parts = []
    for hd in range(H_A):
        a = ha[:, hd * DK_A:(hd + 1) * DK_A]
        parts.append(a * lax.rsqrt(jnp.mean(a * a, axis=-1, keepdims=True) + EPS))
    hn = jnp.concatenate(parts, axis=1) * mg_ref[...] * jax.nn.sigmoid(mo_ref[...])
    att = att_ref[...]
    an = att * lax.rsqrt(_group_mean_sq(att, bd_ref[...]) + EPS) * ag_ref[...]
    mix = jnp.dot(hn.astype(BF16), wo_ref[0:W_A, :], preferred_element_type=F32) \
        + jnp.dot(an.astype(BF16), wo_ref[W_A:W_A + W_B, :], preferred_element_type=F32)
    y = x_ref[...] + g1_ref[...] * mix
    y_ref[...] = y
    h2 = y * lax.rsqrt(jnp.mean(y * y, axis=-1, keepdims=True) + EPS) * g2_ref[...]
    h2_ref[...] = h2 * (1.0 + sc_ref[...]) + sh_ref[...]


def _out_proj(ha, mo, att, x2, gate1, scale2, shift2, per_token_mod, toks_per_seq, tm, mg, ag, g2, bd, wo):
    n = x2.shape[0]
    if per_token_mod:
        mod_spec = pl.BlockSpec((tm, D_MODEL), lambda i: (i, 0))
    else:
        tiles_per_seq = toks_per_seq // tm
        mod_spec = pl.BlockSpec((None, 1, D_MODEL), lambda i: (i // tiles_per_seq, 0, 0))
    const = lambda shape: pl.BlockSpec(shape, lambda i: (0,) * len(shape))
    tok = lambda w: pl.BlockSpec((tm, w), lambda i: (i, 0))
    return pl.pallas_call(
        _out_kernel,
        grid=(n // tm,),
        in_specs=[tok(W_A), tok(W_A), tok(W_B), tok(D_MODEL), mod_spec, mod_spec, mod_spec,
                  const((1, W_A)), const((1, W_B)), const((1, D_MODEL)), const((W_B, W_B)),
                  const((W_A + W_B, D_MODEL))],
        out_specs=[tok(D_MODEL), tok(D_MODEL)],
        out_shape=[jax.ShapeDtypeStruct((n, D_MODEL), F32)] * 2,
        compiler_params=_cparams(("arbitrary",)),
        name="out_proj",
    )(ha, mo, att, x2, gate1, scale2, shift2, mg, ag, g2, bd, wo)


def _top16(s, nrows):
    iota = lax.broadcasted_iota(jnp.int32, s.shape, 0).astype(F32)
    vals, idxs = [], []
    for _ in range(PEER_TOPK):
        m = jnp.max(s, axis=0, keepdims=True)
        pos = jnp.min(jnp.where(s == m, iota, float(nrows)), axis=0, keepdims=True)
        vals.append(m)
        idxs.append(pos)
        s = jnp.where(iota == pos, -jnp.inf, s)
    return jnp.concatenate(vals, axis=0), jnp.concatenate(idxs, axis=0)


def _peer_route_kernel(h2_ref, wq_ref, keys_ref, eidx_ref, gate_ref, sv_s, si_s, *, tm):
    qh = jnp.dot(h2_ref[...].astype(BF16), wq_ref[...], preferred_element_type=F32)
    for hp in range(2 * PEER_HEADS):
        qs = qh[:, hp * N_SUBKEYS:(hp + 1) * N_SUBKEYS]
        s = _nt_dot(keys_ref[hp % 2], qs, precision=HIGHEST)
        v, i = _top16(s, N_SUBKEYS)
        sv_s[hp] = v
        si_s[hp] = i

    iota = lax.broadcasted_iota(jnp.int32, (PEER_TOPK * PEER_TOPK, tm), 0).astype(F32)

    def head(h, carry):
        sv0 = sv_s[2 * h]
        sv1 = sv_s[2 * h + 1]
        si0 = si_s[2 * h]
        si1 = si_s[2 * h + 1]
        cand = jnp.concatenate([sv0[a:a + 1, :] + sv1 for a in range(PEER_TOPK)], axis=0)
        cidx = jnp.concatenate([si0[a:a + 1, :] * float(N_SUBKEYS) + si1 for a in range(PEER_TOPK)], axis=0)
        fv, ev = [], []
        for _ in range(PEER_TOPK):
            m = jnp.max(cand, axis=0, keepdims=True)
            pos = jnp.min(jnp.where(cand == m, iota, float(PEER_TOPK * PEER_TOPK)), axis=0, keepdims=True)
            sel = iota == pos
            ev.append(jnp.max(jnp.where(sel, cidx, -1.0), axis=0, keepdims=True))
            fv.append(m)
            cand = jnp.where(sel, -jnp.inf, cand)
        fvs = jnp.concatenate(fv, axis=0)
        e = jnp.exp(fvs - fv[0])
        rs = pl.ds(pl.multiple_of(h * PEER_TOPK, PEER_TOPK), PEER_TOPK)
        gate_ref[rs, :] = e / jnp.sum(e, axis=0, keepdims=True)
        eidx_ref[rs, :] = jnp.concatenate(ev, axis=0).astype(jnp.int32)
        return carry

    lax.fori_loop(0, PEER_HEADS, head, 0)


def _peer_route(h2, wq, keys, tm):
    n = h2.shape[0]
    kern = functools.partial(_peer_route_kernel, tm=tm)
    return pl.pallas_call(
        kern,
        grid=(n // tm,),
        in_specs=[pl.BlockSpec((tm, D_MODEL), lambda i: (i, 0)),
                  pl.BlockSpec((D_MODEL, 2 * PEER_HEADS * N_SUBKEYS), lambda i: (0, 0)),
                  pl.BlockSpec((2, N_SUBKEYS, N_SUBKEYS), lambda i: (0, 0, 0))],
        out_specs=[pl.BlockSpec((PEER_PAIRS, tm), lambda i: (0, i)), pl.BlockSpec((PEER_PAIRS, tm), lambda i: (0, i))],
        out_shape=[jax.ShapeDtypeStruct((PEER_PAIRS, n), jnp.int32), jax.ShapeDtypeStruct((PEER_PAIRS, n), F32)],
        scratch_shapes=[pltpu.VMEM((2 * PEER_HEADS, PEER_TOPK, tm), F32)] * 2,
        compiler_params=_cparams(("arbitrary",)),
        name="peer_route",
    )(h2, wq, keys)


def _pack_table(tab):
    e = tab.shape[0]
    bits = lax.bitcast_convert_type(tab.astype(BF16), jnp.uint16).astype(jnp.uint32)
    word = (bits[:, D_MODEL // 2:] << 16) | bits[:, :D_MODEL // 2]
    return lax.bitcast_convert_type(word, jnp.int32).reshape(e, 4, LANES)


def _unpack(word):
    lo = lax.bitcast_convert_type(word << 16, F32)
    hi = lax.bitcast_convert_type(word & jnp.int32(-65536), F32)
    return lo, hi


def _peer_act_kernel(idx_ref, x_ref, gate_ref, tab_ref, w_ref, m_s, *, tb):
    lane = lax.broadcasted_iota(jnp.int32, (PEER_PAIRS, tb), 1)

    def token(t, act):
        x = x_ref[t]
        x_lo = x[0:4, :]
        x_hi = x[4:8, :]
        for p in range(PEER_PAIRS):
            lo, hi = _unpack(tab_ref[idx_ref[p, t]])
            m_s[p:p + 1, :] = jnp.sum(lo * x_lo + hi * x_hi, axis=0, keepdims=True)
        dots = jnp.sum(m_s[...], axis=1, keepdims=True)
        return jnp.where(lane == t, dots, act)

    act = lax.fori_loop(0, tb, token, jnp.zeros((PEER_PAIRS, tb), F32))
    gelu = 0.5 * act * (1.0 + lax.erf(act * (2.0 ** -0.5)))
    w_ref[...] = gate_ref[...] * gelu


def _peer_act(eidx, x3, gate, tab, tb):
    n = x3.shape[0]
    kern = functools.partial(_peer_act_kernel, tb=tb)
    return pl.pallas_call(
        kern,
        grid=(n // tb,),
        in_specs=[pl.BlockSpec((PEER_PAIRS, tb), lambda i: (0, i), memory_space=pltpu.SMEM),
                  pl.BlockSpec((tb, 8, LANES), lambda i: (i, 0, 0)),
                  pl.BlockSpec((PEER_PAIRS, tb), lambda i: (0, i)),
                  pl.BlockSpec(tab.shape, lambda i: (0, 0, 0), pipeline_mode=pl.Buffered(1))],
        out_specs=pl.BlockSpec((PEER_PAIRS, tb), lambda i: (0, i)),
        out_shape=jax.ShapeDtypeStruct((PEER_PAIRS, n), F32),
        scratch_shapes=[pltpu.VMEM((PEER_PAIRS, LANES), F32)],
        compiler_params=_cparams(("arbitrary",)),
        name="peer_act",
    )(eidx, x3, gate, tab)


def _peer_mix_kernel(idx_ref, w_ref, y_ref, g2_ref, tab_ref, o_ref, *, tb, per_token_mod):
    def token(t, carry):
        accs = [jnp.zeros((4, LANES), F32) for _ in range(4)]
        for p in range(PEER_PAIRS):
            lo, hi = _unpack(tab_ref[idx_ref[p, t]])
            w = w_ref[p, t]
            k = 2 * (p % 2)
            accs[k] = accs[k] + w * lo
            accs[k + 1] = accs[k + 1] + w * hi
        out = jnp.concatenate([accs[0] + accs[2], accs[1] + accs[3]], axis=0)
        g2 = g2_ref[t] if per_token_mod else g2_ref[0]
        o_ref[t] = y_ref[t] + g2 * out
        return carry

    lax.fori_loop(0, tb, token, 0)


def _peer_mix(eidx, w, y3, gate2, per_token_mod, toks_per_seq, tab, tb):
    n = y3.shape[0]
    if per_token_mod:
        g_spec = pl.BlockSpec((tb, 8, LANES), lambda i: (i, 0, 0))
    else:
        blocks_per_seq = toks_per_seq // tb
        g_spec = pl.BlockSpec((1, 8, LANES), lambda i: (i // blocks_per_seq, 0, 0))
    smem = pl.BlockSpec((PEER_PAIRS, tb), lambda i: (0, i), memory_space=pltpu.SMEM)
    kern = functools.partial(_peer_mix_kernel, tb=tb, per_token_mod=per_token_mod)
    return pl.pallas_call(
        kern,
        grid=(n // tb,),
        in_specs=[smem, smem, pl.BlockSpec((tb, 8, LANES), lambda i: (i, 0, 0)), g_spec,
                  pl.BlockSpec(tab.shape, lambda i: (0, 0, 0), pipeline_mode=pl.Buffered(1))],
        out_specs=pl.BlockSpec((tb, 8, LANES), lambda i: (i, 0, 0)),
        out_shape=jax.ShapeDtypeStruct((n, 8, LANES), F32),
        compiler_params=_cparams(("arbitrary",)),
        name="peer_mix",
    )(eidx, w, y3, gate2, tab)


def _layer(x, mod, conv_buf, C0, n0, m0, k_buf, v_buf, rel_bias, wts, tm, tb):
    B, T, _ = x.shape
    n = B * T
    x2 = x.reshape(n, D_MODEL)
    shift1, scale1, gate1, shift2, scale2, gate2 = jnp.split(mod, 6, axis=-1)
    per_token = T % tm != 0
    if per_token:
        expand = lambda a: jnp.repeat(a, T, axis=0)
    else:
        expand = lambda a: a.reshape(B, 1, D_MODEL)

    q, k, v, mqk, mv, mo, gcol, grow = _in_proj(
        x2, expand(scale1), expand(shift1), per_token, T, tm, wts["g1"], wts["wm"], wts["wgc"], wts["wgr"],
        wts["bd"], wts["qg"], wts["kg"])

    if k_buf is None:
        att = _att_prompt(q, k, v, _prompt_bias(rel_bias), B, T)
        k_new = k.reshape(B, T, H_B, HD_B)
        v_new = v.reshape(B, T, H_B, HD_B)
    else:
        wbuf = k_buf.shape[1]
        bias_c, bias_n = _sample_bias(rel_bias, T, wbuf)
        att, k_new, v_new = _att_sample(q, k, v, k_buf.reshape(B, wbuf, W_B), v_buf.reshape(B, wbuf, W_B),
                                        bias_c, bias_n, B, T, wbuf)
        k_new = k_new.reshape(B, wbuf, H_B, HD_B)
        v_new = v_new.reshape(B, wbuf, H_B, HD_B)

    ha, C, nn, m = _mlstm(mqk, mv, gcol, grow, wts["conv_w"], wts["conv_b"], conv_buf, wts["b_gate"], C0, n0, m0, B, T)
    mqk3 = mqk.reshape(B, T, 2 * W_A)
    if T >= CONV_W - 1:
        conv_new = mqk3[:, T - (CONV_W - 1):]
    else:
        conv_new = jnp.concatenate([conv_buf, mqk3], axis=1)[:, -(CONV_W - 1):]

    y1, h2 = _out_proj(ha, mo, att, x2, expand(gate1), expand(scale2), expand(shift2), per_token, T, tm,
                       wts["mg"], wts["ag"], wts["g2"], wts["bd"], wts["wo"])

    eidx, gate = _peer_route(h2, wts["wq"], wts["keys"], tb)
    w = _peer_act(eidx, h2.reshape(n, 8, LANES), gate, wts["u_tab"], tb)
    if per_token:
        g2e = jnp.repeat(gate2, T, axis=0).reshape(n, 8, LANES)
    else:
        g2e = gate2.reshape(B, 8, LANES)
    y = _peer_mix(eidx, w, y1.reshape(n, 8, LANES), g2e, per_token, T, wts["v_tab"], tb)
    return (y.reshape(B, T, D_MODEL), k_new, v_new, conv_new, C, nn.reshape(B, H_A, DK_A), m[:, :, 0, 0])


def _prep_weights(l, norm1_g, norm2_g, w_in, b_gate, conv_w, conv_b, q_norm_g, k_norm_g, att_out_g, mlstm_out_g,
                  w_out, peer_wq, peer_keys, peer_u, peer_v):
    w = w_in[l]
    wg = w[:, W_MAIN:]
    grp = np.arange(W_B) // HD_B
    bd = jnp.asarray((grp[:, None] == grp[None, :]).astype(np.float32) / HD_B, BF16)
    return dict(
        g1=norm1_g[l].reshape(1, -1), g2=norm2_g[l].reshape(1, -1),
        wm=w[:, :W_MAIN].astype(BF16),
        wgc=jnp.pad(wg, ((0, 0), (0, LANES - 2 * H_A))), wgr=wg.T,
        bd=bd, qg=jnp.tile(q_norm_g[l], H_B).reshape(1, -1), kg=jnp.tile(k_norm_g[l], H_B).reshape(1, -1),
        conv_w=conv_w[l], conv_b=conv_b[l], b_gate=b_gate[l],
        mg=mlstm_out_g[l].reshape(1, -1), ag=att_out_g[l].reshape(1, -1),
        wo=w_out[l].astype(BF16), wq=peer_wq[l].astype(BF16), keys=peer_keys[l],
        u_tab=_pack_table(peer_u[l]), v_tab=_pack_table(peer_v[l]),
    )


def kernel(x_prompt, x_sample, cache_k, cache_v, state_conv, state_C, state_n, state_m, c_prompt, c_sample,
           rel_bias, w_ada, b_ada, norm1_g, norm2_g, w_in, b_gate, conv_w, conv_b, q_norm_g, k_norm_g,
           att_out_g, mlstm_out_g, w_out, peer_wq, peer_keys, peer_u, peer_v):
    depth = w_ada.shape[0]
    bp = x_prompt.shape[0]
    bs = x_sample.shape[0]
    yp, ys = x_prompt, x_sample
    sp, ss = [], []
    for l in range(depth):
        wts = _prep_weights(l, norm1_g, norm2_g, w_in, b_gate, conv_w, conv_b, q_norm_g, k_norm_g, att_out_g,
                            mlstm_out_g, w_out, peer_wq, peer_keys, peer_u, peer_v)
        mod = _ada(jnp.concatenate([c_prompt, c_sample], axis=0), w_ada[l], b_ada[l])
        zc = jnp.zeros((bp, CONV_W - 1, 2 * W_A), F32)
        zC = jnp.zeros((bp, H_A, DK_A, DK_A), F32)
        zn = jnp.zeros((bp, H_A, DK_A), F32)
        zm = jnp.zeros((bp, H_A), F32)
        outp = _layer(yp, mod[:bp], zc, zC, zn, zm, None, None, rel_bias, wts, tm=256, tb=128)
        outs = _layer(ys, mod[bp:], state_conv[l], state_C[l], state_n[l], state_m[l], cache_k[l], cache_v[l],
                      rel_bias, wts, tm=256, tb=128)
        yp, ys = outp[0], outs[0]
        sp.append(outp[1:])
        ss.append(outs[1:])
    k_p, v_p, conv_p, C_p, n_p, m_p = [jnp.stack([s[i] for s in sp], axis=0) for i in range(6)]
    k_s, v_s, conv_s, C_s, n_s, m_s = [jnp.stack([s[i] for s in ss], axis=0) for i in range(6)]
    return (yp, ys, k_p, v_p, conv_p, C_p, n_p, m_p, k_s, v_s, conv_s, C_s, n_s, m_s)
```

```python
import functools
import math

import numpy as np
import jax
import jax.numpy as jnp
from jax import lax
from jax.experimental import pallas as pl
from jax.experimental.pallas import tpu as pltpu

F32 = jnp.float32
BF16 = jnp.bfloat16
HIGHEST = lax.Precision.HIGHEST

D_MODEL = 1024
H_A = 4
DK_A = 128
W_A = H_A * DK_A
H_B = 8
HD_B = 64
W_B = H_B * HD_B
CONV_W = 4
MLSTM_CHUNK = 64
DIL_PATTERNS = ((128, 1), (512, 4), (2048, 16))
ATT_STEPS = 128
N_BUCKETS = 32
MAX_DIST = 2048
PEER_HEADS = 8
N_SUBKEYS = 128
PEER_TOPK = 16
PEER_PAIRS = PEER_HEADS * PEER_TOPK
EPS = 1e-6
NEG = -1e30
W_MAIN = 3 * W_B + 4 * W_A
LANES = 128
VMEM_LIMIT = 56 * 1024 * 1024


def _cparams(sem):
    return pltpu.CompilerParams(dimension_semantics=sem, vmem_limit_bytes=VMEM_LIMIT)


def _nt_dot(a, b, precision=None):
    return lax.dot_general(a, b, (((1,), (1,)), ((), ())), preferred_element_type=F32, precision=precision)


def _t5_bucket(dist):
    max_exact = N_BUCKETS // 2
    d = np.maximum(dist, 1).astype(np.float32)
    large = max_exact + (np.log(d / max_exact) / math.log(MAX_DIST / max_exact) * (N_BUCKETS - max_exact)).astype(np.int32)
    large = np.minimum(large, N_BUCKETS - 1)
    return np.where(dist < max_exact, dist, large).astype(np.int32)


def _ada_kernel(c_ref, w_ref, b_ref, o_ref):
    c = c_ref[...]
    s = c * jax.nn.sigmoid(c)
    o_ref[...] = jnp.dot(s, w_ref[...], preferred_element_type=F32, precision=HIGHEST) + b_ref[...]


def _ada(c_all, w_ada, b_ada):
    n = c_all.shape[0]
    return pl.pallas_call(
        _ada_kernel,
        grid=(6,),
        in_specs=[pl.BlockSpec((n, D_MODEL), lambda j: (0, 0)),
                  pl.BlockSpec((D_MODEL, D_MODEL), lambda j: (0, j)),
                  pl.BlockSpec((1, D_MODEL), lambda j: (0, j))],
        out_specs=pl.BlockSpec((n, D_MODEL), lambda j: (0, j)),
        out_shape=jax.ShapeDtypeStruct((n, 6 * D_MODEL), F32),
        compiler_params=_cparams(("arbitrary",)),
        name="ada",
    )(c_all, w_ada, b_ada.reshape(1, -1))


def _group_mean_sq(a, bd):
    sq = a * a
    hi = sq.astype(BF16)
    lo = (sq - hi.astype(F32)).astype(BF16)
    return jnp.dot(hi, bd, preferred_element_type=F32) + jnp.dot(lo, bd, preferred_element_type=F32)


def _in_kernel(x_ref, sc_ref, sh_ref, g1_ref, wm_ref, wgc_ref, wgr_ref, bd_ref, qg_ref, kg_ref,
               q_ref, k_ref, v_ref, mqk_ref, mv_ref, mo_ref, gcol_ref, grow_ref):
    x = x_ref[...]
    ms = jnp.mean(x * x, axis=-1, keepdims=True)
    h = x * lax.rsqrt(ms + EPS) * g1_ref[...]
    h = h * (1.0 + sc_ref[...]) + sh_ref[...]
    y = jnp.dot(h.astype(BF16), wm_ref[...], preferred_element_type=F32)
    bd = bd_ref[...]
    aq = y[:, 0:W_B]
    ak = y[:, W_B:2 * W_B]
    q_ref[...] = aq * lax.rsqrt(_group_mean_sq(aq, bd) + EPS) * qg_ref[...] * (HD_B ** -0.5)
    k_ref[...] = ak * lax.rsqrt(_group_mean_sq(ak, bd) + EPS) * kg_ref[...]
    v_ref[...] = y[:, 2 * W_B:3 * W_B]
    o = 3 * W_B
    mqk_ref[...] = y[:, o:o + 2 * W_A]
    mv_ref[...] = y[:, o + 2 * W_A:o + 3 * W_A]
    mo_ref[...] = y[:, o + 3 * W_A:o + 4 * W_A]
    gcol_ref[...] = jnp.dot(h, wgc_ref[...], preferred_element_type=F32, precision=HIGHEST)
    grow_ref[...] = _nt_dot(wgr_ref[...], h, precision=HIGHEST)


def _in_proj(x2, scale, shift, per_token_mod, toks_per_seq, tm, g1, wm, wgc, wgr, bd, qg, kg):
    n = x2.shape[0]
    nt = n // tm
    if per_token_mod:
        mod_spec = pl.BlockSpec((tm, D_MODEL), lambda i: (i, 0))
    else:
        tiles_per_seq = toks_per_seq // tm
        mod_spec = pl.BlockSpec((None, 1, D_MODEL), lambda i: (i // tiles_per_seq, 0, 0))
    const = lambda shape: pl.BlockSpec(shape, lambda i: (0,) * len(shape))
    tok = lambda w: pl.BlockSpec((tm, w), lambda i: (i, 0))
    outs = pl.pallas_call(
        _in_kernel,
        grid=(nt,),
        in_specs=[tok(D_MODEL), mod_spec, mod_spec, const((1, D_MODEL)), const((D_MODEL, W_MAIN)),
                  const((D_MODEL, LANES)), const((8, D_MODEL)), const((W_B, W_B)), const((1, W_B)), const((1, W_B))],
        out_specs=[tok(W_B), tok(W_B), tok(W_B), tok(2 * W_A), tok(W_A), tok(W_A), tok(LANES),
                   pl.BlockSpec((8, tm), lambda i: (0, i))],
        out_shape=[jax.ShapeDtypeStruct((n, W_B), F32)] * 3 + [jax.ShapeDtypeStruct((n, 2 * W_A), F32)]
                  + [jax.ShapeDtypeStruct((n, W_A), F32)] * 2 + [jax.ShapeDtypeStruct((n, LANES), F32),
                                                                 jax.ShapeDtypeStruct((8, n), F32)],
        compiler_params=_cparams(("arbitrary",)),
        name="in_proj",
    )(x2, scale, shift, g1, wm, wgc, wgr, bd, qg, kg)
    return outs


def _log_sigmoid(x):
    return jnp.minimum(x, 0.0) - jnp.log1p(jnp.exp(-jnp.abs(x)))


def _mlstm_kernel(bg_ref, m0_ref, mq_ref, mk_ref, cwq_ref, cwk_ref, cbq_ref, cbk_ref, bufq_ref, bufk_ref,
                  v_ref, gcol_ref, grow_ref, c0_ref, n0_ref,
                  h_ref, c_ref, n_ref, m_ref, sq, sk, *, T, L, rows_whole):
    b = pl.program_id(0)
    hd = pl.program_id(1)
    nc = T // L

    def conv(u_ref, buf_ref, w_ref, cb_ref, s_ref, out_ref):
        s_ref[0:8, :] = jnp.zeros((8, LANES), F32)
        s_ref[5:8, :] = buf_ref[...]
        s_ref[8:8 + T, :] = u_ref[...]
        y = cb_ref[...]
        for j in range(CONV_W):
            y = y + s_ref[5 + j:5 + j + T, :] * w_ref[j:j + 1, :]
        out_ref[...] = y * jax.nn.sigmoid(y)

    conv(mq_ref, bufq_ref, cwq_ref, cbq_ref, sq, h_ref)
    conv(mk_ref, bufk_ref, cwk_ref, cbk_ref, sk, sq.at[8:8 + T, :])

    ig_b = bg_ref[hd]
    f_b = bg_ref[H_A + hd]
    lane = lax.broadcasted_iota(jnp.int32, (L, LANES), 1)
    ri = lax.broadcasted_iota(jnp.int32, (L, L), 0)
    ci = lax.broadcasted_iota(jnp.int32, (L, L), 1)
    causal = ri >= ci
    tri = causal.astype(F32)
    tri_t = (ri <= ci).astype(F32)

    def chunk(c, carry):
        C, n, m = carry
        r0 = pl.multiple_of(c * L, L)
        qc = h_ref[pl.ds(r0, L), :]
        kc = sq[pl.ds(r0 + 8, L), :] * (DK_A ** -0.5)
        vc = v_ref[pl.ds(r0, L), :]
        g = gcol_ref[pl.ds(r0, L), :]
        ig_col = jnp.sum(jnp.where(lane == hd, g, 0.0), axis=1, keepdims=True) + ig_b
        f_col = jnp.sum(jnp.where(lane == hd + H_A, g, 0.0), axis=1, keepdims=True) + f_b
        lf_col = _log_sigmoid(f_col)
        rc = c + (b * nc if rows_whole else 0)
        ig_row = grow_ref[hd, pl.ds(rc, 1), :] + ig_b
        lf_row = _log_sigmoid(grow_ref[hd + H_A, pl.ds(rc, 1), :] + f_b)
        b_col = jnp.dot(tri, jnp.broadcast_to(lf_col, (L, LANES)), preferred_element_type=F32,
                        precision=HIGHEST)[:, 0:1]
        b_row = jnp.dot(jnp.broadcast_to(lf_row, (8, L)), tri_t, preferred_element_type=F32,
                        precision=HIGHEST)[0:1, :]
        g_col = b_col + m
        dm = jnp.where(causal, b_col - b_row + ig_row, -jnp.inf)
        mt = jnp.maximum(g_col, jnp.max(dm, axis=1, keepdims=True))
        wd = jnp.exp(dm - mt)
        wg = jnp.exp(g_col - mt)
        qb = qc.astype(BF16)
        kb = kc.astype(BF16)
        vb = vc.astype(BF16)
        s = _nt_dot(qb, kb) * wd
        num = wg * jnp.dot(qb, C.astype(BF16), preferred_element_type=F32) \
            + jnp.dot(s.astype(BF16), vb, preferred_element_type=F32)
        den = wg * jnp.sum(qc * n, axis=1, keepdims=True) + jnp.sum(s, axis=1, keepdims=True)
        h_ref[pl.ds(r0, L), :] = num / jnp.maximum(jnp.abs(den), jnp.exp(-mt))
        bl = b_col[L - 1:L, :]
        ws_log_col = bl - b_col + ig_col
        ws_log_row = bl - b_row + ig_row
        m_new = jnp.maximum(bl + m, jnp.max(ws_log_row, axis=1, keepdims=True))
        ws_col = jnp.exp(ws_log_col - m_new)
        wc = jnp.exp(bl + m - m_new)
        kw = ws_col * kc
        upd = lax.dot_general(kw.astype(BF16), vb, (((0,), (0,)), ((), ())), preferred_element_type=F32)
        return wc * C + upd, wc * n + jnp.sum(kw, axis=0, keepdims=True), m_new

    m0 = jnp.full((1, 1), m0_ref[b, hd], F32)
    C, n, m = lax.fori_loop(0, nc, chunk, (c0_ref[...], n0_ref[...], m0))
    c_ref[...] = C
    n_ref[...] = n
    m_ref[...] = jnp.broadcast_to(m, (1, LANES))


def _mlstm(mqk, mv, gcol, grow, conv_w, conv_b, conv_buf, b_gate, C0, n0, m0, B, T):
    L = math.gcd(T, MLSTM_CHUNK)
    nc = T // L
    n = B * T
    rows_whole = nc % 8 != 0
    grow3 = grow.reshape(8, n // L, L)
    if rows_whole:
        grow_spec = pl.BlockSpec((8, n // L, L), lambda b, h: (0, 0, 0))
    else:
        grow_spec = pl.BlockSpec((8, nc, L), lambda b, h: (0, b, 0))
    smem = pl.BlockSpec(memory_space=pltpu.SMEM)
    colq = lambda rows: pl.BlockSpec((rows, LANES), lambda b, h: (0, h))
    colk = lambda rows: pl.BlockSpec((rows, LANES), lambda b, h: (0, h + H_A))
    st = lambda r, c: pl.BlockSpec((None, None, r, c), lambda b, h: (b, h, 0, 0))
    kern = functools.partial(_mlstm_kernel, T=T, L=L, rows_whole=rows_whole)
    return pl.pallas_call(
        kern,
        grid=(B, H_A),
        in_specs=[smem, smem,
                  pl.BlockSpec((T, LANES), lambda b, h: (b, h)), pl.BlockSpec((T, LANES), lambda b, h: (b, h + H_A)),
                  colq(CONV_W), colk(CONV_W), colq(1), colk(1),
                  pl.BlockSpec((None, CONV_W - 1, LANES), lambda b, h: (b, 0, h)),
                  pl.BlockSpec((None, CONV_W - 1, LANES), lambda b, h: (b, 0, h + H_A)),
                  pl.BlockSpec((T, LANES), lambda b, h: (b, h)),
                  pl.BlockSpec((T, LANES), lambda b, h: (b, 0)),
                  grow_spec, st(DK_A, DK_A), st(1, DK_A)],
        out_specs=[pl.BlockSpec((T, LANES), lambda b, h: (b, h)), st(DK_A, DK_A), st(1, DK_A), st(1, LANES)],
        out_shape=[jax.ShapeDtypeStruct((n, W_A), F32), jax.ShapeDtypeStruct((B, H_A, DK_A, DK_A), F32),
                   jax.ShapeDtypeStruct((B, H_A, 1, DK_A), F32), jax.ShapeDtypeStruct((B, H_A, 1, LANES), F32)],
        scratch_shapes=[pltpu.VMEM((T + 8, LANES), F32), pltpu.VMEM((T + 8, LANES), F32)],
        compiler_params=_cparams(("arbitrary", "arbitrary")),
        name="mlstm",
    )(b_gate, m0, mqk, mqk, conv_w, conv_w, conv_b.reshape(1, -1), conv_b.reshape(1, -1), conv_buf, conv_buf,
      mv, gcol, grow3, C0, n0.reshape(B, H_A, 1, DK_A))


def _att_prompt_kernel(q_ref, k_ref, v_ref, bias_ref, o_ref, num_s, mx_s, dn_s, *, T):
    lane = lax.broadcasted_iota(jnp.int32, (ATT_STEPS, LANES), 1)
    first = lane < HD_B
    nblk = T // ATT_STEPS

    for br, (_, dil) in enumerate(DIL_PATTERNS):
        span = ATT_STEPS * dil
        has_prev = T > span

        def rows(start, dil=dil):
            return pl.ds(start, ATT_STEPS, stride=dil) if dil > 1 else pl.ds(start, ATT_STEPS)

        def blk(i, carry, br=br, dil=dil, span=span, has_prev=has_prev, rows=rows):
            r = i % dil
            nb = i // dil
            start = nb * span + r
            qb = q_ref[rows(start), :]
            kcur = k_ref[rows(start), :].astype(BF16)
            vcur = v_ref[rows(start), :].astype(BF16)
            if has_prev:
                pstart = jnp.maximum(start - span, 0)
                kprev = k_ref[rows(pstart), :].astype(BF16)
                vprev = v_ref[rows(pstart), :].astype(BF16)
            res = []
            for hh in range(2):
                qh = jnp.where(first if hh == 0 else ~first, qb, 0.0).astype(BF16)
                lc = _nt_dot(qh, kcur) + bias_ref[br, hh, :, ATT_STEPS:2 * ATT_STEPS]
                mx = jnp.max(lc, axis=1, keepdims=True)
                if has_prev:
                    lp = _nt_dot(qh, kprev) + bias_ref[br, hh, :, 0:ATT_STEPS]
                    lp = jnp.where(nb > 0, lp, NEG)
                    mx = jnp.maximum(mx, jnp.max(lp, axis=1, keepdims=True))
                pc = jnp.exp(lc - mx)
                den = jnp.sum(pc, axis=1, keepdims=True)
                num = jnp.dot(pc.astype(BF16), vcur, preferred_element_type=F32)
                if has_prev:
                    pp = jnp.exp(lp - mx)
                    den = den + jnp.sum(pp, axis=1, keepdims=True)
                    num = num + jnp.dot(pp.astype(BF16), vprev, preferred_element_type=F32)
                res.append((num, mx, den))
            num_s[br, rows(start), :] = jnp.where(first, res[0][0], res[1][0])
            mx_s[br, rows(start), :] = jnp.where(first, res[0][1], res[1][1])
            dn_s[br, rows(start), :] = jnp.where(first, res[0][2], res[1][2])
            return carry

        lax.fori_loop(0, nblk, blk, 0)

    def comb(i, carry):
        rs = pl.ds(pl.multiple_of(i * ATT_STEPS, ATT_STEPS), ATT_STEPS)
        m = jnp.maximum(jnp.maximum(mx_s[0, rs, :], mx_s[1, rs, :]), mx_s[2, rs, :])
        num = jnp.zeros((ATT_STEPS, LANES), F32)
        den = jnp.zeros((ATT_STEPS, LANES), F32)
        for br in range(3):
            e = jnp.exp(mx_s[br, rs, :] - m)
            num = num + e * num_s[br, rs, :]
            den = den + e * dn_s[br, rs, :]
        o_ref[rs, :] = num / den
        return carry

    lax.fori_loop(0, nblk, comb, 0)


def _att_prompt(q, k, v, bias, B, T):
    n = B * T
    blk = pl.BlockSpec((T, LANES), lambda b, p: (b, p))
    kern = functools.partial(_att_prompt_kernel, T=T)
    return pl.pallas_call(
        kern,
        grid=(B, H_B // 2),
        in_specs=[blk, blk, blk, pl.BlockSpec((3, 2, ATT_STEPS, 2 * ATT_STEPS), lambda b, p: (0, p, 0, 0))],
        out_specs=blk,
        out_shape=jax.ShapeDtypeStruct((n, W_B), F32),
        scratch_shapes=[pltpu.VMEM((3, T, LANES), F32)] * 3,
        compiler_params=_cparams(("arbitrary", "arbitrary")),
        name="att_prompt",
    )(q, k, v, bias)


def _prompt_bias(rel_bias):
    qi = np.arange(ATT_STEPS)[:, None]
    kj = np.arange(2 * ATT_STEPS)[None, :]
    jstep = ATT_STEPS + qi - kj
    band = (jstep >= 0) & (jstep <= ATT_STEPS)
    tabs = []
    for _, dil in DIL_PATTERNS:
        bucket = _t5_bucket(np.clip(jstep, 0, ATT_STEPS) * dil)
        t = jnp.moveaxis(rel_bias[bucket].astype(F32), -1, 0)
        tabs.append(jnp.where(band[None], t, NEG))
    return jnp.stack(tabs, axis=0)


def _att_sample_kernel(q_ref, kn_ref, vn_ref, ck_ref, cv_ref, bc_ref, bn_ref, o_ref, ko_ref, vo_ref, *, S, WBUF):
    lane = lax.broadcasted_iota(jnp.int32, (S, LANES), 1)
    first = lane < HD_B
    for hp in range(H_B // 2):
        cs = slice(hp * LANES, (hp + 1) * LANES)
        kc = ck_ref[:, cs].astype(BF16)
        vc = cv_ref[:, cs].astype(BF16)
        kn = kn_ref[:, cs].astype(BF16)
        vn = vn_ref[:, cs].astype(BF16)
        qp = q_ref[:, cs]
        res = []
        for hh in range(2):
            h = 2 * hp + hh
            qh = jnp.where(first if hh == 0 else ~first, qp, 0.0).astype(BF16)
            lc = _nt_dot(qh, kc) + bc_ref[h]
            ln = _nt_dot(qh, kn) + bn_ref[h]
            mx = jnp.maximum(jnp.max(lc, axis=1, keepdims=True), jnp.max(ln, axis=1, keepdims=True))
            pc = jnp.exp(lc - mx)
            pn = jnp.exp(ln - mx)
            den = jnp.sum(pc, axis=1, keepdims=True) + jnp.sum(pn, axis=1, keepdims=True)
            num = jnp.dot(pc.astype(BF16), vc, preferred_element_type=F32) \
                + jnp.dot(pn.astype(BF16), vn, preferred_element_type=F32)
            res.append(num / den)
        o_ref[:, cs] = jnp.where(first, res[0], res[1])
    ko_ref[0:WBUF - S, :] = ck_ref[S:WBUF, :]
    ko_ref[WBUF - S:WBUF, :] = kn_ref[...]
    vo_ref[0:WBUF - S, :] = cv_ref[S:WBUF, :]
    vo_ref[WBUF - S:WBUF, :] = vn_ref[...]


def _att_sample(q, kn, vn, ck, cv, bias_c, bias_n, B, S, WBUF):
    tokb = pl.BlockSpec((S, W_B), lambda b: (b, 0))
    cache = pl.BlockSpec((None, WBUF, W_B), lambda b: (b, 0, 0))
    kern = functools.partial(_att_sample_kernel, S=S, WBUF=WBUF)
    return pl.pallas_call(
        kern,
        grid=(B,),
        in_specs=[tokb, tokb, tokb, cache, cache,
                  pl.BlockSpec((H_B, S, WBUF), lambda b: (0, 0, 0)), pl.BlockSpec((H_B, S, S), lambda b: (0, 0, 0))],
        out_specs=[tokb, cache, cache],
        out_shape=[jax.ShapeDtypeStruct((B * S, W_B), F32), jax.ShapeDtypeStruct((B, WBUF, W_B), F32),
                   jax.ShapeDtypeStruct((B, WBUF, W_B), F32)],
        compiler_params=_cparams(("arbitrary",)),
        name="att_sample",
    )(q, kn, vn, ck, cv, bias_c, bias_n)


def _sample_bias(rel_bias, S, WBUF):
    dist = np.arange(WBUF + S)
    mult = np.zeros(WBUF + S, np.int64)
    for w, dil in DIL_PATTERNS:
        mult += ((dist % dil == 0) & (dist <= w)).astype(np.int64)
    logm = np.where(mult > 0, np.log(np.maximum(mult, 1)), 0.0).astype(np.float32)
    tab = rel_bias[_t5_bucket(dist)].astype(F32).T + logm[None, :]
    tab = jnp.where((mult > 0)[None, :], tab, NEG)
    s = np.arange(S)[:, None]
    dc = WBUF + s - np.arange(WBUF)[None, :]
    dn = s - np.arange(S)[None, :]
    bias_c = tab[:, dc]
    bias_n = jnp.where((dn >= 0)[None], tab[:, np.maximum(dn, 0)], NEG)
    return bias_c, bias_n


def _out_kernel(ha_ref, mo_ref, att_ref, x_ref, g1_ref, sc_ref, sh_ref, mg_ref, ag_ref, g2_ref, bd_ref, wo_ref,
                y_ref, h2_ref):
    ha = ha_ref[...]
    parts = []
    for hd in range(H_A):
        a = ha[:, hd * DK_A:(hd + 1) * DK_A]
        parts.append(a * lax.rsqrt(jnp.mean(a * a, axis=-1, keepdims=True) + EPS))
    hn = jnp.concatenate(parts, axis=1) * mg_ref[...] * jax.nn.sigmoid(mo_ref[...])
    att = att_ref[...]
    an = att * lax.rsqrt(_group_mean_sq(att, bd_ref[...]) + EPS) * ag_ref[...]
    mix = jnp.dot(hn.astype(BF16), wo_ref[0:W_A, :], preferred_element_type=F32) \
        + jnp.dot(an.astype(BF16), wo_ref[W_A:W_A + W_B, :], preferred_element_type=F32)
    y = x_ref[...] + g1_ref[...] * mix
    y_ref[...] = y
    h2 = y * lax.rsqrt(jnp.mean(y * y, axis=-1, keepdims=True) + EPS) * g2_ref[...]
    h2_ref[...] = h2 * (1.0 + sc_ref[...]) + sh_ref[...]


def _out_proj(ha, mo, att, x2, gate1, scale2, shift2, per_token_mod, toks_per_seq, tm, mg, ag, g2, bd, wo):
    n = x2.shape[0]
    if per_token_mod:
        mod_spec = pl.BlockSpec((tm, D_MODEL), lambda i: (i, 0))
    else:
        tiles_per_seq = toks_per_seq // tm
        mod_spec = pl.BlockSpec((None, 1, D_MODEL), lambda i: (i // tiles_per_seq, 0, 0))
    const = lambda shape: pl.BlockSpec(shape, lambda i: (0,) * len(shape))
    tok = lambda w: pl.BlockSpec((tm, w), lambda i: (i, 0))
    return pl.pallas_call(
        _out_kernel,
        grid=(n // tm,),
        in_specs=[tok(W_A), tok(W_A), tok(W_B), tok(D_MODEL), mod_spec, mod_spec, mod_spec,
                  const((1, W_A)), const((1, W_B)), const((1, D_MODEL)), const((W_B, W_B)),
                  const((W_A + W_B, D_MODEL))],
        out_specs=[tok(D_MODEL), tok(D_MODEL)],
        out_shape=[jax.ShapeDtypeStruct((n, D_MODEL), F32)] * 2,
        compiler_params=_cparams(("arbitrary",)),
        name="out_proj",
    )(ha, mo, att, x2, gate1, scale2, shift2, mg, ag, g2, bd, wo)


def _top16(s, nrows):
    iota = lax.broadcasted_iota(jnp.int32, s.shape, 0).astype(F32)
    vals, idxs = [], []
    for _ in range(PEER_TOPK):
        m = jnp.max(s, axis=0, keepdims=True)
        pos = jnp.min(jnp.where(s == m, iota, float(nrows)), axis=0, keepdims=True)
        vals.append(m)
        idxs.append(pos)
        s = jnp.where(iota == pos, -jnp.inf, s)
    return jnp.concatenate(vals, axis=0), jnp.concatenate(idxs, axis=0)


def _peer_route_kernel(h2_ref, wq_ref, keys_ref, eidx_ref, gate_ref, sv_s, si_s, *, tm):
    qh = jnp.dot(h2_ref[...].astype(BF16), wq_ref[...], preferred_element_type=F32)
    for hp in range(2 * PEER_HEADS):
        qs = qh[:, hp * N_SUBKEYS:(hp + 1) * N_SUBKEYS]
        s = _nt_dot(keys_ref[hp % 2], qs, precision=HIGHEST)
        v, i = _top16(s, N_SUBKEYS)
        sv_s[hp] = v
        si_s[hp] = i

    iota = lax.broadcasted_iota(jnp.int32, (PEER_TOPK * PEER_TOPK, tm), 0).astype(F32)

    def head(h, carry):
        sv0 = sv_s[2 * h]
        sv1 = sv_s[2 * h + 1]
        si0 = si_s[2 * h]
        si1 = si_s[2 * h + 1]
        cand = jnp.concatenate([sv0[a:a + 1, :] + sv1 for a in range(PEER_TOPK)], axis=0)
        cidx = jnp.concatenate([si0[a:a + 1, :] * float(N_SUBKEYS) + si1 for a in range(PEER_TOPK)], axis=0)
        fv, ev = [], []
        for _ in range(PEER_TOPK):
            m = jnp.max(cand, axis=0, keepdims=True)
            pos = jnp.min(jnp.where(cand == m, iota, float(PEER_TOPK * PEER_TOPK)), axis=0, keepdims=True)
            sel = iota == pos
            ev.append(jnp.max(jnp.where(sel, cidx, -1.0), axis=0, keepdims=True))
            fv.append(m)
            cand = jnp.where(sel, -jnp.inf, cand)
        fvs = jnp.concatenate(fv, axis=0)
        e = jnp.exp(fvs - fv[0])
        rs = pl.ds(pl.multiple_of(h * PEER_TOPK, PEER_TOPK), PEER_TOPK)
        gate_ref[rs, :] = e / jnp.sum(e, axis=0, keepdims=True)
        eidx_ref[rs, :] = jnp.concatenate(ev, axis=0).astype(jnp.int32)
        return carry

    lax.fori_loop(0, PEER_HEADS, head, 0)


def _peer_route(h2, wq, keys, tm):
    n = h2.shape[0]
    kern = functools.partial(_peer_route_kernel, tm=tm)
    return pl.pallas_call(
        kern,
        grid=(n // tm,),
        in_specs=[pl.BlockSpec((tm, D_MODEL), lambda i: (i, 0)),
                  pl.BlockSpec((D_MODEL, 2 * PEER_HEADS * N_SUBKEYS), lambda i: (0, 0)),
                  pl.BlockSpec((2, N_SUBKEYS, N_SUBKEYS), lambda i: (0, 0, 0))],
        out_specs=[pl.BlockSpec((PEER_PAIRS, tm), lambda i: (0, i)), pl.BlockSpec((PEER_PAIRS, tm), lambda i: (0, i))],
        out_shape=[jax.ShapeDtypeStruct((PEER_PAIRS, n), jnp.int32), jax.ShapeDtypeStruct((PEER_PAIRS, n), F32)],
        scratch_shapes=[pltpu.VMEM((2 * PEER_HEADS, PEER_TOPK, tm), F32)] * 2,
        compiler_params=_cparams(("arbitrary",)),
        name="peer_route",
    )(h2, wq, keys)


ROW_WORDS = 4
ROW_CHUNKS = 2 * ROW_WORDS
GROUP = LANES // ROW_CHUNKS


def _pack_table(tab):
    e = tab.shape[0]
    bits = lax.bitcast_convert_type(tab.astype(BF16), jnp.uint16).astype(jnp.uint32).reshape(e, ROW_WORDS, 2, LANES)
    word = (bits[:, :, 1, :] << 16) | bits[:, :, 0, :]
    return lax.bitcast_convert_type(word, jnp.int32)


def _gather_rows(idx_ref, t, tab_ref, g_s):
    for p in range(PEER_PAIRS):
        g_s[ROW_WORDS * p:ROW_WORDS * (p + 1), :] = tab_ref[idx_ref[t, p]]


def _gathered(g_s):
    return pltpu.bitcast(g_s[...], BF16)


def _token_loop(tb, idx_ref, tab_ref, g0_s, g1_s, compute):
    _gather_rows(idx_ref, 0, tab_ref, g0_s)

    def step(i, carry):
        t0 = 2 * i
        _gather_rows(idx_ref, t0 + 1, tab_ref, g1_s)
        compute(t0, g0_s)
        _gather_rows(idx_ref, jnp.minimum(t0 + 2, tb - 1), tab_ref, g0_s)
        compute(t0 + 1, g1_s)
        return carry

    lax.fori_loop(0, tb // 2, step, 0)


def _peer_act_kernel(idx_ref, x_ref, gate_ref, tab_ref, w_ref, g0_s, g1_s, d_s, *, tb):
    ri = lax.broadcasted_iota(jnp.int32, (LANES, LANES), 0)
    ci = lax.broadcasted_iota(jnp.int32, (LANES, LANES), 1)
    eye = ri == ci
    n_groups = PEER_PAIRS // GROUP

    def compute(t, g_s):
        g = _gathered(g_s)
        xrep = jnp.tile(x_ref[t], (GROUP, 1)).astype(BF16)
        z = _nt_dot(g, xrep)
        for gi in range(n_groups):
            zz = jnp.where(eye, z[gi * LANES:(gi + 1) * LANES, :], 0.0)
            d_s[gi, pl.ds(t, 1), :] = jnp.sum(zz, axis=0, keepdims=True)

    _token_loop(tb, idx_ref, tab_ref, g0_s, g1_s, compute)
    act = jnp.zeros((tb, PEER_PAIRS), F32)
    for gi in range(n_groups):
        fold = (ri // ROW_CHUNKS + gi * GROUP == ci).astype(F32)
        act = act + jnp.dot(d_s[gi], fold, preferred_element_type=F32, precision=HIGHEST)
    gelu = 0.5 * act * (1.0 + lax.erf(act * (2.0 ** -0.5)))
    w_ref[...] = gate_ref[...] * gelu


def _table_spec(tab):
    return pl.BlockSpec(tab.shape, lambda i: (0, 0, 0), pipeline_mode=pl.Buffered(1))


def _peer_act(eidx, x3, gate, tab, tb):
    n = x3.shape[0]
    kern = functools.partial(_peer_act_kernel, tb=tb)
    return pl.pallas_call(
        kern,
        grid=(n // tb,),
        in_specs=[pl.BlockSpec((tb, PEER_PAIRS), lambda i: (i, 0), memory_space=pltpu.SMEM),
                  pl.BlockSpec((tb, ROW_CHUNKS, LANES), lambda i: (i, 0, 0)),
                  pl.BlockSpec((tb, PEER_PAIRS), lambda i: (i, 0)),
                  _table_spec(tab)],
        out_specs=pl.BlockSpec((tb, PEER_PAIRS), lambda i: (i, 0)),
        out_shape=jax.ShapeDtypeStruct((n, PEER_PAIRS), F32),
        scratch_shapes=[pltpu.VMEM((PEER_PAIRS * ROW_WORDS, LANES), jnp.int32)] * 2 + [
                        pltpu.VMEM((PEER_PAIRS // GROUP, tb, LANES), F32)],
        compiler_params=_cparams(("arbitrary",)),
        name="peer_act",
    )(eidx, x3, gate, tab)


def _peer_mix_kernel(idx_ref, w_ref, y_ref, g2_ref, tab_ref, o_ref, g0_s, g1_s, wh_s, wl_s, *, tb, per_token_mod):
    kdim = PEER_PAIRS * ROW_CHUNKS
    pi = lax.broadcasted_iota(jnp.int32, (PEER_PAIRS, kdim), 0)
    ki = lax.broadcasted_iota(jnp.int32, (PEER_PAIRS, kdim), 1)
    rep = (ki // ROW_CHUNKS == pi).astype(BF16)
    w = w_ref[...]
    hi = w.astype(BF16)
    lo = (w - hi.astype(F32)).astype(BF16)
    wh_s[...] = jnp.dot(hi, rep, preferred_element_type=F32)
    wl_s[...] = jnp.dot(lo, rep, preferred_element_type=F32)
    sub = lax.broadcasted_iota(jnp.int32, (ROW_CHUNKS, kdim), 0)
    col = lax.broadcasted_iota(jnp.int32, (ROW_CHUNKS, kdim), 1)
    diag = col % ROW_CHUNKS == sub

    def compute(t, g_s):
        g = _gathered(g_s)
        wmat = jnp.concatenate([jnp.where(diag, wh_s[pl.ds(t, 1), :], 0.0),
                                jnp.where(diag, wl_s[pl.ds(t, 1), :], 0.0)], axis=0).astype(BF16)
        res = jnp.dot(wmat, g, preferred_element_type=F32)
        out = res[0:ROW_CHUNKS, :] + res[ROW_CHUNKS:2 * ROW_CHUNKS, :]
        g2 = g2_ref[t] if per_token_mod else g2_ref[0]
        o_ref[t] = y_ref[t] + g2 * out

    _token_loop(tb, idx_ref, tab_ref, g0_s, g1_s, compute)


def _peer_mix(eidx, w, y3, gate2, per_token_mod, toks_per_seq, tab, tb):
    n = y3.shape[0]
    if per_token_mod:
        g_spec = pl.BlockSpec((tb, ROW_CHUNKS, LANES), lambda i: (i, 0, 0))
    else:
        blocks_per_seq = toks_per_seq // tb
        g_spec = pl.BlockSpec((1, ROW_CHUNKS, LANES), lambda i: (i // blocks_per_seq, 0, 0))
    kern = functools.partial(_peer_mix_kernel, tb=tb, per_token_mod=per_token_mod)
    return pl.pallas_call(
        kern,
        grid=(n // tb,),
        in_specs=[pl.BlockSpec((tb, PEER_PAIRS), lambda i: (i, 0), memory_space=pltpu.SMEM),
                  pl.BlockSpec((tb, PEER_PAIRS), lambda i: (i, 0)),
                  pl.BlockSpec((tb, ROW_CHUNKS, LANES), lambda i: (i, 0, 0)), g_spec, _table_spec(tab)],
        out_specs=pl.BlockSpec((tb, ROW_CHUNKS, LANES), lambda i: (i, 0, 0)),
        out_shape=jax.ShapeDtypeStruct((n, ROW_CHUNKS, LANES), F32),
        scratch_shapes=[pltpu.VMEM((PEER_PAIRS * ROW_WORDS, LANES), jnp.int32)] * 2 + [
                        pltpu.VMEM((tb, PEER_PAIRS * ROW_CHUNKS), F32),
                        pltpu.VMEM((tb, PEER_PAIRS * ROW_CHUNKS), F32)],
        compiler_params=_cparams(("arbitrary",)),
        name="peer_mix",
    )(eidx, w, y3, gate2, tab)


def _layer(x, mod, conv_buf, C0, n0, m0, k_buf, v_buf, rel_bias, wts, tm, tb):
    B, T, _ = x.shape
    n = B * T
    x2 = x.reshape(n, D_MODEL)
    shift1, scale1, gate1, shift2, scale2, gate2 = jnp.split(mod, 6, axis=-1)
    per_token = T % tm != 0
    if per_token:
        expand = lambda a: jnp.repeat(a, T, axis=0)
    else:
        expand = lambda a: a.reshape(B, 1, D_MODEL)

    q, k, v, mqk, mv, mo, gcol, grow = _in_proj(
        x2, expand(scale1), expand(shift1), per_token, T, tm, wts["g1"], wts["wm"], wts["wgc"], wts["wgr"],
        wts["bd"], wts["qg"], wts["kg"])

    if k_buf is None:
        att = _att_prompt(q, k, v, _prompt_bias(rel_bias), B, T)
        k_new = k.reshape(B, T, H_B, HD_B)
        v_new = v.reshape(B, T, H_B, HD_B)
    else:
        wbuf = k_buf.shape[1]
        bias_c, bias_n = _sample_bias(rel_bias, T, wbuf)
        att, k_new, v_new = _att_sample(q, k, v, k_buf.reshape(B, wbuf, W_B), v_buf.reshape(B, wbuf, W_B),
                                        bias_c, bias_n, B, T, wbuf)
        k_new = k_new.reshape(B, wbuf, H_B, HD_B)
        v_new = v_new.reshape(B, wbuf, H_B, HD_B)

    ha, C, nn, m = _mlstm(mqk, mv, gcol, grow, wts["conv_w"], wts["conv_b"], conv_buf, wts["b_gate"], C0, n0, m0, B, T)
    mqk3 = mqk.reshape(B, T, 2 * W_A)
    if T >= CONV_W - 1:
        conv_new = mqk3[:, T - (CONV_W - 1):]
    else:
        conv_new = jnp.concatenate([conv_buf, mqk3], axis=1)[:, -(CONV_W - 1):]

    y1, h2 = _out_proj(ha, mo, att, x2, expand(gate1), expand(scale2), expand(shift2), per_token, T, tm,
                       wts["mg"], wts["ag"], wts["g2"], wts["bd"], wts["wo"])

    eidx, gate = _peer_route(h2, wts["wq"], wts["keys"], tb)
    eidx = eidx.T
    w = _peer_act(eidx, h2.reshape(n, 8, LANES), gate.T, wts["u_tab"], tb)
    if per_token:
        g2e = jnp.repeat(gate2, T, axis=0).reshape(n, 8, LANES)
    else:
        g2e = gate2.reshape(B, 8, LANES)
    y = _peer_mix(eidx, w, y1.reshape(n, 8, LANES), g2e, per_token, T, wts["v_tab"], tb)
    return (y.reshape(B, T, D_MODEL), k_new, v_new, conv_new, C, nn.reshape(B, H_A, DK_A), m[:, :, 0, 0])


def _prep_weights(l, norm1_g, norm2_g, w_in, b_gate, conv_w, conv_b, q_norm_g, k_norm_g, att_out_g, mlstm_out_g,
                  w_out, peer_wq, peer_keys, peer_u, peer_v):
    w = w_in[l]
    wg = w[:, W_MAIN:]
    grp = np.arange(W_B) // HD_B
    bd = jnp.asarray((grp[:, None] == grp[None, :]).astype(np.float32) / HD_B, BF16)
    return dict(
        g1=norm1_g[l].reshape(1, -1), g2=norm2_g[l].reshape(1, -1),
        wm=w[:, :W_MAIN].astype(BF16),
        wgc=jnp.pad(wg, ((0, 0), (0, LANES - 2 * H_A))), wgr=wg.T,
        bd=bd, qg=jnp.tile(q_norm_g[l], H_B).reshape(1, -1), kg=jnp.tile(k_norm_g[l], H_B).reshape(1, -1),
        conv_w=conv_w[l], conv_b=conv_b[l], b_gate=b_gate[l],
        mg=mlstm_out_g[l].reshape(1, -1), ag=att_out_g[l].reshape(1, -1),
        wo=w_out[l].astype(BF16), wq=peer_wq[l].astype(BF16), keys=peer_keys[l],
        u_tab=_pack_table(peer_u[l]), v_tab=_pack_table(peer_v[l]),
    )


def kernel(x_prompt, x_sample, cache_k, cache_v, state_conv, state_C, state_n, state_m, c_prompt, c_sample,
           rel_bias, w_ada, b_ada, norm1_g, norm2_g, w_in, b_gate, conv_w, conv_b, q_norm_g, k_norm_g,
           att_out_g, mlstm_out_g, w_out, peer_wq, peer_keys, peer_u, peer_v):
    depth = w_ada.shape[0]
    bp = x_prompt.shape[0]
    bs = x_sample.shape[0]
    yp, ys = x_prompt, x_sample
    sp, ss = [], []
    for l in range(depth):
        wts = _prep_weights(l, norm1_g, norm2_g, w_in, b_gate, conv_w, conv_b, q_norm_g, k_norm_g, att_out_g,
                            mlstm_out_g, w_out, peer_wq, peer_keys, peer_u, peer_v)
        mod = _ada(jnp.concatenate([c_prompt, c_sample], axis=0), w_ada[l], b_ada[l])
        zc = jnp.zeros((bp, CONV_W - 1, 2 * W_A), F32)
        zC = jnp.zeros((bp, H_A, DK_A, DK_A), F32)
        zn = jnp.zeros((bp, H_A, DK_A), F32)
        zm = jnp.zeros((bp, H_A), F32)
        outp = _layer(yp, mod[:bp], zc, zC, zn, zm, None, None, rel_bias, wts, tm=256, tb=128)
        outs = _layer(ys, mod[bp:], state_conv[l], state_C[l], state_n[l], state_m[l], cache_k[l], cache_v[l],
                      rel_bias, wts, tm=256, tb=128)
        yp, ys = outp[0], outs[0]
        sp.append(outp[1:])
        ss.append(outs[1:])
    k_p, v_p, conv_p, C_p, n_p, m_p = [jnp.stack([s[i] for s in sp], axis=0) for i in range(6)]
    k_s, v_s, conv_s, C_s, n_s, m_s = [jnp.stack([s[i] for s in ss], axis=0) for i in range(6)]
    return (yp, ys, k_p, v_p, conv_p, C_p, n_p, m_p, k_s, v_s, conv_s, C_s, n_s, m_s)
```

```python
import functools
import math

import numpy as np
import jax
import jax.numpy as jnp
from jax import lax
from jax.experimental import pallas as pl
from jax.experimental.pallas import tpu as pltpu

F32 = jnp.float32
BF16 = jnp.bfloat16
HIGHEST = lax.Precision.HIGHEST

D_MODEL = 1024
H_A = 4
DK_A = 128
W_A = H_A * DK_A
H_B = 8
HD_B = 64
W_B = H_B * HD_B
CONV_W = 4
MLSTM_CHUNK = 64
MLSTM_HP = 2
DIL_PATTERNS = ((128, 1), (512, 4), (2048, 16))
ATT_STEPS = 128
N_BUCKETS = 32
MAX_DIST = 2048
PEER_HEADS = 8
N_SUBKEYS = 128
PEER_TOPK = 16
PEER_PAIRS = PEER_HEADS * PEER_TOPK
EPS = 1e-6
NEG = -1e30
W_MAIN = 3 * W_B + 4 * W_A
LANES = 128
VMEM_LIMIT = 56 * 1024 * 1024


def _cparams(sem):
    return pltpu.CompilerParams(dimension_semantics=sem, vmem_limit_bytes=VMEM_LIMIT)


def _nt_dot(a, b, precision=None):
    return lax.dot_general(a, b, (((1,), (1,)), ((), ())), preferred_element_type=F32, precision=precision)


def _t5_bucket(dist):
    max_exact = N_BUCKETS // 2
    d = np.maximum(dist, 1).astype(np.float32)
    large = max_exact + (np.log(d / max_exact) / math.log(MAX_DIST / max_exact) * (N_BUCKETS - max_exact)).astype(np.int32)
    large = np.minimum(large, N_BUCKETS - 1)
    return np.where(dist < max_exact, dist, large).astype(np.int32)


def _ada_kernel(c_ref, w_ref, b_ref, o_ref):
    c = c_ref[...]
    s = c * jax.nn.sigmoid(c)
    o_ref[...] = jnp.dot(s, w_ref[...], preferred_element_type=F32, precision=HIGHEST) + b_ref[...]


def _ada(c_all, w_ada, b_ada):
    n = c_all.shape[0]
    return pl.pallas_call(
        _ada_kernel,
        grid=(6,),
        in_specs=[pl.BlockSpec((n, D_MODEL), lambda j: (0, 0)),
                  pl.BlockSpec((D_MODEL, D_MODEL), lambda j: (0, j)),
                  pl.BlockSpec((1, D_MODEL), lambda j: (0, j))],
        out_specs=pl.BlockSpec((n, D_MODEL), lambda j: (0, j)),
        out_shape=jax.ShapeDtypeStruct((n, 6 * D_MODEL), F32),
        compiler_params=_cparams(("arbitrary",)),
        name="ada",
    )(c_all, w_ada, b_ada.reshape(1, -1))


def _group_mean_sq(a, bd):
    sq = a * a
    hi = sq.astype(BF16)
    lo = (sq - hi.astype(F32)).astype(BF16)
    return jnp.dot(hi, bd, preferred_element_type=F32) + jnp.dot(lo, bd, preferred_element_type=F32)


def _in_kernel(x_ref, sc_ref, sh_ref, g1_ref, wm_ref, wgc_ref, wgr_ref, bd_ref, qg_ref, kg_ref,
               q_ref, k_ref, v_ref, mqk_ref, mv_ref, mo_ref, gcol_ref, grow_ref):
    x = x_ref[...]
    ms = jnp.mean(x * x, axis=-1, keepdims=True)
    h = x * lax.rsqrt(ms + EPS) * g1_ref[...]
    h = h * (1.0 + sc_ref[...]) + sh_ref[...]
    y = jnp.dot(h.astype(BF16), wm_ref[...], preferred_element_type=F32)
    bd = bd_ref[...]
    aq = y[:, 0:W_B]
    ak = y[:, W_B:2 * W_B]
    q_ref[...] = aq * lax.rsqrt(_group_mean_sq(aq, bd) + EPS) * qg_ref[...] * (HD_B ** -0.5)
    k_ref[...] = ak * lax.rsqrt(_group_mean_sq(ak, bd) + EPS) * kg_ref[...]
    v_ref[...] = y[:, 2 * W_B:3 * W_B]
    o = 3 * W_B
    mqk_ref[...] = y[:, o:o + 2 * W_A]
    mv_ref[...] = y[:, o + 2 * W_A:o + 3 * W_A]
    mo_ref[...] = y[:, o + 3 * W_A:o + 4 * W_A]
    gcol_ref[...] = jnp.dot(h, wgc_ref[...], preferred_element_type=F32, precision=HIGHEST)
    grow_ref[...] = _nt_dot(wgr_ref[...], h, precision=HIGHEST)


def _in_proj(x2, scale, shift, per_token_mod, toks_per_seq, tm, g1, wm, wgc, wgr, bd, qg, kg):
    n = x2.shape[0]
    nt = n // tm
    if per_token_mod:
        mod_spec = pl.BlockSpec((tm, D_MODEL), lambda i: (i, 0))
    else:
        tiles_per_seq = toks_per_seq // tm
        mod_spec = pl.BlockSpec((None, 1, D_MODEL), lambda i: (i // tiles_per_seq, 0, 0))
    const = lambda shape: pl.BlockSpec(shape, lambda i: (0,) * len(shape))
    tok = lambda w: pl.BlockSpec((tm, w), lambda i: (i, 0))
    outs = pl.pallas_call(
        _in_kernel,
        grid=(nt,),
        in_specs=[tok(D_MODEL), mod_spec, mod_spec, const((1, D_MODEL)), const((D_MODEL, W_MAIN)),
                  const((D_MODEL, LANES)), const((8, D_MODEL)), const((W_B, W_B)), const((1, W_B)), const((1, W_B))],
        out_specs=[tok(W_B), tok(W_B), tok(W_B), tok(2 * W_A), tok(W_A), tok(W_A), tok(LANES),
                   pl.BlockSpec((8, tm), lambda i: (0, i))],
        out_shape=[jax.ShapeDtypeStruct((n, W_B), F32)] * 3 + [jax.ShapeDtypeStruct((n, 2 * W_A), F32)]
                  + [jax.ShapeDtypeStruct((n, W_A), F32)] * 2 + [jax.ShapeDtypeStruct((n, LANES), F32),
                                                                 jax.ShapeDtypeStruct((8, n), F32)],
        compiler_params=_cparams(("arbitrary",)),
        name="in_proj",
    )(x2, scale, shift, g1, wm, wgc, wgr, bd, qg, kg)
    return outs


def _log_sigmoid(x):
    return jnp.minimum(x, 0.0) - jnp.log1p(jnp.exp(-jnp.abs(x)))


def _mlstm_kernel(bg_ref, m0_ref, mq_ref, mk_ref, cwq_ref, cwk_ref, cbq_ref, cbk_ref, bufq_ref, bufk_ref,
                  v_ref, gcol_ref, grow_ref, c0_ref, n0_ref,
                  h_ref, c_ref, n_ref, m_ref, sq, sk, *, T, L, rows_whole):
    b = pl.program_id(0)
    hg = pl.program_id(1)
    nc = T // L

    def conv(u_ref, buf_ref, w_ref, cb_ref, s_ref, out_ref):
        s_ref[0:8, :] = jnp.zeros((8, MLSTM_HP * LANES), F32)
        s_ref[5:8, :] = buf_ref[...]
        s_ref[8:8 + T, :] = u_ref[...]
        y = cb_ref[...]
        for j in range(CONV_W):
            y = y + s_ref[5 + j:5 + j + T, :] * w_ref[j:j + 1, :]
        out_ref[...] = y * jax.nn.sigmoid(y)

    conv(mq_ref, bufq_ref, cwq_ref, cbq_ref, sq, h_ref)
    conv(mk_ref, bufk_ref, cwk_ref, cbk_ref, sk, sq.at[8:8 + T, :])

    lane = lax.broadcasted_iota(jnp.int32, (L, LANES), 1)
    ri = lax.broadcasted_iota(jnp.int32, (L, L), 0)
    ci = lax.broadcasted_iota(jnp.int32, (L, L), 1)
    causal = ri >= ci
    tri = causal.astype(F32)
    tri_t = (ri <= ci).astype(F32)

    def head_chunk(c, j, carry):
        C, n, m = carry
        hd = hg * MLSTM_HP + j
        ig_b = bg_ref[hd]
        f_b = bg_ref[H_A + hd]
        cs = slice(j * LANES, (j + 1) * LANES)
        r0 = pl.multiple_of(c * L, L)
        qc = h_ref[pl.ds(r0, L), cs]
        kc = sq[pl.ds(r0 + 8, L), cs] * (DK_A ** -0.5)
        vc = v_ref[pl.ds(r0, L), cs]
        g = gcol_ref[pl.ds(r0, L), :]
        ig_col = jnp.sum(jnp.where(lane == hd, g, 0.0), axis=1, keepdims=True) + ig_b
        f_col = jnp.sum(jnp.where(lane == hd + H_A, g, 0.0), axis=1, keepdims=True) + f_b
        lf_col = _log_sigmoid(f_col)
        rc = c + (b * nc if rows_whole else 0)
        ig_row = grow_ref[hd, pl.ds(rc, 1), :] + ig_b
        lf_row = _log_sigmoid(grow_ref[hd + H_A, pl.ds(rc, 1), :] + f_b)
        b_col = jnp.dot(tri, jnp.broadcast_to(lf_col, (L, LANES)), preferred_element_type=F32,
                        precision=HIGHEST)[:, 0:1]
        b_row = jnp.dot(jnp.broadcast_to(lf_row, (8, L)), tri_t, preferred_element_type=F32,
                        precision=HIGHEST)[0:1, :]
        g_col = b_col + m
        dm = jnp.where(causal, b_col - b_row + ig_row, -jnp.inf)
        mt = jnp.maximum(g_col, jnp.max(dm, axis=1, keepdims=True))
        wd = jnp.exp(dm - mt)
        wg = jnp.exp(g_col - mt)
        qb = qc.astype(BF16)
        kb = kc.astype(BF16)
        vb = vc.astype(BF16)
        s = _nt_dot(qb, kb) * wd
        num = wg * jnp.dot(qb, C.astype(BF16), preferred_element_type=F32) \
            + jnp.dot(s.astype(BF16), vb, preferred_element_type=F32)
        den = wg * jnp.sum(qc * n, axis=1, keepdims=True) + jnp.sum(s, axis=1, keepdims=True)
        h_ref[pl.ds(r0, L), cs] = num / jnp.maximum(jnp.abs(den), jnp.exp(-mt))
        bl = b_col[L - 1:L, :]
        ws_log_col = bl - b_col + ig_col
        ws_log_row = bl - b_row + ig_row
        m_new = jnp.maximum(bl + m, jnp.max(ws_log_row, axis=1, keepdims=True))
        ws_col = jnp.exp(ws_log_col - m_new)
        wc = jnp.exp(bl + m - m_new)
        kw = ws_col * kc
        upd = lax.dot_general(kw.astype(BF16), vb, (((0,), (0,)), ((), ())), preferred_element_type=F32)
        return wc * C + upd, wc * n + jnp.sum(kw, axis=0, keepdims=True), m_new

    def chunk(c, carry):
        return tuple(head_chunk(c, j, carry[j]) for j in range(MLSTM_HP))

    init = tuple((c0_ref[j], n0_ref[j], jnp.full((1, 1), m0_ref[b, hg * MLSTM_HP + j], F32))
                 for j in range(MLSTM_HP))
    final = lax.fori_loop(0, nc, chunk, init)
    for j in range(MLSTM_HP):
        c_ref[j] = final[j][0]
        n_ref[j] = final[j][1]
        m_ref[j] = jnp.broadcast_to(final[j][2], (1, LANES))


def _mlstm(mqk, mv, gcol, grow, conv_w, conv_b, conv_buf, b_gate, C0, n0, m0, B, T):
    L = math.gcd(T, MLSTM_CHUNK)
    nc = T // L
    n = B * T
    rows_whole = nc % 8 != 0
    grow3 = grow.reshape(8, n // L, L)
    if rows_whole:
        grow_spec = pl.BlockSpec((8, n // L, L), lambda b, h: (0, 0, 0))
    else:
        grow_spec = pl.BlockSpec((8, nc, L), lambda b, h: (0, b, 0))
    smem = pl.BlockSpec(memory_space=pltpu.SMEM)
    wide = MLSTM_HP * LANES
    ng = H_A // MLSTM_HP
    colq = lambda rows: pl.BlockSpec((rows, wide), lambda b, h: (0, h))
    colk = lambda rows: pl.BlockSpec((rows, wide), lambda b, h: (0, h + ng))
    st = lambda r, c: pl.BlockSpec((None, MLSTM_HP, r, c), lambda b, h: (b, h, 0, 0))
    kern = functools.partial(_mlstm_kernel, T=T, L=L, rows_whole=rows_whole)
    return pl.pallas_call(
        kern,
        grid=(B, ng),
        in_specs=[smem, smem,
                  pl.BlockSpec((T, wide), lambda b, h: (b, h)), pl.BlockSpec((T, wide), lambda b, h: (b, h + ng)),
                  colq(CONV_W), colk(CONV_W), colq(1), colk(1),
                  pl.BlockSpec((None, CONV_W - 1, wide), lambda b, h: (b, 0, h)),
                  pl.BlockSpec((None, CONV_W - 1, wide), lambda b, h: (b, 0, h + ng)),
                  pl.BlockSpec((T, wide), lambda b, h: (b, h)),
                  pl.BlockSpec((T, LANES), lambda b, h: (b, 0)),
                  grow_spec, st(DK_A, DK_A), st(1, DK_A)],
        out_specs=[pl.BlockSpec((T, wide), lambda b, h: (b, h)), st(DK_A, DK_A), st(1, DK_A), st(1, LANES)],
        out_shape=[jax.ShapeDtypeStruct((n, W_A), F32), jax.ShapeDtypeStruct((B, H_A, DK_A, DK_A), F32),
                   jax.ShapeDtypeStruct((B, H_A, 1, DK_A), F32), jax.ShapeDtypeStruct((B, H_A, 1, LANES), F32)],
        scratch_shapes=[pltpu.VMEM((T + 8, wide), F32), pltpu.VMEM((T + 8, wide), F32)],
        compiler_params=_cparams(("arbitrary", "arbitrary")),
        name="mlstm",
    )(b_gate, m0, mqk, mqk, conv_w, conv_w, conv_b.reshape(1, -1), conv_b.reshape(1, -1), conv_buf, conv_buf,
      mv, gcol, grow3, C0, n0.reshape(B, H_A, 1, DK_A))


def _att_prompt_kernel(q_ref, k_ref, v_ref, bias_ref, o_ref, num_s, mx_s, dn_s, *, T):
    lane = lax.broadcasted_iota(jnp.int32, (ATT_STEPS, LANES), 1)
    first = lane < HD_B
    nblk = T // ATT_STEPS

    for br, (_, dil) in enumerate(DIL_PATTERNS):
        span = ATT_STEPS * dil
        has_prev = T > span

        def rows(start, dil=dil):
            return pl.ds(start, ATT_STEPS, stride=dil) if dil > 1 else pl.ds(start, ATT_STEPS)

        def blk(i, carry, br=br, dil=dil, span=span, has_prev=has_prev, rows=rows):
            r = i % dil
            nb = i // dil
            start = nb * span + r
            qb = q_ref[rows(start), :]
            kcur = k_ref[rows(start), :].astype(BF16)
            vcur = v_ref[rows(start), :].astype(BF16)
            if has_prev:
                pstart = jnp.maximum(start - span, 0)
                kprev = k_ref[rows(pstart), :].astype(BF16)
                vprev = v_ref[rows(pstart), :].astype(BF16)
            res = []
            for hh in range(2):
                qh = jnp.where(first if hh == 0 else ~first, qb, 0.0).astype(BF16)
                lc = _nt_dot(qh, kcur) + bias_ref[br, hh, :, ATT_STEPS:2 * ATT_STEPS]
                mx = jnp.max(lc, axis=1, keepdims=True)
                if has_prev:
                    lp = _nt_dot(qh, kprev) + bias_ref[br, hh, :, 0:ATT_STEPS]
                    lp = jnp.where(nb > 0, lp, NEG)
                    mx = jnp.maximum(mx, jnp.max(lp, axis=1, keepdims=True))
                pc = jnp.exp(lc - mx)
                den = jnp.sum(pc, axis=1, keepdims=True)
                num = jnp.dot(pc.astype(BF16), vcur, preferred_element_type=F32)
                if has_prev:
                    pp = jnp.exp(lp - mx)
                    den = den + jnp.sum(pp, axis=1, keepdims=True)
                    num = num + jnp.dot(pp.astype(BF16), vprev, preferred_element_type=F32)
                res.append((num, mx, den))
            num_s[br, rows(start), :] = jnp.where(first, res[0][0], res[1][0])
            mx_s[br, rows(start), :] = jnp.where(first, res[0][1], res[1][1])
            dn_s[br, rows(start), :] = jnp.where(first, res[0][2], res[1][2])
            return carry

        lax.fori_loop(0, nblk // 2, lambda i, c, blk=blk: blk(i + nblk // 2, blk(i, c)), 0)

    def comb(i, carry):
        rs = pl.ds(pl.multiple_of(i * ATT_STEPS, ATT_STEPS), ATT_STEPS)
        m = jnp.maximum(jnp.maximum(mx_s[0, rs, :], mx_s[1, rs, :]), mx_s[2, rs, :])
        num = jnp.zeros((ATT_STEPS, LANES), F32)
        den = jnp.zeros((ATT_STEPS, LANES), F32)
        for br in range(3):
            e = jnp.exp(mx_s[br, rs, :] - m)
            num = num + e * num_s[br, rs, :]
            den = den + e * dn_s[br, rs, :]
        o_ref[rs, :] = num / den
        return carry

    lax.fori_loop(0, nblk, comb, 0)


def _att_prompt(q, k, v, bias, B, T):
    n = B * T
    blk = pl.BlockSpec((T, LANES), lambda b, p: (b, p))
    kern = functools.partial(_att_prompt_kernel, T=T)
    return pl.pallas_call(
        kern,
        grid=(B, H_B // 2),
        in_specs=[blk, blk, blk, pl.BlockSpec((3, 2, ATT_STEPS, 2 * ATT_STEPS), lambda b, p: (0, p, 0, 0))],
        out_specs=blk,
        out_shape=jax.ShapeDtypeStruct((n, W_B), F32),
        scratch_shapes=[pltpu.VMEM((3, T, LANES), F32)] * 3,
        compiler_params=_cparams(("arbitrary", "arbitrary")),
        name="att_prompt",
    )(q, k, v, bias)


def _prompt_bias(rel_bias):
    qi = np.arange(ATT_STEPS)[:, None]
    kj = np.arange(2 * ATT_STEPS)[None, :]
    jstep = ATT_STEPS + qi - kj
    band = (jstep >= 0) & (jstep <= ATT_STEPS)
    tabs = []
    for _, dil in DIL_PATTERNS:
        bucket = _t5_bucket(np.clip(jstep, 0, ATT_STEPS) * dil)
        t = jnp.moveaxis(rel_bias[bucket].astype(F32), -1, 0)
        tabs.append(jnp.where(band[None], t, NEG))
    return jnp.stack(tabs, axis=0)


def _att_sample_kernel(q_ref, kn_ref, vn_ref, ck_ref, cv_ref, bc_ref, bn_ref, o_ref, ko_ref, vo_ref, *, S, WBUF):
    lane = lax.broadcasted_iota(jnp.int32, (S, LANES), 1)
    first = lane < HD_B
    for hp in range(H_B // 2):
        cs = slice(hp * LANES, (hp + 1) * LANES)
        kc = ck_ref[:, cs].astype(BF16)
        vc = cv_ref[:, cs].astype(BF16)
        kn = kn_ref[:, cs].astype(BF16)
        vn = vn_ref[:, cs].astype(BF16)
        qp = q_ref[:, cs]
        res = []
        for hh in range(2):
            h = 2 * hp + hh
            qh = jnp.where(first if hh == 0 else ~first, qp, 0.0).astype(BF16)
            lc = _nt_dot(qh, kc) + bc_ref[h]
            ln = _nt_dot(qh, kn) + bn_ref[h]
            mx = jnp.maximum(jnp.max(lc, axis=1, keepdims=True), jnp.max(ln, axis=1, keepdims=True))
            pc = jnp.exp(lc - mx)
            pn = jnp.exp(ln - mx)
            den = jnp.sum(pc, axis=1, keepdims=True) + jnp.sum(pn, axis=1, keepdims=True)
            num = jnp.dot(pc.astype(BF16), vc, preferred_element_type=F32) \
                + jnp.dot(pn.astype(BF16), vn, preferred_element_type=F32)
            res.append(num / den)
        o_ref[:, cs] = jnp.where(first, res[0], res[1])
    ko_ref[0:WBUF - S, :] = ck_ref[S:WBUF, :]
    ko_ref[WBUF - S:WBUF, :] = kn_ref[...]
    vo_ref[0:WBUF - S, :] = cv_ref[S:WBUF, :]
    vo_ref[WBUF - S:WBUF, :] = vn_ref[...]


def _att_sample(q, kn, vn, ck, cv, bias_c, bias_n, B, S, WBUF):
    tokb = pl.BlockSpec((S, W_B), lambda b: (b, 0))
    cache = pl.BlockSpec((None, WBUF, W_B), lambda b: (b, 0, 0))
    kern = functools.partial(_att_sample_kernel, S=S, WBUF=WBUF)
    return pl.pallas_call(
        kern,
        grid=(B,),
        in_specs=[tokb, tokb, tokb, cache, cache,
                  pl.BlockSpec((H_B, S, WBUF), lambda b: (0, 0, 0)), pl.BlockSpec((H_B, S, S), lambda b: (0, 0, 0))],
        out_specs=[tokb, cache, cache],
        out_shape=[jax.ShapeDtypeStruct((B * S, W_B), F32), jax.ShapeDtypeStruct((B, WBUF, W_B), F32),
                   jax.ShapeDtypeStruct((B, WBUF, W_B), F32)],
        compiler_params=_cparams(("arbitrary",)),
        name="att_sample",
    )(q, kn, vn, ck, cv, bias_c, bias_n)


def _sample_bias(rel_bias, S, WBUF):
    dist = np.arange(WBUF + S)
    mult = np.zeros(WBUF + S, np.int64)
    for w, dil in DIL_PATTERNS:
        mult += ((dist % dil == 0) & (dist <= w)).astype(np.int64)
    logm = np.where(mult > 0, np.log(np.maximum(mult, 1)), 0.0).astype(np.float32)
    tab = rel_bias[_t5_bucket(dist)].astype(F32).T + logm[None, :]
    tab = jnp.where((mult > 0)[None, :], tab, NEG)
    s = np.arange(S)[:, None]
    dc = WBUF + s - np.arange(WBUF)[None, :]
    dn = s - np.arange(S)[None, :]
    bias_c = tab[:, dc]
    bias_n = jnp.where((dn >= 0)[None], tab[:, np.maximum(dn, 0)], NEG)
    return bias_c, bias_n


def _out_kernel(ha_ref, mo_ref, att_ref, x_ref, g1_ref, sc_ref, sh_ref, mg_ref, ag_ref, g2_ref, bd_ref, wo_ref,
                y_ref, h2_ref):
    ha = ha_ref[...]
    parts = []
    for hd in range(H_A):
        a = ha[:, hd * DK_A:(hd + 1) * DK_A]
        parts.append(a * lax.rsqrt(jnp.mean(a * a, axis=-1, keepdims=True) + EPS))
    hn = jnp.concatenate(parts, axis=1) * mg_ref[...] * jax.nn.sigmoid(mo_ref[...])
    att = att_ref[...]
    an = att * lax.rsqrt(_group_mean_sq(att, bd_ref[...]) + EPS) * ag_ref[...]
    mix = jnp.dot(hn.astype(BF16), wo_ref[0:W_A, :], preferred_element_type=F32) \
        + jnp.dot(an.astype(BF16), wo_ref[W_A:W_A + W_B, :], preferred_element_type=F32)
    y = x_ref[...] + g1_ref[...] * mix
    y_ref[...] = y
    h2 = y * lax.rsqrt(jnp.mean(y * y, axis=-1, keepdims=True) + EPS) * g2_ref[...]
    h2_ref[...] = h2 * (1.0 + sc_ref[...]) + sh_ref[...]


def _out_proj(ha, mo, att, x2, gate1, scale2, shift2, per_token_mod, toks_per_seq, tm, mg, ag, g2, bd, wo):
    n = x2.shape[0]
    if per_token_mod:
        mod_spec = pl.BlockSpec((tm, D_MODEL), lambda i: (i, 0))
    else:
        tiles_per_seq = toks_per_seq // tm
        mod_spec = pl.BlockSpec((None, 1, D_MODEL), lambda i: (i // tiles_per_seq, 0, 0))
    const = lambda shape: pl.BlockSpec(shape, lambda i: (0,) * len(shape))
    tok = lambda w: pl.BlockSpec((tm, w), lambda i: (i, 0))
    return pl.pallas_call(
        _out_kernel,
        grid=(n // tm,),
        in_specs=[tok(W_A), tok(W_A), tok(W_B), tok(D_MODEL), mod_spec, mod_spec, mod_spec,
                  const((1, W_A)), const((1, W_B)), const((1, D_MODEL)), const((W_B, W_B)),
                  const((W_A + W_B, D_MODEL))],
        out_specs=[tok(D_MODEL), tok(D_MODEL)],
        out_shape=[jax.ShapeDtypeStruct((n, D_MODEL), F32)] * 2,
        compiler_params=_cparams(("arbitrary",)),
        name="out_proj",
    )(ha, mo, att, x2, gate1, scale2, shift2, mg, ag, g2, bd, wo)


def _top16(s, nrows):
    iota = lax.broadcasted_iota(jnp.int32, s.shape, 0).astype(F32)
    vals, idxs = [], []
    for _ in range(PEER_TOPK):
        m = jnp.max(s, axis=0, keepdims=True)
        pos = jnp.min(jnp.where(s == m, iota, float(nrows)), axis=0, keepdims=True)
        vals.append(m)
        idxs.append(pos)
        s = jnp.where(iota == pos, -jnp.inf, s)
    return jnp.concatenate(vals, axis=0), jnp.concatenate(idxs, axis=0)


def _peer_route_kernel(h2_ref, wq_ref, keys_ref, eidx_ref, gate_ref, sv_s, si_s, gt_s, et_s, *, tm):
    qh = jnp.dot(h2_ref[...].astype(BF16), wq_ref[...], preferred_element_type=F32)
    for hp in range(2 * PEER_HEADS):
        qs = qh[:, hp * N_SUBKEYS:(hp + 1) * N_SUBKEYS]
        s = _nt_dot(keys_ref[hp % 2], qs, precision=HIGHEST)
        v, i = _top16(s, N_SUBKEYS)
        sv_s[hp] = v
        si_s[hp] = i

    k = PEER_TOPK
    half = k // 2
    sub = lax.broadcasted_iota(jnp.int32, (half, tm), 0).astype(F32)
    pos = jnp.concatenate(
        [lax.broadcasted_iota(jnp.int32, (k, tm), 0).astype(F32)]
        + [float(a * k) + sub for a in range(1, half)] + [(sub + float(half)) * float(k)], axis=0)
    n_cand = float(k * k)

    def head(h, carry):
        sv0 = sv_s[2 * h]
        sv1 = sv_s[2 * h + 1]
        si0 = si_s[2 * h] * float(N_SUBKEYS)
        si1 = si_s[2 * h + 1]
        cands = [sv0[0:1, :] + sv1]
        cidxs = [si0[0:1, :] + si1]
        for a in range(1, half):
            c = sv0[a:a + 1, :] + sv1[0:half, :]
            nb = k // (a + 1)
            cands.append(c if nb >= half else jnp.where(sub < float(nb), c, -jnp.inf))
            cidxs.append(si0[a:a + 1, :] + si1[0:half, :])
        cands.append(sv0[half:k, :] + sv1[0:1, :])
        cidxs.append(si0[half:k, :] + si1[0:1, :])
        cand = jnp.concatenate(cands, axis=0)
        cidx = jnp.concatenate(cidxs, axis=0)
        fv, ev = [], []
        for _ in range(k):
            m = jnp.max(cand, axis=0, keepdims=True)
            first = jnp.min(jnp.where(cand == m, pos, n_cand), axis=0, keepdims=True)
            sel = pos == first
            ev.append(jnp.max(jnp.where(sel, cidx, -1.0), axis=0, keepdims=True))
            fv.append(m)
            cand = jnp.where(sel, -jnp.inf, cand)
        fvs = jnp.concatenate(fv, axis=0)
        e = jnp.exp(fvs - fv[0])
        rs = pl.ds(pl.multiple_of(h * k, k), k)
        gt_s[rs, :] = e / jnp.sum(e, axis=0, keepdims=True)
        et_s[rs, :] = jnp.concatenate(ev, axis=0)
        return carry

    lax.fori_loop(0, PEER_HEADS, head, 0)
    for j in range(tm // LANES):
        cs = slice(j * LANES, (j + 1) * LANES)
        gate_ref[cs, :] = gt_s[:, cs].T
        eidx_ref[cs, :] = et_s[:, cs].T.astype(jnp.int32)


def _peer_route(h2, wq, keys, tm):
    n = h2.shape[0]
    kern = functools.partial(_peer_route_kernel, tm=tm)
    return pl.pallas_call(
        kern,
        grid=(n // tm,),
        in_specs=[pl.BlockSpec((tm, D_MODEL), lambda i: (i, 0)),
                  pl.BlockSpec((D_MODEL, 2 * PEER_HEADS * N_SUBKEYS), lambda i: (0, 0)),
                  pl.BlockSpec((2, N_SUBKEYS, N_SUBKEYS), lambda i: (0, 0, 0))],
        out_specs=[pl.BlockSpec((tm, PEER_PAIRS), lambda i: (i, 0)), pl.BlockSpec((tm, PEER_PAIRS), lambda i: (i, 0))],
        out_shape=[jax.ShapeDtypeStruct((n, PEER_PAIRS), jnp.int32), jax.ShapeDtypeStruct((n, PEER_PAIRS), F32)],
        scratch_shapes=[pltpu.VMEM((2 * PEER_HEADS, PEER_TOPK, tm), F32)] * 2 + [pltpu.VMEM((PEER_PAIRS, tm), F32)] * 2,
        compiler_params=_cparams(("arbitrary",)),
        name="peer_route",
    )(h2, wq, keys)


CHUNK_W = LANES
ROW_CHUNKS = D_MODEL // CHUNK_W
GROUP = LANES // ROW_CHUNKS


def _pack_table(tab):
    return tab.astype(BF16).reshape(tab.shape[0], ROW_CHUNKS, CHUNK_W)


def _gather_rows(idx_ref, t, tab_ref, g_s):
    for p in range(PEER_PAIRS):
        g_s[ROW_CHUNKS * p:ROW_CHUNKS * (p + 1), :] = tab_ref[idx_ref[t, p]]


def _gathered(g_s):
    return g_s[...]


def _token_loop(tb, idx_ref, tab_ref, g0_s, g1_s, compute):
    _gather_rows(idx_ref, 0, tab_ref, g0_s)

    def step(i, carry):
        t0 = 2 * i
        _gather_rows(idx_ref, t0 + 1, tab_ref, g1_s)
        compute(t0, g0_s)
        _gather_rows(idx_ref, jnp.minimum(t0 + 2, tb - 1), tab_ref, g0_s)
        compute(t0 + 1, g1_s)
        return carry

    lax.fori_loop(0, tb // 2, step, 0)


def _peer_act_kernel(idx_ref, x_ref, gate_ref, tab_ref, w_ref, g0_s, g1_s, d_s, *, tb):
    ri = lax.broadcasted_iota(jnp.int32, (LANES, LANES), 0)
    ci = lax.broadcasted_iota(jnp.int32, (LANES, LANES), 1)
    eye = ri == ci
    n_groups = PEER_PAIRS // GROUP

    def compute(t, g_s):
        g = _gathered(g_s)
        row = x_ref[pl.ds(t, 1), :]
        x8 = jnp.concatenate([row[:, r * CHUNK_W:(r + 1) * CHUNK_W] for r in range(ROW_CHUNKS)], axis=0)
        xrep = jnp.tile(x8, (GROUP, 1)).astype(BF16)
        z = _nt_dot(g, xrep)
        for gi in range(n_groups):
            zz = jnp.where(eye, z[gi * LANES:(gi + 1) * LANES, :], 0.0)
            d_s[gi, pl.ds(t, 1), :] = jnp.sum(zz, axis=0, keepdims=True)

    _token_loop(tb, idx_ref, tab_ref, g0_s, g1_s, compute)
    act = jnp.zeros((tb, PEER_PAIRS), F32)
    for gi in range(n_groups):
        fold = (ri // ROW_CHUNKS + gi * GROUP == ci).astype(F32)
        act = act + jnp.dot(d_s[gi], fold, preferred_element_type=F32, precision=HIGHEST)
    gelu = 0.5 * act * (1.0 + lax.erf(act * (2.0 ** -0.5)))
    w_ref[...] = gate_ref[...] * gelu


def _table_spec(tab):
    return pl.BlockSpec(tab.shape, lambda i: (0, 0, 0), pipeline_mode=pl.Buffered(1))


def _peer_act(eidx, x3, gate, tab, tb):
    n = x3.shape[0]
    kern = functools.partial(_peer_act_kernel, tb=tb)
    return pl.pallas_call(
        kern,
        grid=(n // tb,),
        in_specs=[pl.BlockSpec((tb, PEER_PAIRS), lambda i: (i, 0), memory_space=pltpu.SMEM),
                  pl.BlockSpec((tb, D_MODEL), lambda i: (i, 0)),
                  pl.BlockSpec((tb, PEER_PAIRS), lambda i: (i, 0)),
                  _table_spec(tab)],
        out_specs=pl.BlockSpec((tb, PEER_PAIRS), lambda i: (i, 0)),
        out_shape=jax.ShapeDtypeStruct((n, PEER_PAIRS), F32),
        scratch_shapes=[pltpu.VMEM((PEER_PAIRS * ROW_CHUNKS, CHUNK_W), BF16)] * 2 + [
                        pltpu.VMEM((PEER_PAIRS // GROUP, tb, LANES), F32)],
        compiler_params=_cparams(("arbitrary",)),
        name="peer_act",
    )(eidx, x3, gate, tab)


def _peer_mix_kernel(idx_ref, w_ref, y_ref, g2_ref, tab_ref, o_ref, g0_s, g1_s, wh_s, wl_s, *, tb, per_token_mod):
    kdim = PEER_PAIRS * ROW_CHUNKS
    pi = lax.broadcasted_iota(jnp.int32, (PEER_PAIRS, kdim), 0)
    ki = lax.broadcasted_iota(jnp.int32, (PEER_PAIRS, kdim), 1)
    rep = (ki // ROW_CHUNKS == pi).astype(BF16)
    w = w_ref[...]
    hi = w.astype(BF16)
    lo = (w - hi.astype(F32)).astype(BF16)
    wh_s[...] = jnp.dot(hi, rep, preferred_element_type=F32)
    wl_s[...] = jnp.dot(lo, rep, preferred_element_type=F32)
    sub = lax.broadcasted_iota(jnp.int32, (ROW_CHUNKS, kdim), 0)
    col = lax.broadcasted_iota(jnp.int32, (ROW_CHUNKS, kdim), 1)
    diag = col % ROW_CHUNKS == sub

    def compute(t, g_s):
        g = _gathered(g_s)
        wmat = jnp.concatenate([jnp.where(diag, wh_s[pl.ds(t, 1), :], 0.0),
                                jnp.where(diag, wl_s[pl.ds(t, 1), :], 0.0)], axis=0).astype(BF16)
        res = jnp.dot(wmat, g, preferred_element_type=F32)
        out = res[0:ROW_CHUNKS, :] + res[ROW_CHUNKS:2 * ROW_CHUNKS, :]
        g2 = g2_ref[pl.ds(t, 1), :] if per_token_mod else g2_ref[...]
        out_row = jnp.concatenate([out[r:r + 1, :] for r in range(ROW_CHUNKS)], axis=1)
        o_ref[pl.ds(t, 1), :] = y_ref[pl.ds(t, 1), :] + g2 * out_row

    _token_loop(tb, idx_ref, tab_ref, g0_s, g1_s, compute)


def _peer_mix(eidx, w, y3, gate2, per_token_mod, toks_per_seq, tab, tb):
    n = y3.shape[0]
    if per_token_mod:
        g_spec = pl.BlockSpec((tb, D_MODEL), lambda i: (i, 0))
    else:
        blocks_per_seq = toks_per_seq // tb
        g_spec = pl.BlockSpec((None, 1, D_MODEL), lambda i: (i // blocks_per_seq, 0, 0))
    kern = functools.partial(_peer_mix_kernel, tb=tb, per_token_mod=per_token_mod)
    return pl.pallas_call(
        kern,
        grid=(n // tb,),
        in_specs=[pl.BlockSpec((tb, PEER_PAIRS), lambda i: (i, 0), memory_space=pltpu.SMEM),
                  pl.BlockSpec((tb, PEER_PAIRS), lambda i: (i, 0)),
                  pl.BlockSpec((tb, D_MODEL), lambda i: (i, 0)), g_spec, _table_spec(tab)],
        out_specs=pl.BlockSpec((tb, D_MODEL), lambda i: (i, 0)),
        out_shape=jax.ShapeDtypeStruct((n, D_MODEL), F32),
        scratch_shapes=[pltpu.VMEM((PEER_PAIRS * ROW_CHUNKS, CHUNK_W), BF16)] * 2 + [
                        pltpu.VMEM((tb, PEER_PAIRS * ROW_CHUNKS), F32),
                        pltpu.VMEM((tb, PEER_PAIRS * ROW_CHUNKS), F32)],
        compiler_params=_cparams(("arbitrary",)),
        name="peer_mix",
    )(eidx, w, y3, gate2, tab)


def _layer(x, mod, conv_buf, C0, n0, m0, k_buf, v_buf, rel_bias, wts, tm, tb):
    B, T, _ = x.shape
    n = B * T
    x2 = x.reshape(n, D_MODEL)
    shift1, scale1, gate1, shift2, scale2, gate2 = jnp.split(mod, 6, axis=-1)
    per_token = T % tm != 0
    if per_token:
        expand = lambda a: jnp.repeat(a, T, axis=0)
    else:
        expand = lambda a: a.reshape(B, 1, D_MODEL)

    q, k, v, mqk, mv, mo, gcol, grow = _in_proj(
        x2, expand(scale1), expand(shift1), per_token, T, tm, wts["g1"], wts["wm"], wts["wgc"], wts["wgr"],
        wts["bd"], wts["qg"], wts["kg"])

    if k_buf is None:
        att = _att_prompt(q, k, v, _prompt_bias(rel_bias), B, T)
        k_new = k.reshape(B, T, H_B, HD_B)
        v_new = v.reshape(B, T, H_B, HD_B)
    else:
        wbuf = k_buf.shape[1]
        bias_c, bias_n = _sample_bias(rel_bias, T, wbuf)
        att, k_new, v_new = _att_sample(q, k, v, k_buf.reshape(B, wbuf, W_B), v_buf.reshape(B, wbuf, W_B),
                                        bias_c, bias_n, B, T, wbuf)
        k_new = k_new.reshape(B, wbuf, H_B, HD_B)
        v_new = v_new.reshape(B, wbuf, H_B, HD_B)

    ha, C, nn, m = _mlstm(mqk, mv, gcol, grow, wts["conv_w"], wts["conv_b"], conv_buf, wts["b_gate"], C0, n0, m0, B, T)
    mqk3 = mqk.reshape(B, T, 2 * W_A)
    if T >= CONV_W - 1:
        conv_new = mqk3[:, T - (CONV_W - 1):]
    else:
        conv_new = jnp.concatenate([conv_buf, mqk3], axis=1)[:, -(CONV_W - 1):]

    y1, h2 = _out_proj(ha, mo, att, x2, expand(gate1), expand(scale2), expand(shift2), per_token, T, tm,
                       wts["mg"], wts["ag"], wts["g2"], wts["bd"], wts["wo"])

    eidx, gate = _peer_route(h2, wts["wq"], wts["keys"], tb)
    w = _peer_act(eidx, h2, gate, wts["u_tab"], tb)
    y = _peer_mix(eidx, w, y1, expand(gate2), per_token, T, wts["v_tab"], tb)
    return (y.reshape(B, T, D_MODEL), k_new, v_new, conv_new, C, nn.reshape(B, H_A, DK_A), m[:, :, 0, 0])


def _prep_weights(l, norm1_g, norm2_g, w_in, b_gate, conv_w, conv_b, q_norm_g, k_norm_g, att_out_g, mlstm_out_g,
                  w_out, peer_wq, peer_keys, peer_u, peer_v):
    w = w_in[l]
    wg = w[:, W_MAIN:]
    grp = np.arange(W_B) // HD_B
    bd = jnp.asarray((grp[:, None] == grp[None, :]).astype(np.float32) / HD_B, BF16)
    return dict(
        g1=norm1_g[l].reshape(1, -1), g2=norm2_g[l].reshape(1, -1),
        wm=w[:, :W_MAIN].astype(BF16),
        wgc=jnp.pad(wg, ((0, 0), (0, LANES - 2 * H_A))), wgr=wg.T,
        bd=bd, qg=jnp.tile(q_norm_g[l], H_B).reshape(1, -1), kg=jnp.tile(k_norm_g[l], H_B).reshape(1, -1),
        conv_w=conv_w[l], conv_b=conv_b[l], b_gate=b_gate[l],
        mg=mlstm_out_g[l].reshape(1, -1), ag=att_out_g[l].reshape(1, -1),
        wo=w_out[l].astype(BF16), wq=peer_wq[l].astype(BF16), keys=peer_keys[l],
        u_tab=_pack_table(peer_u[l]), v_tab=_pack_table(peer_v[l]),
    )


def kernel(x_prompt, x_sample, cache_k, cache_v, state_conv, state_C, state_n, state_m, c_prompt, c_sample,
           rel_bias, w_ada, b_ada, norm1_g, norm2_g, w_in, b_gate, conv_w, conv_b, q_norm_g, k_norm_g,
           att_out_g, mlstm_out_g, w_out, peer_wq, peer_keys, peer_u, peer_v):
    depth = w_ada.shape[0]
    bp = x_prompt.shape[0]
    bs = x_sample.shape[0]
    yp, ys = x_prompt, x_sample
    sp, ss = [], []
    for l in range(depth):
        wts = _prep_weights(l, norm1_g, norm2_g, w_in, b_gate, conv_w, conv_b, q_norm_g, k_norm_g, att_out_g,
                            mlstm_out_g, w_out, peer_wq, peer_keys, peer_u, peer_v)
        mod = _ada(jnp.concatenate([c_prompt, c_sample], axis=0), w_ada[l], b_ada[l])
        zc = jnp.zeros((bp, CONV_W - 1, 2 * W_A), F32)
        zC = jnp.zeros((bp, H_A, DK_A, DK_A), F32)
        zn = jnp.zeros((bp, H_A, DK_A), F32)
        zm = jnp.zeros((bp, H_A), F32)
        outp = _layer(yp, mod[:bp], zc, zC, zn, zm, None, None, rel_bias, wts, tm=256, tb=128)
        outs = _layer(ys, mod[bp:], state_conv[l], state_C[l], state_n[l], state_m[l], cache_k[l], cache_v[l],
                      rel_bias, wts, tm=256, tb=128)
        yp, ys = outp[0], outs[0]
        sp.append(outp[1:])
        ss.append(outs[1:])
    k_p, v_p, conv_p, C_p, n_p, m_p = [jnp.stack([s[i] for s in sp], axis=0) for i in range(6)]
    k_s, v_s, conv_s, C_s, n_s, m_s = [jnp.stack([s[i] for s in ss], axis=0) for i in range(6)]
    return (yp, ys, k_p, v_p, conv_p, C_p, n_p, m_p, k_s, v_s, conv_s, C_s, n_s, m_s)
```

```python
import functools
import math

import numpy as np
import jax
import jax.numpy as jnp
from jax import lax
from jax.experimental import pallas as pl
from jax.experimental.pallas import tpu as pltpu

F32 = jnp.float32
BF16 = jnp.bfloat16
HIGHEST = lax.Precision.HIGHEST

D_MODEL = 1024
H_A = 4
DK_A = 128
W_A = H_A * DK_A
H_B = 8
HD_B = 64
W_B = H_B * HD_B
CONV_W = 4
MLSTM_CHUNK = 64
MLSTM_HP = 4
DIL_PATTERNS = ((128, 1), (512, 4), (2048, 16))
ATT_STEPS = 128
ATT_UNROLL = 4
N_BUCKETS = 32
MAX_DIST = 2048
PEER_HEADS = 8
N_SUBKEYS = 128
PEER_TOPK = 16
PEER_PAIRS = PEER_HEADS * PEER_TOPK
EPS = 1e-6
NEG = -1e30
W_MAIN = 3 * W_B + 4 * W_A
LANES = 128
VMEM_LIMIT = 56 * 1024 * 1024


def _cparams(sem):
    return pltpu.CompilerParams(dimension_semantics=sem, vmem_limit_bytes=VMEM_LIMIT)


def _nt_dot(a, b, precision=None):
    return lax.dot_general(a, b, (((1,), (1,)), ((), ())), preferred_element_type=F32, precision=precision)


def _t5_bucket(dist):
    max_exact = N_BUCKETS // 2
    d = np.maximum(dist, 1).astype(np.float32)
    large = max_exact + (np.log(d / max_exact) / math.log(MAX_DIST / max_exact) * (N_BUCKETS - max_exact)).astype(np.int32)
    large = np.minimum(large, N_BUCKETS - 1)
    return np.where(dist < max_exact, dist, large).astype(np.int32)


def _ada_kernel(c_ref, w_ref, b_ref, o_ref):
    c = c_ref[...]
    s = c * jax.nn.sigmoid(c)
    o_ref[...] = jnp.dot(s, w_ref[...], preferred_element_type=F32, precision=HIGHEST) + b_ref[...]


def _ada(c_all, w_ada, b_ada):
    n = c_all.shape[0]
    return pl.pallas_call(
        _ada_kernel,
        grid=(6,),
        in_specs=[pl.BlockSpec((n, D_MODEL), lambda j: (0, 0)),
                  pl.BlockSpec((D_MODEL, D_MODEL), lambda j: (0, j)),
                  pl.BlockSpec((1, D_MODEL), lambda j: (0, j))],
        out_specs=pl.BlockSpec((n, D_MODEL), lambda j: (0, j)),
        out_shape=jax.ShapeDtypeStruct((n, 6 * D_MODEL), F32),
        compiler_params=_cparams(("arbitrary",)),
        name="ada",
    )(c_all, w_ada, b_ada.reshape(1, -1))


def _group_mean_sq(a, bd):
    sq = a * a
    hi = sq.astype(BF16)
    lo = (sq - hi.astype(F32)).astype(BF16)
    return jnp.dot(hi, bd, preferred_element_type=F32) + jnp.dot(lo, bd, preferred_element_type=F32)


def _in_kernel(x_ref, sc_ref, sh_ref, g1_ref, wm_ref, wgc_ref, wgr_ref, bd_ref, qg_ref, kg_ref,
               q_ref, k_ref, v_ref, mqk_ref, mv_ref, mo_ref, gcol_ref, grow_ref):
    x = x_ref[...]
    ms = jnp.mean(x * x, axis=-1, keepdims=True)
    h = x * lax.rsqrt(ms + EPS) * g1_ref[...]
    h = h * (1.0 + sc_ref[...]) + sh_ref[...]
    y = jnp.dot(h.astype(BF16), wm_ref[...], preferred_element_type=F32)
    bd = bd_ref[...]
    aq = y[:, 0:W_B]
    ak = y[:, W_B:2 * W_B]
    q_ref[...] = aq * lax.rsqrt(_group_mean_sq(aq, bd) + EPS) * qg_ref[...] * (HD_B ** -0.5)
    k_ref[...] = ak * lax.rsqrt(_group_mean_sq(ak, bd) + EPS) * kg_ref[...]
    v_ref[...] = y[:, 2 * W_B:3 * W_B]
    o = 3 * W_B
    mqk_ref[...] = y[:, o:o + 2 * W_A]
    mv_ref[...] = y[:, o + 2 * W_A:o + 3 * W_A]
    mo_ref[...] = y[:, o + 3 * W_A:o + 4 * W_A]
    gcol_ref[...] = jnp.dot(h, wgc_ref[...], preferred_element_type=F32, precision=HIGHEST)
    grow_ref[...] = _nt_dot(wgr_ref[...], h, precision=HIGHEST)


def _in_proj(x2, scale, shift, per_token_mod, toks_per_seq, tm, g1, wm, wgc, wgr, bd, qg, kg):
    n = x2.shape[0]
    nt = n // tm
    if per_token_mod:
        mod_spec = pl.BlockSpec((tm, D_MODEL), lambda i: (i, 0))
    else:
        tiles_per_seq = toks_per_seq // tm
        mod_spec = pl.BlockSpec((None, 1, D_MODEL), lambda i: (i // tiles_per_seq, 0, 0))
    const = lambda shape: pl.BlockSpec(shape, lambda i: (0,) * len(shape))
    tok = lambda w: pl.BlockSpec((tm, w), lambda i: (i, 0))
    outs = pl.pallas_call(
        _in_kernel,
        grid=(nt,),
        in_specs=[tok(D_MODEL), mod_spec, mod_spec, const((1, D_MODEL)), const((D_MODEL, W_MAIN)),
                  const((D_MODEL, LANES)), const((8, D_MODEL)), const((W_B, W_B)), const((1, W_B)), const((1, W_B))],
        out_specs=[tok(W_B), tok(W_B), tok(W_B), tok(2 * W_A), tok(W_A), tok(W_A), tok(LANES),
                   pl.BlockSpec((8, tm), lambda i: (0, i))],
        out_shape=[jax.ShapeDtypeStruct((n, W_B), F32)] * 3 + [jax.ShapeDtypeStruct((n, 2 * W_A), F32)]
                  + [jax.ShapeDtypeStruct((n, W_A), F32)] * 2 + [jax.ShapeDtypeStruct((n, LANES), F32),
                                                                 jax.ShapeDtypeStruct((8, n), F32)],
        compiler_params=_cparams(("arbitrary",)),
        name="in_proj",
    )(x2, scale, shift, g1, wm, wgc, wgr, bd, qg, kg)
    return outs


def _log_sigmoid(x):
    return jnp.minimum(x, 0.0) - jnp.log1p(jnp.exp(-jnp.abs(x)))


def _mlstm_kernel(bg_ref, m0_ref, mq_ref, mk_ref, cwq_ref, cwk_ref, cbq_ref, cbk_ref, bufq_ref, bufk_ref,
                  v_ref, gcol_ref, grow_ref, c0_ref, n0_ref,
                  h_ref, c_ref, n_ref, m_ref, sq, sk, *, T, L, rows_whole):
    b = pl.program_id(0)
    hg = pl.program_id(1)
    nc = T // L

    def conv(u_ref, buf_ref, w_ref, cb_ref, s_ref, out_ref):
        s_ref[0:8, :] = jnp.zeros((8, MLSTM_HP * LANES), F32)
        s_ref[5:8, :] = buf_ref[...]
        s_ref[8:8 + T, :] = u_ref[...]
        y = cb_ref[...]
        for j in range(CONV_W):
            y = y + s_ref[5 + j:5 + j + T, :] * w_ref[j:j + 1, :]
        out_ref[...] = y * jax.nn.sigmoid(y)

    conv(mq_ref, bufq_ref, cwq_ref, cbq_ref, sq, h_ref)
    conv(mk_ref, bufk_ref, cwk_ref, cbk_ref, sk, sq.at[8:8 + T, :])

    lane = lax.broadcasted_iota(jnp.int32, (L, LANES), 1)
    ri = lax.broadcasted_iota(jnp.int32, (L, L), 0)
    ci = lax.broadcasted_iota(jnp.int32, (L, L), 1)
    causal = ri >= ci
    tri = causal.astype(F32)
    tri_t = (ri <= ci).astype(F32)

    def chunk(c, carry):
        r0 = pl.multiple_of(c * L, L)
        rc = c + (b * nc if rows_whole else 0)
        g = gcol_ref[pl.ds(r0, L), :]
        st = []
        for j in range(MLSTM_HP):
            C, n, m = carry[j]
            hd = hg * MLSTM_HP + j
            ig_b = bg_ref[hd]
            f_b = bg_ref[H_A + hd]
            cs = slice(j * LANES, (j + 1) * LANES)
            qc = h_ref[pl.ds(r0, L), cs]
            kc = sq[pl.ds(r0 + 8, L), cs] * (DK_A ** -0.5)
            vb = v_ref[pl.ds(r0, L), cs].astype(BF16)
            ig_col = jnp.sum(jnp.where(lane == hd, g, 0.0), axis=1, keepdims=True) + ig_b
            f_col = jnp.sum(jnp.where(lane == hd + H_A, g, 0.0), axis=1, keepdims=True) + f_b
            ig_row = grow_ref[hd, pl.ds(rc, 1), :] + ig_b
            lf_row = _log_sigmoid(grow_ref[hd + H_A, pl.ds(rc, 1), :] + f_b)
            b_col = jnp.dot(tri, jnp.broadcast_to(_log_sigmoid(f_col), (L, LANES)), preferred_element_type=F32,
                            precision=HIGHEST)[:, 0:1]
            b_row = jnp.dot(jnp.broadcast_to(lf_row, (8, L)), tri_t, preferred_element_type=F32,
                            precision=HIGHEST)[0:1, :]
            qb = qc.astype(BF16)
            qk = _nt_dot(qb, kc.astype(BF16))
            qC = jnp.dot(qb, C.astype(BF16), preferred_element_type=F32)
            st.append(dict(C=C, n=n, m=m, cs=cs, qc=qc, kc=kc, vb=vb, ig_col=ig_col, ig_row=ig_row,
                           b_col=b_col, b_row=b_row, qk=qk, qC=qC))
        for d in st:
            b_col, b_row, m = d["b_col"], d["b_row"], d["m"]
            g_col = b_col + m
            dm = jnp.where(causal, b_col - b_row + d["ig_row"], -jnp.inf)
            mt = jnp.maximum(g_col, jnp.max(dm, axis=1, keepdims=True))
            s = d["qk"] * jnp.exp(dm - mt)
            wg = jnp.exp(g_col - mt)
            bl = b_col[L - 1:L, :]
            m_new = jnp.maximum(bl + m, jnp.max(bl - b_row + d["ig_row"], axis=1, keepdims=True))
            kw = jnp.exp(bl - b_col + d["ig_col"] - m_new) * d["kc"]
            d.update(mt=mt, s=s, wg=wg, m_new=m_new, kw=kw, wc=jnp.exp(bl + m - m_new))
        out = []
        for d in st:
            s, wg = d["s"], d["wg"]
            num = wg * d["qC"] + jnp.dot(s.astype(BF16), d["vb"], preferred_element_type=F32)
            den = wg * jnp.sum(d["qc"] * d["n"], axis=1, keepdims=True) + jnp.sum(s, axis=1, keepdims=True)
            h_ref[pl.ds(r0, L), d["cs"]] = num / jnp.maximum(jnp.abs(den), jnp.exp(-d["mt"]))
            upd = lax.dot_general(d["kw"].astype(BF16), d["vb"], (((0,), (0,)), ((), ())), preferred_element_type=F32)
            out.append((d["wc"] * d["C"] + upd, d["wc"] * d["n"] + jnp.sum(d["kw"], axis=0, keepdims=True), d["m_new"]))
        return tuple(out)

    init = tuple((c0_ref[j], n0_ref[j], jnp.full((1, 1), m0_ref[b, hg * MLSTM_HP + j], F32))
                 for j in range(MLSTM_HP))
    final = lax.fori_loop(0, nc, chunk, init)
    for j in range(MLSTM_HP):
        c_ref[j] = final[j][0]
        n_ref[j] = final[j][1]
        m_ref[j] = jnp.broadcast_to(final[j][2], (1, LANES))


def _mlstm(mqk, mv, gcol, grow, conv_w, conv_b, conv_buf, b_gate, C0, n0, m0, B, T):
    L = math.gcd(T, MLSTM_CHUNK)
    nc = T // L
    n = B * T
    rows_whole = nc % 8 != 0
    grow3 = grow.reshape(8, n // L, L)
    if rows_whole:
        grow_spec = pl.BlockSpec((8, n // L, L), lambda b, h: (0, 0, 0))
    else:
        grow_spec = pl.BlockSpec((8, nc, L), lambda b, h: (0, b, 0))
    smem = pl.BlockSpec(memory_space=pltpu.SMEM)
    wide = MLSTM_HP * LANES
    ng = H_A // MLSTM_HP
    colq = lambda rows: pl.BlockSpec((rows, wide), lambda b, h: (0, h))
    colk = lambda rows: pl.BlockSpec((rows, wide), lambda b, h: (0, h + ng))
    st = lambda r, c: pl.BlockSpec((None, MLSTM_HP, r, c), lambda b, h: (b, h, 0, 0))
    kern = functools.partial(_mlstm_kernel, T=T, L=L, rows_whole=rows_whole)
    return pl.pallas_call(
        kern,
        grid=(B, ng),
        in_specs=[smem, smem,
                  pl.BlockSpec((T, wide), lambda b, h: (b, h)), pl.BlockSpec((T, wide), lambda b, h: (b, h + ng)),
                  colq(CONV_W), colk(CONV_W), colq(1), colk(1),
                  pl.BlockSpec((None, CONV_W - 1, wide), lambda b, h: (b, 0, h)),
                  pl.BlockSpec((None, CONV_W - 1, wide), lambda b, h: (b, 0, h + ng)),
                  pl.BlockSpec((T, wide), lambda b, h: (b, h)),
                  pl.BlockSpec((T, LANES), lambda b, h: (b, 0)),
                  grow_spec, st(DK_A, DK_A), st(1, DK_A)],
        out_specs=[pl.BlockSpec((T, wide), lambda b, h: (b, h)), st(DK_A, DK_A), st(1, DK_A), st(1, LANES)],
        out_shape=[jax.ShapeDtypeStruct((n, W_A), F32), jax.ShapeDtypeStruct((B, H_A, DK_A, DK_A), F32),
                   jax.ShapeDtypeStruct((B, H_A, 1, DK_A), F32), jax.ShapeDtypeStruct((B, H_A, 1, LANES), F32)],
        scratch_shapes=[pltpu.VMEM((T + 8, wide), F32), pltpu.VMEM((T + 8, wide), F32)],
        compiler_params=_cparams(("arbitrary", "arbitrary")),
        name="mlstm",
    )(b_gate, m0, mqk, mqk, conv_w, conv_w, conv_b.reshape(1, -1), conv_b.reshape(1, -1), conv_buf, conv_buf,
      mv, gcol, grow3, C0, n0.reshape(B, H_A, 1, DK_A))


def _att_prompt_kernel(q_ref, k_ref, v_ref, bias_ref, o_ref, num_s, mx_s, dn_s, mxs_s, *, T):
    lane = lax.broadcasted_iota(jnp.int32, (ATT_STEPS, LANES), 1)
    first = lane < HD_B
    nblk = T // ATT_STEPS

    for br, (_, dil) in enumerate(DIL_PATTERNS):
        span = ATT_STEPS * dil
        has_prev = T > span

        def rows(start, dil=dil):
            return pl.ds(start, ATT_STEPS, stride=dil) if dil > 1 else pl.ds(start, ATT_STEPS)

        def blk(i0, carry, br=br, dil=dil, span=span, has_prev=has_prev, rows=rows):
            units = []
            for u in range(ATT_UNROLL):
                i = i0 + u * (nblk // ATT_UNROLL)
                nb = i // dil
                start = nb * span + i % dil
                qb = q_ref[rows(start), :]
                kcur = k_ref[rows(start), :].astype(BF16)
                kprev = k_ref[rows(jnp.maximum(start - span, 0)), :].astype(BF16) if has_prev else None
                for hh in range(2):
                    own = first if hh == 0 else ~first
                    qh = jnp.where(own, qb, 0.0).astype(BF16)
                    lc = _nt_dot(qh, kcur) + bias_ref[br, hh, :, ATT_STEPS:2 * ATT_STEPS]
                    lp = None
                    if has_prev:
                        lp = _nt_dot(qh, kprev) + bias_ref[br, hh, :, 0:ATT_STEPS]
                        lp = jnp.where(nb > 0, lp, NEG)
                    units.append((start, own, lc, lp))
            weights = []
            for start, own, lc, lp in units:
                mx = jnp.max(jnp.maximum(lc, lp) if has_prev else lc, axis=1, keepdims=True)
                pc = jnp.exp(lc - mx).astype(BF16)
                pp = jnp.exp(lp - mx).astype(BF16) if has_prev else None
                weights.append((mx, pc, pp))
            res = []
            for (start, own, _, _), (mx, pc, pp) in zip(units, weights):
                acc = jnp.dot(pc, jnp.where(own, v_ref[rows(start), :], 1.0).astype(BF16), preferred_element_type=F32)
                if has_prev:
                    vprev = v_ref[rows(jnp.maximum(start - span, 0)), :]
                    acc = acc + jnp.dot(pp, jnp.where(own, vprev, 1.0).astype(BF16), preferred_element_type=F32)
                res.append((acc, mx))
            for u in range(ATT_UNROLL):
                start = units[2 * u][0]
                (a0, m0), (a1, m1) = res[2 * u], res[2 * u + 1]
                num_s[br, rows(start), :] = jnp.where(first, a0, a1)
                dn_s[br, rows(start), :] = jnp.where(first, a1, a0)
                mx_s[br, rows(start), :] = jnp.where(first, m0, m1)
                mxs_s[br, rows(start), :] = jnp.where(first, m1, m0)
            return carry

        lax.fori_loop(0, nblk // ATT_UNROLL, blk, 0)

    def comb(i, carry):
        rs = pl.ds(pl.multiple_of(i * ATT_STEPS, ATT_STEPS), ATT_STEPS)
        m = jnp.maximum(jnp.maximum(mx_s[0, rs, :], mx_s[1, rs, :]), mx_s[2, rs, :])
        ms = jnp.maximum(jnp.maximum(mxs_s[0, rs, :], mxs_s[1, rs, :]), mxs_s[2, rs, :])
        num = jnp.zeros((ATT_STEPS, LANES), F32)
        den = jnp.zeros((ATT_STEPS, LANES), F32)
        for br in range(3):
            num = num + jnp.exp(mx_s[br, rs, :] - m) * num_s[br, rs, :]
            den = den + jnp.exp(mxs_s[br, rs, :] - ms) * dn_s[br, rs, :]
        o_ref[rs, :] = num / pltpu.roll(den, HD_B, axis=1)
        return carry

    lax.fori_loop(0, nblk, comb, 0)


def _att_prompt(q, k, v, bias, B, T):
    n = B * T
    blk = pl.BlockSpec((T, LANES), lambda b, p: (b, p))
    kern = functools.partial(_att_prompt_kernel, T=T)
    return pl.pallas_call(
        kern,
        grid=(B, H_B // 2),
        in_specs=[blk, blk, blk, pl.BlockSpec((3, 2, ATT_STEPS, 2 * ATT_STEPS), lambda b, p: (0, p, 0, 0))],
        out_specs=blk,
        out_shape=jax.ShapeDtypeStruct((n, W_B), F32),
        scratch_shapes=[pltpu.VMEM((3, T, LANES), F32)] * 4,
        compiler_params=_cparams(("arbitrary", "arbitrary")),
        name="att_prompt",
    )(q, k, v, bias)


def _prompt_bias(rel_bias):
    tab = _distance_bias(rel_bias, ATT_STEPS * DIL_PATTERNS[-1][1] + 1)
    pad = jnp.full((H_B, ATT_STEPS - 1), NEG, F32)
    tabs = []
    for _, dil in DIL_PATTERNS:
        w = jnp.concatenate([pad, tab[:, 0:ATT_STEPS * dil + 1:dil], pad], axis=1)[:, ::-1]
        tabs.append(jnp.stack([w[:, ATT_STEPS - 1 - i:3 * ATT_STEPS - 1 - i] for i in range(ATT_STEPS)], axis=1))
    return jnp.stack(tabs, axis=0)


def _distance_bias(rel_bias, n):
    onehot = np.zeros((n, N_BUCKETS), np.float32)
    onehot[np.arange(n), _t5_bucket(np.arange(n))] = 1.0
    return jnp.dot(jnp.asarray(onehot), rel_bias.astype(F32), precision=HIGHEST).T


def _att_sample_kernel(q_ref, kn_ref, vn_ref, ck_ref, cv_ref, bc_ref, bn_ref, o_ref, ko_ref, vo_ref, *, S, WBUF):
    lane = lax.broadcasted_iota(jnp.int32, (S, LANES), 1)
    first = lane < HD_B
    for hp in range(H_B // 2):
        cs = slice(hp * LANES, (hp + 1) * LANES)
        kc = ck_ref[:, cs].astype(BF16)
        vc = cv_ref[:, cs].astype(BF16)
        kn = kn_ref[:, cs].astype(BF16)
        vn = vn_ref[:, cs].astype(BF16)
        qp = q_ref[:, cs]
        res = []
        for hh in range(2):
            h = 2 * hp + hh
            qh = jnp.where(first if hh == 0 else ~first, qp, 0.0).astype(BF16)
            lc = _nt_dot(qh, kc) + bc_ref[h]
            ln = _nt_dot(qh, kn) + bn_ref[h]
            mx = jnp.maximum(jnp.max(lc, axis=1, keepdims=True), jnp.max(ln, axis=1, keepdims=True))
            pc = jnp.exp(lc - mx)
            pn = jnp.exp(ln - mx)
            den = jnp.sum(pc, axis=1, keepdims=True) + jnp.sum(pn, axis=1, keepdims=True)
            num = jnp.dot(pc.astype(BF16), vc, preferred_element_type=F32) \
                + jnp.dot(pn.astype(BF16), vn, preferred_element_type=F32)
            res.append(num / den)
        o_ref[:, cs] = jnp.where(first, res[0], res[1])
    ko_ref[0:WBUF - S, :] = ck_ref[S:WBUF, :]
    ko_ref[WBUF - S:WBUF, :] = kn_ref[...]
    vo_ref[0:WBUF - S, :] = cv_ref[S:WBUF, :]
    vo_ref[WBUF - S:WBUF, :] = vn_ref[...]


def _att_sample(q, kn, vn, ck, cv, bias_c, bias_n, B, S, WBUF):
    tokb = pl.BlockSpec((S, W_B), lambda b: (b, 0))
    cache = pl.BlockSpec((None, WBUF, W_B), lambda b: (b, 0, 0))
    kern = functools.partial(_att_sample_kernel, S=S, WBUF=WBUF)
    return pl.pallas_call(
        kern,
        grid=(B,),
        in_specs=[tokb, tokb, tokb, cache, cache,
                  pl.BlockSpec((H_B, S, WBUF), lambda b: (0, 0, 0)), pl.BlockSpec((H_B, S, S), lambda b: (0, 0, 0))],
        out_specs=[tokb, cache, cache],
        out_shape=[jax.ShapeDtypeStruct((B * S, W_B), F32), jax.ShapeDtypeStruct((B, WBUF, W_B), F32),
                   jax.ShapeDtypeStruct((B, WBUF, W_B), F32)],
        compiler_params=_cparams(("arbitrary",)),
        name="att_sample",
    )(q, kn, vn, ck, cv, bias_c, bias_n)


def _sample_bias(rel_bias, S, WBUF):
    dist = np.arange(WBUF + S)
    mult = np.zeros(WBUF + S, np.int64)
    for w, dil in DIL_PATTERNS:
        mult += ((dist % dil == 0) & (dist <= w)).astype(np.int64)
    logm = np.where(mult > 0, np.log(np.maximum(mult, 1)), 0.0).astype(np.float32)
    tab = _distance_bias(rel_bias, WBUF + S) + logm[None, :]
    tab = jnp.where((mult > 0)[None, :], tab, NEG)
    rev = jnp.concatenate([tab[:, ::-1], jnp.full((H_B, S), NEG, F32)], axis=1)
    last = WBUF + S - 1
    bias_c = jnp.stack([rev[:, S - 1 - s:S - 1 - s + WBUF] for s in range(S)], axis=1)
    bias_n = jnp.stack([rev[:, last - s:last - s + S] for s in range(S)], axis=1)
    return bias_c, bias_n


def _out_kernel(ha_ref, mo_ref, att_ref, x_ref, g1_ref, sc_ref, sh_ref, mg_ref, ag_ref, g2_ref, bd_ref, wo_ref,
                y_ref, h2_ref):
    ha = ha_ref[...]
    parts = []
    for hd in range(H_A):
        a = ha[:, hd * DK_A:(hd + 1) * DK_A]
        parts.append(a * lax.rsqrt(jnp.mean(a * a, axis=-1, keepdims=True) + EPS))
    hn = jnp.concatenate(parts, axis=1) * mg_ref[...] * jax.nn.sigmoid(mo_ref[...])
    att = att_ref[...]
    an = att * lax.rsqrt(_group_mean_sq(att, bd_ref[...]) + EPS) * ag_ref[...]
    mix = jnp.dot(hn.astype(BF16), wo_ref[0:W_A, :], preferred_element_type=F32) \
        + jnp.dot(an.astype(BF16), wo_ref[W_A:W_A + W_B, :], preferred_element_type=F32)
    y = x_ref[...] + g1_ref[...] * mix
    y_ref[...] = y
    h2 = y * lax.rsqrt(jnp.mean(y * y, axis=-1, keepdims=True) + EPS) * g2_ref[...]
    h2_ref[...] = h2 * (1.0 + sc_ref[...]) + sh_ref[...]


def _out_proj(ha, mo, att, x2, gate1, scale2, shift2, per_token_mod, toks_per_seq, tm, mg, ag, g2, bd, wo):
    n = x2.shape[0]
    if per_token_mod:
        mod_spec = pl.BlockSpec((tm, D_MODEL), lambda i: (i, 0))
    else:
        tiles_per_seq = toks_per_seq // tm
        mod_spec = pl.BlockSpec((None, 1, D_MODEL), lambda i: (i // tiles_per_seq, 0, 0))
    const = lambda shape: pl.BlockSpec(shape, lambda i: (0,) * len(shape))
    tok = lambda w: pl.BlockSpec((tm, w), lambda i: (i, 0))
    return pl.pallas_call(
        _out_kernel,
        grid=(n // tm,),
        in_specs=[tok(W_A), tok(W_A), tok(W_B), tok(D_MODEL), mod_spec, mod_spec, mod_spec,
                  const((1, W_A)), const((1, W_B)), const((1, D_MODEL)), const((W_B, W_B)),
                  const((W_A + W_B, D_MODEL))],
        out_specs=[tok(D_MODEL), tok(D_MODEL)],
        out_shape=[jax.ShapeDtypeStruct((n, D_MODEL), F32)] * 2,
        compiler_params=_cparams(("arbitrary",)),
        name="out_proj",
    )(ha, mo, att, x2, gate1, scale2, shift2, mg, ag, g2, bd, wo)


def _top16(s, nrows):
    iota = lax.broadcasted_iota(jnp.int32, s.shape, 0).astype(F32)
    vals, idxs = [], []
    for _ in range(PEER_TOPK):
        m = jnp.max(s, axis=0, keepdims=True)
        pos = jnp.min(jnp.where(s == m, iota, float(nrows)), axis=0, keepdims=True)
        vals.append(m)
        idxs.append(pos)
        s = jnp.where(iota == pos, -jnp.inf, s)
    return jnp.concatenate(vals, axis=0), jnp.concatenate(idxs, axis=0)


def _peer_route_kernel(h2_ref, wq_ref, keys_ref, eidx_ref, gate_ref, sv_s, si_s, gt_s, et_s, *, tm):
    qh = jnp.dot(h2_ref[...].astype(BF16), wq_ref[...], preferred_element_type=F32)
    for hp in range(2 * PEER_HEADS):
        qs = qh[:, hp * N_SUBKEYS:(hp + 1) * N_SUBKEYS]
        s = _nt_dot(keys_ref[hp % 2], qs, precision=HIGHEST)
        v, i = _top16(s, N_SUBKEYS)
        sv_s[hp] = v
        si_s[hp] = i

    k = PEER_TOPK
    half = k // 2
    sub = lax.broadcasted_iota(jnp.int32, (half, tm), 0).astype(F32)
    pos = jnp.concatenate(
        [lax.broadcasted_iota(jnp.int32, (k, tm), 0).astype(F32)]
        + [float(a * k) + sub for a in range(1, half)] + [(sub + float(half)) * float(k)], axis=0)
    n_cand = float(k * k)

    def head(h, carry):
        sv0 = sv_s[2 * h]
        sv1 = sv_s[2 * h + 1]
        si0 = si_s[2 * h] * float(N_SUBKEYS)
        si1 = si_s[2 * h + 1]
        cands = [sv0[0:1, :] + sv1]
        cidxs = [si0[0:1, :] + si1]
        for a in range(1, half):
            c = sv0[a:a + 1, :] + sv1[0:half, :]
            nb = k // (a + 1)
            cands.append(c if nb >= half else jnp.where(sub < float(nb), c, -jnp.inf))
            cidxs.append(si0[a:a + 1, :] + si1[0:half, :])
        cands.append(sv0[half:k, :] + sv1[0:1, :])
        cidxs.append(si0[half:k, :] + si1[0:1, :])
        cand = jnp.concatenate(cands, axis=0)
        cidx = jnp.concatenate(cidxs, axis=0)
        fv, ev = [], []
        for _ in range(k):
            m = jnp.max(cand, axis=0, keepdims=True)
            first = jnp.min(jnp.where(cand == m, pos, n_cand), axis=0, keepdims=True)
            sel = pos == first
            ev.append(jnp.max(jnp.where(sel, cidx, -1.0), axis=0, keepdims=True))
            fv.append(m)
            cand = jnp.where(sel, -jnp.inf, cand)
        fvs = jnp.concatenate(fv, axis=0)
        e = jnp.exp(fvs - fv[0])
        rs = pl.ds(pl.multiple_of(h * k, k), k)
        gt_s[rs, :] = e / jnp.sum(e, axis=0, keepdims=True)
        et_s[rs, :] = jnp.concatenate(ev, axis=0)
        return carry

    lax.fori_loop(0, PEER_HEADS, head, 0)
    for j in range(tm // LANES):
        cs = slice(j * LANES, (j + 1) * LANES)
        gate_ref[cs, :] = gt_s[:, cs].T
        eidx_ref[cs, :] = et_s[:, cs].T.astype(jnp.int32)


def _peer_route(h2, wq, keys, tm):
    n = h2.shape[0]
    kern = functools.partial(_peer_route_kernel, tm=tm)
    return pl.pallas_call(
        kern,
        grid=(n // tm,),
        in_specs=[pl.BlockSpec((tm, D_MODEL), lambda i: (i, 0)),
                  pl.BlockSpec((D_MODEL, 2 * PEER_HEADS * N_SUBKEYS), lambda i: (0, 0)),
                  pl.BlockSpec((2, N_SUBKEYS, N_SUBKEYS), lambda i: (0, 0, 0))],
        out_specs=[pl.BlockSpec((tm, PEER_PAIRS), lambda i: (i, 0)), pl.BlockSpec((tm, PEER_PAIRS), lambda i: (i, 0))],
        out_shape=[jax.ShapeDtypeStruct((n, PEER_PAIRS), jnp.int32), jax.ShapeDtypeStruct((n, PEER_PAIRS), F32)],
        scratch_shapes=[pltpu.VMEM((2 * PEER_HEADS, PEER_TOPK, tm), F32)] * 2 + [pltpu.VMEM((PEER_PAIRS, tm), F32)] * 2,
        compiler_params=_cparams(("arbitrary",)),
        name="peer_route",
    )(h2, wq, keys)


CHUNK_W = LANES
ROW_CHUNKS = D_MODEL // CHUNK_W
GROUP = LANES // ROW_CHUNKS


def _pack_table(tab):
    return tab.astype(BF16).reshape(tab.shape[0], ROW_CHUNKS, CHUNK_W)


def _gather_rows(idx_ref, t, tab_ref, g_s):
    for p in range(PEER_PAIRS):
        g_s[ROW_CHUNKS * p:ROW_CHUNKS * (p + 1), :] = tab_ref[idx_ref[t, p]]


def _gathered(g_s):
    return g_s[...]


def _token_loop(tb, idx_ref, tab_ref, g0_s, g1_s, compute):
    _gather_rows(idx_ref, 0, tab_ref, g0_s)

    def step(i, carry):
        t0 = 2 * i
        _gather_rows(idx_ref, t0 + 1, tab_ref, g1_s)
        compute(t0, g0_s)
        _gather_rows(idx_ref, jnp.minimum(t0 + 2, tb - 1), tab_ref, g0_s)
        compute(t0 + 1, g1_s)
        return carry

    lax.fori_loop(0, tb // 2, step, 0)


def _peer_act_kernel(idx_ref, x_ref, gate_ref, tab_ref, w_ref, g0_s, g1_s, d_s, *, tb):
    ri = lax.broadcasted_iota(jnp.int32, (LANES, LANES), 0)
    ci = lax.broadcasted_iota(jnp.int32, (LANES, LANES), 1)
    eye = ri == ci
    n_groups = PEER_PAIRS // GROUP

    def compute(t, g_s):
        g = _gathered(g_s)
        row = x_ref[pl.ds(t, 1), :]
        x8 = jnp.concatenate([row[:, r * CHUNK_W:(r + 1) * CHUNK_W] for r in range(ROW_CHUNKS)], axis=0)
        xrep = jnp.tile(x8, (GROUP, 1)).astype(BF16)
        z = _nt_dot(g, xrep)
        for gi in range(n_groups):
            zz = jnp.where(eye, z[gi * LANES:(gi + 1) * LANES, :], 0.0)
            d_s[gi, pl.ds(t, 1), :] = jnp.sum(zz, axis=0, keepdims=True)

    _token_loop(tb, idx_ref, tab_ref, g0_s, g1_s, compute)
    act = jnp.zeros((tb, PEER_PAIRS), F32)
    for gi in range(n_groups):
        fold = (ri // ROW_CHUNKS + gi * GROUP == ci).astype(F32)
        act = act + jnp.dot(d_s[gi], fold, preferred_element_type=F32, precision=HIGHEST)
    gelu = 0.5 * act * (1.0 + lax.erf(act * (2.0 ** -0.5)))
    w_ref[...] = gate_ref[...] * gelu


def _table_spec(tab):
    return pl.BlockSpec(tab.shape, lambda i: (0, 0, 0), pipeline_mode=pl.Buffered(1))


def _peer_act(eidx, x3, gate, tab, tb):
    n = x3.shape[0]
    kern = functools.partial(_peer_act_kernel, tb=tb)
    return pl.pallas_call(
        kern,
        grid=(n // tb,),
        in_specs=[pl.BlockSpec((tb, PEER_PAIRS), lambda i: (i, 0), memory_space=pltpu.SMEM),
                  pl.BlockSpec((tb, D_MODEL), lambda i: (i, 0)),
                  pl.BlockSpec((tb, PEER_PAIRS), lambda i: (i, 0)),
                  _table_spec(tab)],
        out_specs=pl.BlockSpec((tb, PEER_PAIRS), lambda i: (i, 0)),
        out_shape=jax.ShapeDtypeStruct((n, PEER_PAIRS), F32),
        scratch_shapes=[pltpu.VMEM((PEER_PAIRS * ROW_CHUNKS, CHUNK_W), BF16)] * 2 + [
                        pltpu.VMEM((PEER_PAIRS // GROUP, tb, LANES), F32)],
        compiler_params=_cparams(("arbitrary",)),
        name="peer_act",
    )(eidx, x3, gate, tab)


def _peer_mix_kernel(idx_ref, w_ref, y_ref, g2_ref, tab_ref, o_ref, g0_s, g1_s, wh_s, wl_s, *, tb, per_token_mod):
    kdim = PEER_PAIRS * ROW_CHUNKS
    pi = lax.broadcasted_iota(jnp.int32, (PEER_PAIRS, kdim), 0)
    ki = lax.broadcasted_iota(jnp.int32, (PEER_PAIRS, kdim), 1)
    rep = (ki // ROW_CHUNKS == pi).astype(BF16)
    w = w_ref[...]
    hi = w.astype(BF16)
    lo = (w - hi.astype(F32)).astype(BF16)
    wh_s[...] = jnp.dot(hi, rep, preferred_element_type=F32)
    wl_s[...] = jnp.dot(lo, rep, preferred_element_type=F32)
    sub = lax.broadcasted_iota(jnp.int32, (ROW_CHUNKS, kdim), 0)
    col = lax.broadcasted_iota(jnp.int32, (ROW_CHUNKS, kdim), 1)
    diag = col % ROW_CHUNKS == sub

    def compute(t, g_s):
        g = _gathered(g_s)
        wmat = jnp.concatenate([jnp.where(diag, wh_s[pl.ds(t, 1), :], 0.0),
                                jnp.where(diag, wl_s[pl.ds(t, 1), :], 0.0)], axis=0).astype(BF16)
        res = jnp.dot(wmat, g, preferred_element_type=F32)
        out = res[0:ROW_CHUNKS, :] + res[ROW_CHUNKS:2 * ROW_CHUNKS, :]
        g2 = g2_ref[pl.ds(t, 1), :] if per_token_mod else g2_ref[...]
        out_row = jnp.concatenate([out[r:r + 1, :] for r in range(ROW_CHUNKS)], axis=1)
        o_ref[pl.ds(t, 1), :] = y_ref[pl.ds(t, 1), :] + g2 * out_row

    _token_loop(tb, idx_ref, tab_ref, g0_s, g1_s, compute)


def _peer_mix(eidx, w, y3, gate2, per_token_mod, toks_per_seq, tab, tb):
    n = y3.shape[0]
    if per_token_mod:
        g_spec = pl.BlockSpec((tb, D_MODEL), lambda i: (i, 0))
    else:
        blocks_per_seq = toks_per_seq // tb
        g_spec = pl.BlockSpec((None, 1, D_MODEL), lambda i: (i // blocks_per_seq, 0, 0))
    kern = functools.partial(_peer_mix_kernel, tb=tb, per_token_mod=per_token_mod)
    return pl.pallas_call(
        kern,
        grid=(n // tb,),
        in_specs=[pl.BlockSpec((tb, PEER_PAIRS), lambda i: (i, 0), memory_space=pltpu.SMEM),
                  pl.BlockSpec((tb, PEER_PAIRS), lambda i: (i, 0)),
                  pl.BlockSpec((tb, D_MODEL), lambda i: (i, 0)), g_spec, _table_spec(tab)],
        out_specs=pl.BlockSpec((tb, D_MODEL), lambda i: (i, 0)),
        out_shape=jax.ShapeDtypeStruct((n, D_MODEL), F32),
        scratch_shapes=[pltpu.VMEM((PEER_PAIRS * ROW_CHUNKS, CHUNK_W), BF16)] * 2 + [
                        pltpu.VMEM((tb, PEER_PAIRS * ROW_CHUNKS), F32),
                        pltpu.VMEM((tb, PEER_PAIRS * ROW_CHUNKS), F32)],
        compiler_params=_cparams(("arbitrary",)),
        name="peer_mix",
    )(eidx, w, y3, gate2, tab)


def _layer(x, mod, conv_buf, C0, n0, m0, k_buf, v_buf, rel_bias, wts, tm, tb):
    B, T, _ = x.shape
    n = B * T
    x2 = x.reshape(n, D_MODEL)
    shift1, scale1, gate1, shift2, scale2, gate2 = jnp.split(mod, 6, axis=-1)
    per_token = T % tm != 0
    if per_token:
        expand = lambda a: jnp.repeat(a, T, axis=0)
    else:
        expand = lambda a: a.reshape(B, 1, D_MODEL)

    q, k, v, mqk, mv, mo, gcol, grow = _in_proj(
        x2, expand(scale1), expand(shift1), per_token, T, tm, wts["g1"], wts["wm"], wts["wgc"], wts["wgr"],
        wts["bd"], wts["qg"], wts["kg"])

    if k_buf is None:
        att = _att_prompt(q, k, v, _prompt_bias(rel_bias), B, T)
        k_new = k.reshape(B, T, H_B, HD_B)
        v_new = v.reshape(B, T, H_B, HD_B)
    else:
        wbuf = k_buf.shape[1]
        bias_c, bias_n = _sample_bias(rel_bias, T, wbuf)
        att, k_new, v_new = _att_sample(q, k, v, k_buf.reshape(B, wbuf, W_B), v_buf.reshape(B, wbuf, W_B),
                                        bias_c, bias_n, B, T, wbuf)
        k_new = k_new.reshape(B, wbuf, H_B, HD_B)
        v_new = v_new.reshape(B, wbuf, H_B, HD_B)

    ha, C, nn, m = _mlstm(mqk, mv, gcol, grow, wts["conv_w"], wts["conv_b"], conv_buf, wts["b_gate"], C0, n0, m0, B, T)
    mqk3 = mqk.reshape(B, T, 2 * W_A)
    if T >= CONV_W - 1:
        conv_new = mqk3[:, T - (CONV_W - 1):]
    else:
        conv_new = jnp.concatenate([conv_buf, mqk3], axis=1)[:, -(CONV_W - 1):]

    y1, h2 = _out_proj(ha, mo, att, x2, expand(gate1), expand(scale2), expand(shift2), per_token, T, tm,
                       wts["mg"], wts["ag"], wts["g2"], wts["bd"], wts["wo"])

    eidx, gate = _peer_route(h2, wts["wq"], wts["keys"], tb)
    w = _peer_act(eidx, h2, gate, wts["u_tab"], tb)
    y = _peer_mix(eidx, w, y1, expand(gate2), per_token, T, wts["v_tab"], tb)
    return (y.reshape(B, T, D_MODEL), k_new, v_new, conv_new, C, nn.reshape(B, H_A, DK_A), m[:, :, 0, 0])


def _prep_weights(l, norm1_g, norm2_g, w_in, b_gate, conv_w, conv_b, q_norm_g, k_norm_g, att_out_g, mlstm_out_g,
                  w_out, peer_wq, peer_keys, peer_u, peer_v):
    w = w_in[l]
    wg = w[:, W_MAIN:]
    grp = np.arange(W_B) // HD_B
    bd = jnp.asarray((grp[:, None] == grp[None, :]).astype(np.float32) / HD_B, BF16)
    return dict(
        g1=norm1_g[l].reshape(1, -1), g2=norm2_g[l].reshape(1, -1),
        wm=w[:, :W_MAIN].astype(BF16),
        wgc=jnp.pad(wg, ((0, 0), (0, LANES - 2 * H_A))), wgr=wg.T,
        bd=bd, qg=jnp.tile(q_norm_g[l], H_B).reshape(1, -1), kg=jnp.tile(k_norm_g[l], H_B).reshape(1, -1),
        conv_w=conv_w[l], conv_b=conv_b[l], b_gate=b_gate[l],
        mg=mlstm_out_g[l].reshape(1, -1), ag=att_out_g[l].reshape(1, -1),
        wo=w_out[l].astype(BF16), wq=peer_wq[l].astype(BF16), keys=peer_keys[l],
        u_tab=_pack_table(peer_u[l]), v_tab=_pack_table(peer_v[l]),
    )


def kernel(x_prompt, x_sample, cache_k, cache_v, state_conv, state_C, state_n, state_m, c_prompt, c_sample,
           rel_bias, w_ada, b_ada, norm1_g, norm2_g, w_in, b_gate, conv_w, conv_b, q_norm_g, k_norm_g,
           att_out_g, mlstm_out_g, w_out, peer_wq, peer_keys, peer_u, peer_v):
    depth = w_ada.shape[0]
    bp = x_prompt.shape[0]
    bs = x_sample.shape[0]
    yp, ys = x_prompt, x_sample
    sp, ss = [], []
    for l in range(depth):
        wts = _prep_weights(l, norm1_g, norm2_g, w_in, b_gate, conv_w, conv_b, q_norm_g, k_norm_g, att_out_g,
                            mlstm_out_g, w_out, peer_wq, peer_keys, peer_u, peer_v)
        mod = _ada(jnp.concatenate([c_prompt, c_sample], axis=0), w_ada[l], b_ada[l])
        zc = jnp.zeros((bp, CONV_W - 1, 2 * W_A), F32)
        zC = jnp.zeros((bp, H_A, DK_A, DK_A), F32)
        zn = jnp.zeros((bp, H_A, DK_A), F32)
        zm = jnp.zeros((bp, H_A), F32)
        outp = _layer(yp, mod[:bp], zc, zC, zn, zm, None, None, rel_bias, wts, tm=256, tb=128)
        outs = _layer(ys, mod[bp:], state_conv[l], state_C[l], state_n[l], state_m[l], cache_k[l], cache_v[l],
                      rel_bias, wts, tm=256, tb=128)
        yp, ys = outp[0], outs[0]
        sp.append(outp[1:])
        ss.append(outs[1:])
    k_p, v_p, conv_p, C_p, n_p, m_p = [jnp.stack([s[i] for s in sp], axis=0) for i in range(6)]
    k_s, v_s, conv_s, C_s, n_s, m_s = [jnp.stack([s[i] for s in ss], axis=0) for i in range(6)]
    return (yp, ys, k_p, v_p, conv_p, C_p, n_p, m_p, k_s, v_s, conv_s, C_s, n_s, m_s)
```

```python
import functools
import math

import numpy as np
import jax
import jax.numpy as jnp
from jax import lax
from jax.experimental import pallas as pl
from jax.experimental.pallas import tpu as pltpu

F32 = jnp.float32
BF16 = jnp.bfloat16
HIGHEST = lax.Precision.HIGHEST

D_MODEL = 1024
H_A = 4
DK_A = 128
W_A = H_A * DK_A
H_B = 8
HD_B = 64
W_B = H_B * HD_B
CONV_W = 4
MLSTM_CHUNK = 64
MLSTM_HP = 4
DIL_PATTERNS = ((128, 1), (512, 4), (2048, 16))
ATT_STEPS = 128
ATT_UNROLL = 4
N_BUCKETS = 32
MAX_DIST = 2048
PEER_HEADS = 8
N_SUBKEYS = 128
PEER_TOPK = 16
PEER_PAIRS = PEER_HEADS * PEER_TOPK
EPS = 1e-6
NEG = -1e30
W_MAIN = 3 * W_B + 4 * W_A
LANES = 128
VMEM_LIMIT = 56 * 1024 * 1024


def _cparams(sem):
    return pltpu.CompilerParams(dimension_semantics=sem, vmem_limit_bytes=VMEM_LIMIT)


def _nt_dot(a, b, precision=None):
    return lax.dot_general(a, b, (((1,), (1,)), ((), ())), preferred_element_type=F32, precision=precision)


def _t5_bucket(dist):
    max_exact = N_BUCKETS // 2
    d = np.maximum(dist, 1).astype(np.float32)
    large = max_exact + (np.log(d / max_exact) / math.log(MAX_DIST / max_exact) * (N_BUCKETS - max_exact)).astype(np.int32)
    large = np.minimum(large, N_BUCKETS - 1)
    return np.where(dist < max_exact, dist, large).astype(np.int32)


def _ada_kernel(c_ref, w_ref, b_ref, o_ref):
    c = c_ref[...]
    s = c * jax.nn.sigmoid(c)
    o_ref[...] = jnp.dot(s, w_ref[...], preferred_element_type=F32, precision=HIGHEST) + b_ref[...]


def _ada(c_all, w_ada, b_ada):
    n = c_all.shape[0]
    return pl.pallas_call(
        _ada_kernel,
        grid=(6,),
        in_specs=[pl.BlockSpec((n, D_MODEL), lambda j: (0, 0)),
                  pl.BlockSpec((D_MODEL, D_MODEL), lambda j: (0, j)),
                  pl.BlockSpec((1, D_MODEL), lambda j: (0, j))],
        out_specs=pl.BlockSpec((n, D_MODEL), lambda j: (0, j)),
        out_shape=jax.ShapeDtypeStruct((n, 6 * D_MODEL), F32),
        compiler_params=_cparams(("arbitrary",)),
        name="ada",
    )(c_all, w_ada, b_ada.reshape(1, -1))


def _group_mean_sq(a, bd):
    sq = a * a
    hi = sq.astype(BF16)
    lo = (sq - hi.astype(F32)).astype(BF16)
    return jnp.dot(hi, bd, preferred_element_type=F32) + jnp.dot(lo, bd, preferred_element_type=F32)


def _in_kernel(x_ref, sc_ref, sh_ref, g1_ref, wm_ref, wgc_ref, wgr_ref, bd_ref, qg_ref, kg_ref,
               q_ref, k_ref, v_ref, mqk_ref, mv_ref, mo_ref, gcol_ref, grow_ref):
    x = x_ref[...]
    ms = jnp.mean(x * x, axis=-1, keepdims=True)
    h = x * lax.rsqrt(ms + EPS) * g1_ref[...]
    h = h * (1.0 + sc_ref[...]) + sh_ref[...]
    y = jnp.dot(h.astype(BF16), wm_ref[...], preferred_element_type=F32)
    bd = bd_ref[...]
    aq = y[:, 0:W_B]
    ak = y[:, W_B:2 * W_B]
    q_ref[...] = aq * lax.rsqrt(_group_mean_sq(aq, bd) + EPS) * qg_ref[...] * (HD_B ** -0.5)
    k_ref[...] = ak * lax.rsqrt(_group_mean_sq(ak, bd) + EPS) * kg_ref[...]
    v_ref[...] = y[:, 2 * W_B:3 * W_B]
    o = 3 * W_B
    mqk_ref[...] = y[:, o:o + 2 * W_A]
    mv_ref[...] = y[:, o + 2 * W_A:o + 3 * W_A]
    mo_ref[...] = y[:, o + 3 * W_A:o + 4 * W_A]
    gcol_ref[...] = jnp.dot(h, wgc_ref[...], preferred_element_type=F32, precision=HIGHEST)
    grow_ref[...] = _nt_dot(wgr_ref[...], h, precision=HIGHEST)


def _in_proj(x2, scale, shift, per_token_mod, toks_per_seq, tm, g1, wm, wgc, wgr, bd, qg, kg):
    n = x2.shape[0]
    nt = n // tm
    if per_token_mod:
        mod_spec = pl.BlockSpec((tm, D_MODEL), lambda i: (i, 0))
    else:
        tiles_per_seq = toks_per_seq // tm
        mod_spec = pl.BlockSpec((None, 1, D_MODEL), lambda i: (i // tiles_per_seq, 0, 0))
    const = lambda shape: pl.BlockSpec(shape, lambda i: (0,) * len(shape))
    tok = lambda w: pl.BlockSpec((tm, w), lambda i: (i, 0))
    outs = pl.pallas_call(
        _in_kernel,
        grid=(nt,),
        in_specs=[tok(D_MODEL), mod_spec, mod_spec, const((1, D_MODEL)), const((D_MODEL, W_MAIN)),
                  const((D_MODEL, LANES)), const((8, D_MODEL)), const((W_B, W_B)), const((1, W_B)), const((1, W_B))],
        out_specs=[tok(W_B), tok(W_B), tok(W_B), tok(2 * W_A), tok(W_A), tok(W_A), tok(LANES),
                   pl.BlockSpec((8, tm), lambda i: (0, i))],
        out_shape=[jax.ShapeDtypeStruct((n, W_B), F32)] * 3 + [jax.ShapeDtypeStruct((n, 2 * W_A), F32)]
                  + [jax.ShapeDtypeStruct((n, W_A), F32)] * 2 + [jax.ShapeDtypeStruct((n, LANES), F32),
                                                                 jax.ShapeDtypeStruct((8, n), F32)],
        compiler_params=_cparams(("arbitrary",)),
        name="in_proj",
    )(x2, scale, shift, g1, wm, wgc, wgr, bd, qg, kg)
    return outs


def _log_sigmoid(x):
    return jnp.minimum(x, 0.0) - jnp.log1p(jnp.exp(-jnp.abs(x)))


def _mlstm_kernel(bg_ref, m0_ref, mq_ref, mk_ref, cwq_ref, cwk_ref, cbq_ref, cbk_ref, bufq_ref, bufk_ref,
                  v_ref, gcol_ref, grow_ref, c0_ref, n0_ref,
                  h_ref, c_ref, n_ref, m_ref, sq, sk, *, T, L, rows_whole):
    b = pl.program_id(0)
    hg = pl.program_id(1)
    nc = T // L

    def conv(u_ref, buf_ref, w_ref, cb_ref, s_ref, out_ref):
        s_ref[0:8, :] = jnp.zeros((8, MLSTM_HP * LANES), F32)
        s_ref[5:8, :] = buf_ref[...]
        s_ref[8:8 + T, :] = u_ref[...]
        y = cb_ref[...]
        for j in range(CONV_W):
            y = y + s_ref[5 + j:5 + j + T, :] * w_ref[j:j + 1, :]
        out_ref[...] = y * jax.nn.sigmoid(y)

    conv(mq_ref, bufq_ref, cwq_ref, cbq_ref, sq, h_ref)
    conv(mk_ref, bufk_ref, cwk_ref, cbk_ref, sk, sq.at[8:8 + T, :])

    lane = lax.broadcasted_iota(jnp.int32, (L, LANES), 1)
    ri = lax.broadcasted_iota(jnp.int32, (L, L), 0)
    ci = lax.broadcasted_iota(jnp.int32, (L, L), 1)
    causal = ri >= ci
    tri = causal.astype(F32)
    tri_t = (ri <= ci).astype(F32)

    def chunk(c, carry):
        r0 = pl.multiple_of(c * L, L)
        rc = c + (b * nc if rows_whole else 0)
        g = gcol_ref[pl.ds(r0, L), :]
        st = []
        for j in range(MLSTM_HP):
            C, n, m = carry[j]
            hd = hg * MLSTM_HP + j
            ig_b = bg_ref[hd]
            f_b = bg_ref[H_A + hd]
            cs = slice(j * LANES, (j + 1) * LANES)
            qc = h_ref[pl.ds(r0, L), cs]
            kc = sq[pl.ds(r0 + 8, L), cs] * (DK_A ** -0.5)
            vb = v_ref[pl.ds(r0, L), cs].astype(BF16)
            ig_col = jnp.sum(jnp.where(lane == hd, g, 0.0), axis=1, keepdims=True) + ig_b
            f_col = jnp.sum(jnp.where(lane == hd + H_A, g, 0.0), axis=1, keepdims=True) + f_b
            ig_row = grow_ref[hd, pl.ds(rc, 1), :] + ig_b
            lf_row = _log_sigmoid(grow_ref[hd + H_A, pl.ds(rc, 1), :] + f_b)
            b_col = jnp.dot(tri, jnp.broadcast_to(_log_sigmoid(f_col), (L, LANES)), preferred_element_type=F32,
                            precision=HIGHEST)[:, 0:1]
            b_row = jnp.dot(jnp.broadcast_to(lf_row, (8, L)), tri_t, preferred_element_type=F32,
                            precision=HIGHEST)[0:1, :]
            qb = qc.astype(BF16)
            qk = _nt_dot(qb, kc.astype(BF16))
            qC = jnp.dot(qb, C.astype(BF16), preferred_element_type=F32)
            st.append(dict(C=C, n=n, m=m, cs=cs, qc=qc, kc=kc, vb=vb, ig_col=ig_col, ig_row=ig_row,
                           b_col=b_col, b_row=b_row, qk=qk, qC=qC))
        for d in st:
            b_col, b_row, m = d["b_col"], d["b_row"], d["m"]
            g_col = b_col + m
            dm = jnp.where(causal, b_col - b_row + d["ig_row"], -jnp.inf)
            mt = jnp.maximum(g_col, jnp.max(dm, axis=1, keepdims=True))
            s = d["qk"] * jnp.exp(dm - mt)
            wg = jnp.exp(g_col - mt)
            bl = b_col[L - 1:L, :]
            m_new = jnp.maximum(bl + m, jnp.max(bl - b_row + d["ig_row"], axis=1, keepdims=True))
            kw = jnp.exp(bl - b_col + d["ig_col"] - m_new) * d["kc"]
            d.update(mt=mt, s=s, wg=wg, m_new=m_new, kw=kw, wc=jnp.exp(bl + m - m_new))
        out = []
        for d in st:
            s, wg = d["s"], d["wg"]
            num = wg * d["qC"] + jnp.dot(s.astype(BF16), d["vb"], preferred_element_type=F32)
            den = wg * jnp.sum(d["qc"] * d["n"], axis=1, keepdims=True) + jnp.sum(s, axis=1, keepdims=True)
            h_ref[pl.ds(r0, L), d["cs"]] = num / jnp.maximum(jnp.abs(den), jnp.exp(-d["mt"]))
            upd = lax.dot_general(d["kw"].astype(BF16), d["vb"], (((0,), (0,)), ((), ())), preferred_element_type=F32)
            out.append((d["wc"] * d["C"] + upd, d["wc"] * d["n"] + jnp.sum(d["kw"], axis=0, keepdims=True), d["m_new"]))
        return tuple(out)

    init = tuple((c0_ref[j], n0_ref[j], jnp.full((1, 1), m0_ref[b, hg * MLSTM_HP + j], F32))
                 for j in range(MLSTM_HP))
    final = lax.fori_loop(0, nc, chunk, init)
    for j in range(MLSTM_HP):
        c_ref[j] = final[j][0]
        n_ref[j] = final[j][1]
        m_ref[j] = jnp.broadcast_to(final[j][2], (1, LANES))


def _mlstm(mqk, mv, gcol, grow, conv_w, conv_b, conv_buf, b_gate, C0, n0, m0, B, T):
    L = math.gcd(T, MLSTM_CHUNK)
    nc = T // L
    n = B * T
    rows_whole = nc % 8 != 0
    grow3 = grow.reshape(8, n // L, L)
    if rows_whole:
        grow_spec = pl.BlockSpec((8, n // L, L), lambda b, h: (0, 0, 0))
    else:
        grow_spec = pl.BlockSpec((8, nc, L), lambda b, h: (0, b, 0))
    smem = pl.BlockSpec(memory_space=pltpu.SMEM)
    wide = MLSTM_HP * LANES
    ng = H_A // MLSTM_HP
    colq = lambda rows: pl.BlockSpec((rows, wide), lambda b, h: (0, h))
    colk = lambda rows: pl.BlockSpec((rows, wide), lambda b, h: (0, h + ng))
    st = lambda r, c: pl.BlockSpec((None, MLSTM_HP, r, c), lambda b, h: (b, h, 0, 0))
    kern = functools.partial(_mlstm_kernel, T=T, L=L, rows_whole=rows_whole)
    return pl.pallas_call(
        kern,
        grid=(B, ng),
        in_specs=[smem, smem,
                  pl.BlockSpec((T, wide), lambda b, h: (b, h)), pl.BlockSpec((T, wide), lambda b, h: (b, h + ng)),
                  colq(CONV_W), colk(CONV_W), colq(1), colk(1),
                  pl.BlockSpec((None, CONV_W - 1, wide), lambda b, h: (b, 0, h)),
                  pl.BlockSpec((None, CONV_W - 1, wide), lambda b, h: (b, 0, h + ng)),
                  pl.BlockSpec((T, wide), lambda b, h: (b, h)),
                  pl.BlockSpec((T, LANES), lambda b, h: (b, 0)),
                  grow_spec, st(DK_A, DK_A), st(1, DK_A)],
        out_specs=[pl.BlockSpec((T, wide), lambda b, h: (b, h)), st(DK_A, DK_A), st(1, DK_A), st(1, LANES)],
        out_shape=[jax.ShapeDtypeStruct((n, W_A), F32), jax.ShapeDtypeStruct((B, H_A, DK_A, DK_A), F32),
                   jax.ShapeDtypeStruct((B, H_A, 1, DK_A), F32), jax.ShapeDtypeStruct((B, H_A, 1, LANES), F32)],
        scratch_shapes=[pltpu.VMEM((T + 8, wide), F32), pltpu.VMEM((T + 8, wide), F32)],
        compiler_params=_cparams(("arbitrary", "arbitrary")),
        name="mlstm",
    )(b_gate, m0, mqk, mqk, conv_w, conv_w, conv_b.reshape(1, -1), conv_b.reshape(1, -1), conv_buf, conv_buf,
      mv, gcol, grow3, C0, n0.reshape(B, H_A, 1, DK_A))


def _att_prompt_kernel(q_ref, k_ref, v_ref, bias_ref, o_ref, num_s, mx_s, dn_s, mxs_s, *, T):
    lane = lax.broadcasted_iota(jnp.int32, (ATT_STEPS, LANES), 1)
    first = lane < HD_B
    nblk = T // ATT_STEPS

    for br, (_, dil) in enumerate(DIL_PATTERNS):
        span = ATT_STEPS * dil
        has_prev = T > span

        def rows(start, dil=dil):
            return pl.ds(start, ATT_STEPS, stride=dil) if dil > 1 else pl.ds(start, ATT_STEPS)

        def blk(i0, carry, br=br, dil=dil, span=span, has_prev=has_prev, rows=rows):
            units = []
            for u in range(ATT_UNROLL):
                i = i0 + u * (nblk // ATT_UNROLL)
                nb = i // dil
                start = nb * span + i % dil
                qb = q_ref[rows(start), :]
                kcur = k_ref[rows(start), :].astype(BF16)
                kprev = k_ref[rows(jnp.maximum(start - span, 0)), :].astype(BF16) if has_prev else None
                for hh in range(2):
                    own = first if hh == 0 else ~first
                    qh = jnp.where(own, qb, 0.0).astype(BF16)
                    lc = _nt_dot(qh, kcur) + bias_ref[br, hh, :, ATT_STEPS:2 * ATT_STEPS]
                    lp = None
                    if has_prev:
                        lp = _nt_dot(qh, kprev) + bias_ref[br, hh, :, 0:ATT_STEPS]
                        lp = jnp.where(nb > 0, lp, NEG)
                    units.append((start, own, lc, lp))
            weights = []
            for start, own, lc, lp in units:
                mx = jnp.max(jnp.maximum(lc, lp) if has_prev else lc, axis=1, keepdims=True)
                pc = jnp.exp(lc - mx).astype(BF16)
                pp = jnp.exp(lp - mx).astype(BF16) if has_prev else None
                weights.append((mx, pc, pp))
            res = []
            for (start, own, _, _), (mx, pc, pp) in zip(units, weights):
                acc = jnp.dot(pc, jnp.where(own, v_ref[rows(start), :], 1.0).astype(BF16), preferred_element_type=F32)
                if has_prev:
                    vprev = v_ref[rows(jnp.maximum(start - span, 0)), :]
                    acc = acc + jnp.dot(pp, jnp.where(own, vprev, 1.0).astype(BF16), preferred_element_type=F32)
                res.append((acc, mx))
            for u in range(ATT_UNROLL):
                start = units[2 * u][0]
                (a0, m0), (a1, m1) = res[2 * u], res[2 * u + 1]
                num_s[br, rows(start), :] = jnp.where(first, a0, a1)
                dn_s[br, rows(start), :] = jnp.where(first, a1, a0)
                mx_s[br, rows(start), :] = jnp.where(first, m0, m1)
                mxs_s[br, rows(start), :] = jnp.where(first, m1, m0)
            return carry

        lax.fori_loop(0, nblk // ATT_UNROLL, blk, 0)

    def comb(i, carry):
        rs = pl.ds(pl.multiple_of(i * ATT_STEPS, ATT_STEPS), ATT_STEPS)
        m = jnp.maximum(jnp.maximum(mx_s[0, rs, :], mx_s[1, rs, :]), mx_s[2, rs, :])
        ms = jnp.maximum(jnp.maximum(mxs_s[0, rs, :], mxs_s[1, rs, :]), mxs_s[2, rs, :])
        num = jnp.zeros((ATT_STEPS, LANES), F32)
        den = jnp.zeros((ATT_STEPS, LANES), F32)
        for br in range(3):
            num = num + jnp.exp(mx_s[br, rs, :] - m) * num_s[br, rs, :]
            den = den + jnp.exp(mxs_s[br, rs, :] - ms) * dn_s[br, rs, :]
        o_ref[rs, :] = num / pltpu.roll(den, HD_B, axis=1)
        return carry

    lax.fori_loop(0, nblk, comb, 0)


def _att_prompt(q, k, v, bias, B, T):
    n = B * T
    blk = pl.BlockSpec((T, LANES), lambda b, p: (b, p))
    kern = functools.partial(_att_prompt_kernel, T=T)
    return pl.pallas_call(
        kern,
        grid=(B, H_B // 2),
        in_specs=[blk, blk, blk, pl.BlockSpec((3, 2, ATT_STEPS, 2 * ATT_STEPS), lambda b, p: (0, p, 0, 0))],
        out_specs=blk,
        out_shape=jax.ShapeDtypeStruct((n, W_B), F32),
        scratch_shapes=[pltpu.VMEM((3, T, LANES), F32)] * 4,
        compiler_params=_cparams(("arbitrary", "arbitrary")),
        name="att_prompt",
    )(q, k, v, bias)


def _prompt_bias(rel_bias):
    tab = _distance_bias(rel_bias, ATT_STEPS * DIL_PATTERNS[-1][1] + 1)
    pad = jnp.full((H_B, ATT_STEPS - 1), NEG, F32)
    tabs = []
    for _, dil in DIL_PATTERNS:
        w = jnp.concatenate([pad, tab[:, 0:ATT_STEPS * dil + 1:dil], pad], axis=1)[:, ::-1]
        tabs.append(jnp.stack([w[:, ATT_STEPS - 1 - i:3 * ATT_STEPS - 1 - i] for i in range(ATT_STEPS)], axis=1))
    return jnp.stack(tabs, axis=0)


def _distance_bias(rel_bias, n):
    onehot = np.zeros((n, N_BUCKETS), np.float32)
    onehot[np.arange(n), _t5_bucket(np.arange(n))] = 1.0
    return jnp.dot(jnp.asarray(onehot), rel_bias.astype(F32), precision=HIGHEST).T


def _att_sample_kernel(q_ref, kn_ref, vn_ref, ck_ref, cv_ref, bc_ref, bn_ref, o_ref, ko_ref, vo_ref, *, S, WBUF):
    lane = lax.broadcasted_iota(jnp.int32, (S, LANES), 1)
    first = lane < HD_B
    for hp in range(H_B // 2):
        cs = slice(hp * LANES, (hp + 1) * LANES)
        kc = ck_ref[:, cs].astype(BF16)
        vc = cv_ref[:, cs].astype(BF16)
        kn = kn_ref[:, cs].astype(BF16)
        vn = vn_ref[:, cs].astype(BF16)
        qp = q_ref[:, cs]
        res = []
        for hh in range(2):
            h = 2 * hp + hh
            qh = jnp.where(first if hh == 0 else ~first, qp, 0.0).astype(BF16)
            lc = _nt_dot(qh, kc) + bc_ref[h]
            ln = _nt_dot(qh, kn) + bn_ref[h]
            mx = jnp.maximum(jnp.max(lc, axis=1, keepdims=True), jnp.max(ln, axis=1, keepdims=True))
            pc = jnp.exp(lc - mx)
            pn = jnp.exp(ln - mx)
            den = jnp.sum(pc, axis=1, keepdims=True) + jnp.sum(pn, axis=1, keepdims=True)
            num = jnp.dot(pc.astype(BF16), vc, preferred_element_type=F32) \
                + jnp.dot(pn.astype(BF16), vn, preferred_element_type=F32)
            res.append(num / den)
        o_ref[:, cs] = jnp.where(first, res[0], res[1])
    ko_ref[0:WBUF - S, :] = ck_ref[S:WBUF, :]
    ko_ref[WBUF - S:WBUF, :] = kn_ref[...]
    vo_ref[0:WBUF - S, :] = cv_ref[S:WBUF, :]
    vo_ref[WBUF - S:WBUF, :] = vn_ref[...]


def _att_sample(q, kn, vn, ck, cv, bias_c, bias_n, B, S, WBUF):
    tokb = pl.BlockSpec((S, W_B), lambda b: (b, 0))
    cache = pl.BlockSpec((None, WBUF, W_B), lambda b: (b, 0, 0))
    kern = functools.partial(_att_sample_kernel, S=S, WBUF=WBUF)
    return pl.pallas_call(
        kern,
        grid=(B,),
        in_specs=[tokb, tokb, tokb, cache, cache,
                  pl.BlockSpec((H_B, S, WBUF), lambda b: (0, 0, 0)), pl.BlockSpec((H_B, S, S), lambda b: (0, 0, 0))],
        out_specs=[tokb, cache, cache],
        out_shape=[jax.ShapeDtypeStruct((B * S, W_B), F32), jax.ShapeDtypeStruct((B, WBUF, W_B), F32),
                   jax.ShapeDtypeStruct((B, WBUF, W_B), F32)],
        compiler_params=_cparams(("arbitrary",)),
        name="att_sample",
    )(q, kn, vn, ck, cv, bias_c, bias_n)


def _sample_bias(rel_bias, S, WBUF):
    dist = np.arange(WBUF + S)
    mult = np.zeros(WBUF + S, np.int64)
    for w, dil in DIL_PATTERNS:
        mult += ((dist % dil == 0) & (dist <= w)).astype(np.int64)
    logm = np.where(mult > 0, np.log(np.maximum(mult, 1)), 0.0).astype(np.float32)
    tab = _distance_bias(rel_bias, WBUF + S) + logm[None, :]
    tab = jnp.where((mult > 0)[None, :], tab, NEG)
    rev = jnp.concatenate([tab[:, ::-1], jnp.full((H_B, S), NEG, F32)], axis=1)
    last = WBUF + S - 1
    bias_c = jnp.stack([rev[:, S - 1 - s:S - 1 - s + WBUF] for s in range(S)], axis=1)
    bias_n = jnp.stack([rev[:, last - s:last - s + S] for s in range(S)], axis=1)
    return bias_c, bias_n


def _out_kernel(ha_ref, mo_ref, att_ref, x_ref, g1_ref, sc_ref, sh_ref, mg_ref, ag_ref, g2_ref, bd_ref, wo_ref,
                y_ref, h2_ref):
    ha = ha_ref[...]
    parts = []
    for hd in range(H_A):
        a = ha[:, hd * DK_A:(hd + 1) * DK_A]
        parts.append(a * lax.rsqrt(jnp.mean(a * a, axis=-1, keepdims=True) + EPS))
    hn = jnp.concatenate(parts, axis=1) * mg_ref[...] * jax.nn.sigmoid(mo_ref[...])
    att = att_ref[...]
    an = att * lax.rsqrt(_group_mean_sq(att, bd_ref[...]) + EPS) * ag_ref[...]
    mix = jnp.dot(hn.astype(BF16), wo_ref[0:W_A, :], preferred_element_type=F32) \
        + jnp.dot(an.astype(BF16), wo_ref[W_A:W_A + W_B, :], preferred_element_type=F32)
    y = x_ref[...] + g1_ref[...] * mix
    y_ref[...] = y
    h2 = y * lax.rsqrt(jnp.mean(y * y, axis=-1, keepdims=True) + EPS) * g2_ref[...]
    h2_ref[...] = h2 * (1.0 + sc_ref[...]) + sh_ref[...]


def _out_proj(ha, mo, att, x2, gate1, scale2, shift2, per_token_mod, toks_per_seq, tm, mg, ag, g2, bd, wo):
    n = x2.shape[0]
    if per_token_mod:
        mod_spec = pl.BlockSpec((tm, D_MODEL), lambda i: (i, 0))
    else:
        tiles_per_seq = toks_per_seq // tm
        mod_spec = pl.BlockSpec((None, 1, D_MODEL), lambda i: (i // tiles_per_seq, 0, 0))
    const = lambda shape: pl.BlockSpec(shape, lambda i: (0,) * len(shape))
    tok = lambda w: pl.BlockSpec((tm, w), lambda i: (i, 0))
    return pl.pallas_call(
        _out_kernel,
        grid=(n // tm,),
        in_specs=[tok(W_A), tok(W_A), tok(W_B), tok(D_MODEL), mod_spec, mod_spec, mod_spec,
                  const((1, W_A)), const((1, W_B)), const((1, D_MODEL)), const((W_B, W_B)),
                  const((W_A + W_B, D_MODEL))],
        out_specs=[tok(D_MODEL), tok(D_MODEL)],
        out_shape=[jax.ShapeDtypeStruct((n, D_MODEL), F32)] * 2,
        compiler_params=_cparams(("arbitrary",)),
        name="out_proj",
    )(ha, mo, att, x2, gate1, scale2, shift2, mg, ag, g2, bd, wo)


def _top16(s, nrows):
    iota = lax.broadcasted_iota(jnp.int32, s.shape, 0).astype(F32)
    vals, idxs = [], []
    for _ in range(PEER_TOPK):
        m = jnp.max(s, axis=0, keepdims=True)
        pos = jnp.min(jnp.where(s == m, iota, float(nrows)), axis=0, keepdims=True)
        vals.append(m)
        idxs.append(pos)
        s = jnp.where(iota == pos, -jnp.inf, s)
    return jnp.concatenate(vals, axis=0), jnp.concatenate(idxs, axis=0)


def _peer_route_kernel(h2_ref, wq_ref, keys_ref, eidx_ref, gate_ref, sv_s, si_s, gt_s, et_s, *, tm):
    qh = jnp.dot(h2_ref[...].astype(BF16), wq_ref[...], preferred_element_type=F32)
    for hp in range(2 * PEER_HEADS):
        qs = qh[:, hp * N_SUBKEYS:(hp + 1) * N_SUBKEYS]
        s = _nt_dot(keys_ref[hp % 2], qs, precision=HIGHEST)
        v, i = _top16(s, N_SUBKEYS)
        sv_s[hp] = v
        si_s[hp] = i

    k = PEER_TOPK
    half = k // 2
    sub = lax.broadcasted_iota(jnp.int32, (half, tm), 0).astype(F32)
    pos = jnp.concatenate(
        [lax.broadcasted_iota(jnp.int32, (k, tm), 0).astype(F32)]
        + [float(a * k) + sub for a in range(1, half)] + [(sub + float(half)) * float(k)], axis=0)
    n_cand = float(k * k)

    def head(h, carry):
        sv0 = sv_s[2 * h]
        sv1 = sv_s[2 * h + 1]
        si0 = si_s[2 * h] * float(N_SUBKEYS)
        si1 = si_s[2 * h + 1]
        cands = [sv0[0:1, :] + sv1]
        cidxs = [si0[0:1, :] + si1]
        for a in range(1, half):
            c = sv0[a:a + 1, :] + sv1[0:half, :]
            nb = k // (a + 1)
            cands.append(c if nb >= half else jnp.where(sub < float(nb), c, -jnp.inf))
            cidxs.append(si0[a:a + 1, :] + si1[0:half, :])
        cands.append(sv0[half:k, :] + sv1[0:1, :])
        cidxs.append(si0[half:k, :] + si1[0:1, :])
        cand = jnp.concatenate(cands, axis=0)
        cidx = jnp.concatenate(cidxs, axis=0)
        fv, ev = [], []
        for _ in range(k):
            m = jnp.max(cand, axis=0, keepdims=True)
            first = jnp.min(jnp.where(cand == m, pos, n_cand), axis=0, keepdims=True)
            sel = pos == first
            ev.append(jnp.max(jnp.where(sel, cidx, -1.0), axis=0, keepdims=True))
            fv.append(m)
            cand = jnp.where(sel, -jnp.inf, cand)
        fvs = jnp.concatenate(fv, axis=0)
        e = jnp.exp(fvs - fv[0])
        rs = pl.ds(pl.multiple_of(h * k, k), k)
        gt_s[rs, :] = e / jnp.sum(e, axis=0, keepdims=True)
        et_s[rs, :] = jnp.concatenate(ev, axis=0)
        return carry

    lax.fori_loop(0, PEER_HEADS, head, 0)
    gate_ref[...] = gt_s[...]
    for j in range(tm // LANES):
        cs = slice(j * LANES, (j + 1) * LANES)
        eidx_ref[cs, :] = et_s[:, cs].T.astype(jnp.int32)


def _peer_route(h2, wq, keys, tm):
    n = h2.shape[0]
    kern = functools.partial(_peer_route_kernel, tm=tm)
    return pl.pallas_call(
        kern,
        grid=(n // tm,),
        in_specs=[pl.BlockSpec((tm, D_MODEL), lambda i: (i, 0)),
                  pl.BlockSpec((D_MODEL, 2 * PEER_HEADS * N_SUBKEYS), lambda i: (0, 0)),
                  pl.BlockSpec((2, N_SUBKEYS, N_SUBKEYS), lambda i: (0, 0, 0))],
        out_specs=[pl.BlockSpec((tm, PEER_PAIRS), lambda i: (i, 0)), pl.BlockSpec((PEER_PAIRS, tm), lambda i: (0, i))],
        out_shape=[jax.ShapeDtypeStruct((n, PEER_PAIRS), jnp.int32), jax.ShapeDtypeStruct((PEER_PAIRS, n), F32)],
        scratch_shapes=[pltpu.VMEM((2 * PEER_HEADS, PEER_TOPK, tm), F32)] * 2 + [pltpu.VMEM((PEER_PAIRS, tm), F32)] * 2,
        compiler_params=_cparams(("arbitrary",)),
        name="peer_route",
    )(h2, wq, keys)


ROW_CHUNKS = D_MODEL // LANES


def _pack_table(tab):
    return tab.astype(BF16).reshape(tab.shape[0], ROW_CHUNKS, LANES)


def _gather_rows(idx_ref, t, tab_ref, g_s):
    for p in range(PEER_PAIRS):
        g_s[ROW_CHUNKS * p:ROW_CHUNKS * (p + 1), :] = tab_ref[idx_ref[t, p]]


def _pair_rows(g_s, j):
    both = g_s[2 * ROW_CHUNKS * j:2 * ROW_CHUNKS * (j + 1), :].astype(F32)
    return both[0:ROW_CHUNKS, :], both[ROW_CHUNKS:2 * ROW_CHUNKS, :]


def _token_loop(tb, stage, compute):
    stage(0, 0)

    def step(i, carry):
        t0 = 2 * i
        stage(t0 + 1, 1)
        compute(t0, 0)
        stage(jnp.minimum(t0 + 2, tb - 1), 0)
        compute(t0 + 1, 1)
        return carry

    lax.fori_loop(0, tb // 2, step, 0)


def _fold_rows(a, sub):
    t = [x + pltpu.roll(x, 4, axis=0) for x in a]
    b = [jnp.where(sub < 4, t[j], t[j + 4]) for j in range(4)]
    c = [jnp.where(sub % 4 < 2, b[j] + pltpu.roll(b[j], 6, axis=0), b[j + 2] + pltpu.roll(b[j + 2], 2, axis=0))
         for j in range(2)]
    return jnp.where(sub % 2 == 0, c[0] + pltpu.roll(c[0], 7, axis=0), c[1] + pltpu.roll(c[1], 1, axis=0))


def _peer_act_kernel(idx_ref, x_ref, gate_ref, tab_ref, w_ref, g0_s, g1_s, m_s, a_s, *, tb):
    sub = lax.broadcasted_iota(jnp.int32, (ROW_CHUNKS, LANES), 0)
    lane = lax.broadcasted_iota(jnp.int32, (PEER_PAIRS, tb), 1)
    g_s = (g0_s, g1_s)
    a_s[...] = jnp.zeros((PEER_PAIRS, tb), F32)

    def compute(t, slot):
        row = x_ref[pl.ds(t, 1), :]
        x8 = jnp.concatenate([row[:, r * LANES:(r + 1) * LANES] for r in range(ROW_CHUNKS)], axis=0)
        for blk in range(PEER_PAIRS // 8):
            prods = []
            for j in range(4):
                u0, u1 = _pair_rows(g_s[slot], 4 * blk + j)
                prods += [u0 * x8, u1 * x8]
            m_s[8 * blk:8 * (blk + 1), :] = _fold_rows(prods, sub)
        dots = jnp.sum(m_s[...], axis=1, keepdims=True)
        a_s[...] = jnp.where(lane == t, dots, a_s[...])

    _token_loop(tb, lambda t, slot: _gather_rows(idx_ref, t, tab_ref, g_s[slot]), compute)
    act = a_s[...]
    gelu = 0.5 * act * (1.0 + lax.erf(act * (2.0 ** -0.5)))
    w_ref[...] = gate_ref[...] * gelu


def _table_spec(tab):
    return pl.BlockSpec(tab.shape, lambda i: (0, 0, 0), pipeline_mode=pl.Buffered(1))


def _peer_act(eidx, x3, gate, tab, tb):
    n = x3.shape[0]
    kern = functools.partial(_peer_act_kernel, tb=tb)
    return pl.pallas_call(
        kern,
        grid=(n // tb,),
        in_specs=[pl.BlockSpec((tb, PEER_PAIRS), lambda i: (i, 0), memory_space=pltpu.SMEM),
                  pl.BlockSpec((tb, D_MODEL), lambda i: (i, 0)),
                  pl.BlockSpec((PEER_PAIRS, tb), lambda i: (0, i)),
                  _table_spec(tab)],
        out_specs=pl.BlockSpec((PEER_PAIRS, tb), lambda i: (0, i)),
        out_shape=jax.ShapeDtypeStruct((PEER_PAIRS, n), F32),
        scratch_shapes=[pltpu.VMEM((PEER_PAIRS * ROW_CHUNKS, LANES), BF16)] * 2 + [
                        pltpu.VMEM((PEER_PAIRS, LANES), F32), pltpu.VMEM((PEER_PAIRS, tb), F32)],
        compiler_params=_cparams(("arbitrary",)),
        name="peer_act",
    )(eidx, x3, gate, tab)


def _peer_mix_kernel(idx_ref, w_ref, y_ref, g2_ref, tab_ref, o_ref, g0_s, g1_s, wb0_s, wb1_s, *, tb, per_token_mod):
    lane = lax.broadcasted_iota(jnp.int32, (PEER_PAIRS, tb), 1)
    g_s = (g0_s, g1_s)
    wb_s = (wb0_s, wb1_s)

    def stage(t, slot):
        _gather_rows(idx_ref, t, tab_ref, g_s[slot])
        col = jnp.sum(jnp.where(lane == t, w_ref[...], 0.0), axis=1, keepdims=True)
        wb_s[slot][...] = jnp.broadcast_to(col, (PEER_PAIRS, LANES))

    def compute(t, slot):
        accs = [jnp.zeros((ROW_CHUNKS, LANES), F32) for _ in range(4)]
        for j in range(PEER_PAIRS // 2):
            v0, v1 = _pair_rows(g_s[slot], j)
            w0 = jnp.broadcast_to(wb_s[slot][2 * j:2 * j + 1, :], (ROW_CHUNKS, LANES))
            w1 = jnp.broadcast_to(wb_s[slot][2 * j + 1:2 * j + 2, :], (ROW_CHUNKS, LANES))
            k = 2 * (j % 2)
            accs[k] = accs[k] + v0 * w0
            accs[k + 1] = accs[k + 1] + v1 * w1
        out = (accs[0] + accs[1]) + (accs[2] + accs[3])
        g2 = g2_ref[pl.ds(t, 1), :] if per_token_mod else g2_ref[...]
        out_row = jnp.concatenate([out[r:r + 1, :] for r in range(ROW_CHUNKS)], axis=1)
        o_ref[pl.ds(t, 1), :] = y_ref[pl.ds(t, 1), :] + g2 * out_row

    _token_loop(tb, stage, compute)


def _peer_mix(eidx, w, y3, gate2, per_token_mod, toks_per_seq, tab, tb):
    n = y3.shape[0]
    if per_token_mod:
        g_spec = pl.BlockSpec((tb, D_MODEL), lambda i: (i, 0))
    else:
        blocks_per_seq = toks_per_seq // tb
        g_spec = pl.BlockSpec((None, 1, D_MODEL), lambda i: (i // blocks_per_seq, 0, 0))
    kern = functools.partial(_peer_mix_kernel, tb=tb, per_token_mod=per_token_mod)
    return pl.pallas_call(
        kern,
        grid=(n // tb,),
        in_specs=[pl.BlockSpec((tb, PEER_PAIRS), lambda i: (i, 0), memory_space=pltpu.SMEM),
                  pl.BlockSpec((PEER_PAIRS, tb), lambda i: (0, i)),
                  pl.BlockSpec((tb, D_MODEL), lambda i: (i, 0)), g_spec, _table_spec(tab)],
        out_specs=pl.BlockSpec((tb, D_MODEL), lambda i: (i, 0)),
        out_shape=jax.ShapeDtypeStruct((n, D_MODEL), F32),
        scratch_shapes=[pltpu.VMEM((PEER_PAIRS * ROW_CHUNKS, LANES), BF16)] * 2 + [
                        pltpu.VMEM((PEER_PAIRS, LANES), F32)] * 2,
        compiler_params=_cparams(("arbitrary",)),
        name="peer_mix",
    )(eidx, w, y3, gate2, tab)


def _layer(x, mod, conv_buf, C0, n0, m0, k_buf, v_buf, rel_bias, wts, tm, tb):
    B, T, _ = x.shape
    n = B * T
    x2 = x.reshape(n, D_MODEL)
    shift1, scale1, gate1, shift2, scale2, gate2 = jnp.split(mod, 6, axis=-1)
    per_token = T % tm != 0
    if per_token:
        expand = lambda a: jnp.repeat(a, T, axis=0)
    else:
        expand = lambda a: a.reshape(B, 1, D_MODEL)

    q, k, v, mqk, mv, mo, gcol, grow = _in_proj(
        x2, expand(scale1), expand(shift1), per_token, T, tm, wts["g1"], wts["wm"], wts["wgc"], wts["wgr"],
        wts["bd"], wts["qg"], wts["kg"])

    if k_buf is None:
        att = _att_prompt(q, k, v, _prompt_bias(rel_bias), B, T)
        k_new = k.reshape(B, T, H_B, HD_B)
        v_new = v.reshape(B, T, H_B, HD_B)
    else:
        wbuf = k_buf.shape[1]
        bias_c, bias_n = _sample_bias(rel_bias, T, wbuf)
        att, k_new, v_new = _att_sample(q, k, v, k_buf.reshape(B, wbuf, W_B), v_buf.reshape(B, wbuf, W_B),
                                        bias_c, bias_n, B, T, wbuf)
        k_new = k_new.reshape(B, wbuf, H_B, HD_B)
        v_new = v_new.reshape(B, wbuf, H_B, HD_B)

    ha, C, nn, m = _mlstm(mqk, mv, gcol, grow, wts["conv_w"], wts["conv_b"], conv_buf, wts["b_gate"], C0, n0, m0, B, T)
    mqk3 = mqk.reshape(B, T, 2 * W_A)
    if T >= CONV_W - 1:
        conv_new = mqk3[:, T - (CONV_W - 1):]
    else:
        conv_new = jnp.concatenate([conv_buf, mqk3], axis=1)[:, -(CONV_W - 1):]

    y1, h2 = _out_proj(ha, mo, att, x2, expand(gate1), expand(scale2), expand(shift2), per_token, T, tm,
                       wts["mg"], wts["ag"], wts["g2"], wts["bd"], wts["wo"])

    eidx, gate = _peer_route(h2, wts["wq"], wts["keys"], tb)
    w = _peer_act(eidx, h2, gate, wts["u_tab"], tb)
    y = _peer_mix(eidx, w, y1, expand(gate2), per_token, T, wts["v_tab"], tb)
    return (y.reshape(B, T, D_MODEL), k_new, v_new, conv_new, C, nn.reshape(B, H_A, DK_A), m[:, :, 0, 0])


def _prep_weights(l, norm1_g, norm2_g, w_in, b_gate, conv_w, conv_b, q_norm_g, k_norm_g, att_out_g, mlstm_out_g,
                  w_out, peer_wq, peer_keys, peer_u, peer_v):
    w = w_in[l]
    wg = w[:, W_MAIN:]
    grp = np.arange(W_B) // HD_B
    bd = jnp.asarray((grp[:, None] == grp[None, :]).astype(np.float32) / HD_B, BF16)
    return dict(
        g1=norm1_g[l].reshape(1, -1), g2=norm2_g[l].reshape(1, -1),
        wm=w[:, :W_MAIN].astype(BF16),
        wgc=jnp.pad(wg, ((0, 0), (0, LANES - 2 * H_A))), wgr=wg.T,
        bd=bd, qg=jnp.tile(q_norm_g[l], H_B).reshape(1, -1), kg=jnp.tile(k_norm_g[l], H_B).reshape(1, -1),
        conv_w=conv_w[l], conv_b=conv_b[l], b_gate=b_gate[l],
        mg=mlstm_out_g[l].reshape(1, -1), ag=att_out_g[l].reshape(1, -1),
        wo=w_out[l].astype(BF16), wq=peer_wq[l].astype(BF16), keys=peer_keys[l],
        u_tab=_pack_table(peer_u[l]), v_tab=_pack_table(peer_v[l]),
    )


def kernel(x_prompt, x_sample, cache_k, cache_v, state_conv, state_C, state_n, state_m, c_prompt, c_sample,
           rel_bias, w_ada, b_ada, norm1_g, norm2_g, w_in, b_gate, conv_w, conv_b, q_norm_g, k_norm_g,
           att_out_g, mlstm_out_g, w_out, peer_wq, peer_keys, peer_u, peer_v):
    depth = w_ada.shape[0]
    bp = x_prompt.shape[0]
    bs = x_sample.shape[0]
    yp, ys = x_prompt, x_sample
    sp, ss = [], []
    for l in range(depth):
        wts = _prep_weights(l, norm1_g, norm2_g, w_in, b_gate, conv_w, conv_b, q_norm_g, k_norm_g, att_out_g,
                            mlstm_out_g, w_out, peer_wq, peer_keys, peer_u, peer_v)
        mod = _ada(jnp.concatenate([c_prompt, c_sample], axis=0), w_ada[l], b_ada[l])
        zc = jnp.zeros((bp, CONV_W - 1, 2 * W_A), F32)
        zC = jnp.zeros((bp, H_A, DK_A, DK_A), F32)
        zn = jnp.zeros((bp, H_A, DK_A), F32)
        zm = jnp.zeros((bp, H_A), F32)
        outp = _layer(yp, mod[:bp], zc, zC, zn, zm, None, None, rel_bias, wts, tm=256, tb=128)
        outs = _layer(ys, mod[bp:], state_conv[l], state_C[l], state_n[l], state_m[l], cache_k[l], cache_v[l],
                      rel_bias, wts, tm=256, tb=128)
        yp, ys = outp[0], outs[0]
        sp.append(outp[1:])
        ss.append(outs[1:])
    k_p, v_p, conv_p, C_p, n_p, m_p = [jnp.stack([s[i] for s in sp], axis=0) for i in range(6)]
    k_s, v_s, conv_s, C_s, n_s, m_s = [jnp.stack([s[i] for s in ss], axis=0) for i in range(6)]
    return (yp, ys, k_p, v_p, conv_p, C_p, n_p, m_p, k_s, v_s, conv_s, C_s, n_s, m_s)
```

```python
import functools
import math

import numpy as np
import jax
import jax.numpy as jnp
from jax import lax
from jax.experimental import pallas as pl
from jax.experimental.pallas import tpu as pltpu

F32 = jnp.float32
BF16 = jnp.bfloat16
HIGHEST = lax.Precision.HIGHEST

D_MODEL = 1024
H_A = 4
DK_A = 128
W_A = H_A * DK_A
H_B = 8
HD_B = 64
W_B = H_B * HD_B
CONV_W = 4
MLSTM_CHUNK = 64
MLSTM_HP = 4
DIL_PATTERNS = ((128, 1), (512, 4), (2048, 16))
ATT_STEPS = 128
ATT_UNROLL = 4
N_BUCKETS = 32
MAX_DIST = 2048
PEER_HEADS = 8
N_SUBKEYS = 128
PEER_TOPK = 16
PEER_PAIRS = PEER_HEADS * PEER_TOPK
EPS = 1e-6
NEG = -1e30
W_MAIN = 3 * W_B + 4 * W_A
LANES = 128
PROJ_TILE = 512
ROUTE_TILE = 256
PEER_BLOCK = LANES
VMEM_LIMIT = 56 * 1024 * 1024


def _cparams(sem):
    return pltpu.CompilerParams(dimension_semantics=sem, vmem_limit_bytes=VMEM_LIMIT)


def _nt_dot(a, b, precision=None):
    return lax.dot_general(a, b, (((1,), (1,)), ((), ())), preferred_element_type=F32, precision=precision)


def _t5_bucket(dist):
    max_exact = N_BUCKETS // 2
    d = np.maximum(dist, 1).astype(np.float32)
    large = max_exact + (np.log(d / max_exact) / math.log(MAX_DIST / max_exact) * (N_BUCKETS - max_exact)).astype(np.int32)
    large = np.minimum(large, N_BUCKETS - 1)
    return np.where(dist < max_exact, dist, large).astype(np.int32)


def _ada_kernel(c_ref, w_ref, b_ref, o_ref):
    c = c_ref[...]
    s = c * jax.nn.sigmoid(c)
    o_ref[...] = jnp.dot(s, w_ref[...], preferred_element_type=F32, precision=HIGHEST) + b_ref[...]


def _ada(c_all, w_ada, b_ada):
    n = c_all.shape[0]
    return pl.pallas_call(
        _ada_kernel,
        grid=(6,),
        in_specs=[pl.BlockSpec((n, D_MODEL), lambda j: (0, 0)),
                  pl.BlockSpec((D_MODEL, D_MODEL), lambda j: (0, j)),
                  pl.BlockSpec((1, D_MODEL), lambda j: (0, j))],
        out_specs=pl.BlockSpec((n, D_MODEL), lambda j: (0, j)),
        out_shape=jax.ShapeDtypeStruct((n, 6 * D_MODEL), F32),
        compiler_params=_cparams(("arbitrary",)),
        name="ada",
    )(c_all, w_ada, b_ada.reshape(1, -1))


def _group_mean_sq(a, bd):
    sq = a * a
    hi = sq.astype(BF16)
    lo = (sq - hi.astype(F32)).astype(BF16)
    return jnp.dot(hi, bd, preferred_element_type=F32) + jnp.dot(lo, bd, preferred_element_type=F32)


def _in_kernel(x_ref, sc_ref, sh_ref, g1_ref, wm_ref, wgc_ref, wgr_ref, bd_ref, qg_ref, kg_ref,
               q_ref, k_ref, v_ref, mqk_ref, mv_ref, mo_ref, gcol_ref, grow_ref):
    x = x_ref[...]
    ms = jnp.mean(x * x, axis=-1, keepdims=True)
    h = x * lax.rsqrt(ms + EPS) * g1_ref[...]
    h = h * (1.0 + sc_ref[...]) + sh_ref[...]
    y = jnp.dot(h.astype(BF16), wm_ref[...], preferred_element_type=F32)
    bd = bd_ref[...]
    aq = y[:, 0:W_B]
    ak = y[:, W_B:2 * W_B]
    q_ref[...] = aq * lax.rsqrt(_group_mean_sq(aq, bd) + EPS) * qg_ref[...] * (HD_B ** -0.5)
    k_ref[...] = ak * lax.rsqrt(_group_mean_sq(ak, bd) + EPS) * kg_ref[...]
    v_ref[...] = y[:, 2 * W_B:3 * W_B]
    o = 3 * W_B
    mqk_ref[...] = y[:, o:o + 2 * W_A]
    mv_ref[...] = y[:, o + 2 * W_A:o + 3 * W_A]
    mo_ref[...] = y[:, o + 3 * W_A:o + 4 * W_A]
    gcol_ref[...] = jnp.dot(h, wgc_ref[...], preferred_element_type=F32, precision=HIGHEST)
    grow_ref[...] = _nt_dot(wgr_ref[...], h, precision=HIGHEST)


def _in_proj(x2, scale, shift, per_token_mod, toks_per_seq, tm, g1, wm, wgc, wgr, bd, qg, kg):
    n = x2.shape[0]
    nt = n // tm
    if per_token_mod:
        mod_spec = pl.BlockSpec((tm, D_MODEL), lambda i: (i, 0))
    else:
        tiles_per_seq = toks_per_seq // tm
        mod_spec = pl.BlockSpec((None, 1, D_MODEL), lambda i: (i // tiles_per_seq, 0, 0))
    const = lambda shape: pl.BlockSpec(shape, lambda i: (0,) * len(shape), pipeline_mode=pl.Buffered(1))
    tok = lambda w: pl.BlockSpec((tm, w), lambda i: (i, 0))
    outs = pl.pallas_call(
        _in_kernel,
        grid=(nt,),
        in_specs=[tok(D_MODEL), mod_spec, mod_spec, const((1, D_MODEL)), const((D_MODEL, W_MAIN)),
                  const((D_MODEL, LANES)), const((8, D_MODEL)), const((W_B, W_B)), const((1, W_B)), const((1, W_B))],
        out_specs=[tok(W_B), tok(W_B), tok(W_B), tok(2 * W_A), tok(W_A), tok(W_A), tok(LANES),
                   pl.BlockSpec((8, tm), lambda i: (0, i))],
        out_shape=[jax.ShapeDtypeStruct((n, W_B), F32)] * 3 + [jax.ShapeDtypeStruct((n, 2 * W_A), F32)]
                  + [jax.ShapeDtypeStruct((n, W_A), F32)] * 2 + [jax.ShapeDtypeStruct((n, LANES), F32),
                                                                 jax.ShapeDtypeStruct((8, n), F32)],
        compiler_params=_cparams(("arbitrary",)),
        name="in_proj",
    )(x2, scale, shift, g1, wm, wgc, wgr, bd, qg, kg)
    return outs


def _log_sigmoid(x):
    return jnp.minimum(x, 0.0) - jnp.log1p(jnp.exp(-jnp.abs(x)))


def _mlstm_kernel(bg_ref, m0_ref, mq_ref, mk_ref, cwq_ref, cwk_ref, cbq_ref, cbk_ref, bufq_ref, bufk_ref,
                  v_ref, gcol_ref, grow_ref, c0_ref, n0_ref,
                  h_ref, c_ref, n_ref, m_ref, sq, sk, *, T, L, rows_whole):
    b = pl.program_id(0)
    hg = pl.program_id(1)
    nc = T // L

    def conv(u_ref, buf_ref, w_ref, cb_ref, s_ref, out_ref):
        s_ref[0:8, :] = jnp.zeros((8, MLSTM_HP * LANES), F32)
        s_ref[5:8, :] = buf_ref[...]
        s_ref[8:8 + T, :] = u_ref[...]
        y = cb_ref[...]
        for j in range(CONV_W):
            y = y + s_ref[5 + j:5 + j + T, :] * w_ref[j:j + 1, :]
        out_ref[...] = y * jax.nn.sigmoid(y)

    conv(mq_ref, bufq_ref, cwq_ref, cbq_ref, sq, h_ref)
    conv(mk_ref, bufk_ref, cwk_ref, cbk_ref, sk, sq.at[8:8 + T, :])

    lane = lax.broadcasted_iota(jnp.int32, (L, LANES), 1)
    ri = lax.broadcasted_iota(jnp.int32, (L, L), 0)
    ci = lax.broadcasted_iota(jnp.int32, (L, L), 1)
    causal = ri >= ci
    tri = causal.astype(F32)
    tri_t = (ri <= ci).astype(F32)

    def chunk(c, carry):
        r0 = pl.multiple_of(c * L, L)
        rc = c + (b * nc if rows_whole else 0)
        g = gcol_ref[pl.ds(r0, L), :]
        st = []
        for j in range(MLSTM_HP):
            C, n, m = carry[j]
            hd = hg * MLSTM_HP + j
            ig_b = bg_ref[hd]
            f_b = bg_ref[H_A + hd]
            cs = slice(j * LANES, (j + 1) * LANES)
            qc = h_ref[pl.ds(r0, L), cs]
            kc = sq[pl.ds(r0 + 8, L), cs] * (DK_A ** -0.5)
            vb = v_ref[pl.ds(r0, L), cs].astype(BF16)
            ig_col = jnp.sum(jnp.where(lane == hd, g, 0.0), axis=1, keepdims=True) + ig_b
            f_col = jnp.sum(jnp.where(lane == hd + H_A, g, 0.0), axis=1, keepdims=True) + f_b
            ig_row = grow_ref[hd, pl.ds(rc, 1), :] + ig_b
            lf_row = _log_sigmoid(grow_ref[hd + H_A, pl.ds(rc, 1), :] + f_b)
            b_col = jnp.dot(tri, jnp.broadcast_to(_log_sigmoid(f_col), (L, LANES)), preferred_element_type=F32,
                            precision=HIGHEST)[:, 0:1]
            b_row = jnp.dot(jnp.broadcast_to(lf_row, (8, L)), tri_t, preferred_element_type=F32,
                            precision=HIGHEST)[0:1, :]
            qb = qc.astype(BF16)
            qk = _nt_dot(qb, kc.astype(BF16))
            qC = jnp.dot(qb, C.astype(BF16), preferred_element_type=F32)
            st.append(dict(C=C, n=n, m=m, cs=cs, qc=qc, kc=kc, vb=vb, ig_col=ig_col, ig_row=ig_row,
                           b_col=b_col, b_row=b_row, qk=qk, qC=qC))
        for d in st:
            b_col, b_row, m = d["b_col"], d["b_row"], d["m"]
            g_col = b_col + m
            dm = jnp.where(causal, b_col - b_row + d["ig_row"], -jnp.inf)
            mt = jnp.maximum(g_col, jnp.max(dm, axis=1, keepdims=True))
            s = d["qk"] * jnp.exp(dm - mt)
            wg = jnp.exp(g_col - mt)
            bl = b_col[L - 1:L, :]
            m_new = jnp.maximum(bl + m, jnp.max(bl - b_row + d["ig_row"], axis=1, keepdims=True))
            kw = jnp.exp(bl - b_col + d["ig_col"] - m_new) * d["kc"]
            d.update(mt=mt, s=s, wg=wg, m_new=m_new, kw=kw, wc=jnp.exp(bl + m - m_new))
        out = []
        for d in st:
            s, wg = d["s"], d["wg"]
            num = wg * d["qC"] + jnp.dot(s.astype(BF16), d["vb"], preferred_element_type=F32)
            den = wg * jnp.sum(d["qc"] * d["n"], axis=1, keepdims=True) + jnp.sum(s, axis=1, keepdims=True)
            h_ref[pl.ds(r0, L), d["cs"]] = num / jnp.maximum(jnp.abs(den), jnp.exp(-d["mt"]))
            upd = lax.dot_general(d["kw"].astype(BF16), d["vb"], (((0,), (0,)), ((), ())), preferred_element_type=F32)
            out.append((d["wc"] * d["C"] + upd, d["wc"] * d["n"] + jnp.sum(d["kw"], axis=0, keepdims=True), d["m_new"]))
        return tuple(out)

    init = tuple((c0_ref[j], n0_ref[j], jnp.full((1, 1), m0_ref[b, hg * MLSTM_HP + j], F32))
                 for j in range(MLSTM_HP))
    final = lax.fori_loop(0, nc, chunk, init)
    for j in range(MLSTM_HP):
        c_ref[j] = final[j][0]
        n_ref[j] = final[j][1]
        m_ref[j] = jnp.broadcast_to(final[j][2], (1, LANES))


def _mlstm(mqk, mv, gcol, grow, conv_w, conv_b, conv_buf, b_gate, C0, n0, m0, B, T):
    L = math.gcd(T, MLSTM_CHUNK)
    nc = T // L
    n = B * T
    rows_whole = nc % 8 != 0
    grow3 = grow.reshape(8, n // L, L)
    if rows_whole:
        grow_spec = pl.BlockSpec((8, n // L, L), lambda b, h: (0, 0, 0))
    else:
        grow_spec = pl.BlockSpec((8, nc, L), lambda b, h: (0, b, 0))
    smem = pl.BlockSpec(memory_space=pltpu.SMEM)
    wide = MLSTM_HP * LANES
    ng = H_A // MLSTM_HP
    colq = lambda rows: pl.BlockSpec((rows, wide), lambda b, h: (0, h))
    colk = lambda rows: pl.BlockSpec((rows, wide), lambda b, h: (0, h + ng))
    st = lambda r, c: pl.BlockSpec((None, MLSTM_HP, r, c), lambda b, h: (b, h, 0, 0))
    kern = functools.partial(_mlstm_kernel, T=T, L=L, rows_whole=rows_whole)
    return pl.pallas_call(
        kern,
        grid=(B, ng),
        in_specs=[smem, smem,
                  pl.BlockSpec((T, wide), lambda b, h: (b, h)), pl.BlockSpec((T, wide), lambda b, h: (b, h + ng)),
                  colq(CONV_W), colk(CONV_W), colq(1), colk(1),
                  pl.BlockSpec((None, CONV_W - 1, wide), lambda b, h: (b, 0, h)),
                  pl.BlockSpec((None, CONV_W - 1, wide), lambda b, h: (b, 0, h + ng)),
                  pl.BlockSpec((T, wide), lambda b, h: (b, h)),
                  pl.BlockSpec((T, LANES), lambda b, h: (b, 0)),
                  grow_spec, st(DK_A, DK_A), st(1, DK_A)],
        out_specs=[pl.BlockSpec((T, wide), lambda b, h: (b, h)), st(DK_A, DK_A), st(1, DK_A), st(1, LANES)],
        out_shape=[jax.ShapeDtypeStruct((n, W_A), F32), jax.ShapeDtypeStruct((B, H_A, DK_A, DK_A), F32),
                   jax.ShapeDtypeStruct((B, H_A, 1, DK_A), F32), jax.ShapeDtypeStruct((B, H_A, 1, LANES), F32)],
        scratch_shapes=[pltpu.VMEM((T + 8, wide), F32), pltpu.VMEM((T + 8, wide), F32)],
        compiler_params=_cparams(("arbitrary", "arbitrary")),
        name="mlstm",
    )(b_gate, m0, mqk, mqk, conv_w, conv_w, conv_b.reshape(1, -1), conv_b.reshape(1, -1), conv_buf, conv_buf,
      mv, gcol, grow3, C0, n0.reshape(B, H_A, 1, DK_A))


def _att_prompt_kernel(q_ref, k_ref, v_ref, bias_ref, o_ref, num_s, mx_s, dn_s, mxs_s, *, T):
    lane = lax.broadcasted_iota(jnp.int32, (ATT_STEPS, LANES), 1)
    first = lane < HD_B
    nblk = T // ATT_STEPS

    for br, (_, dil) in enumerate(DIL_PATTERNS):
        span = ATT_STEPS * dil
        has_prev = T > span

        def rows(start, dil=dil):
            return pl.ds(start, ATT_STEPS, stride=dil) if dil > 1 else pl.ds(start, ATT_STEPS)

        def blk(i0, carry, br=br, dil=dil, span=span, has_prev=has_prev, rows=rows):
            units = []
            for u in range(ATT_UNROLL):
                i = i0 + u * (nblk // ATT_UNROLL)
                nb = i // dil
                start = nb * span + i % dil
                qb = q_ref[rows(start), :]
                kcur = k_ref[rows(start), :].astype(BF16)
                kprev = k_ref[rows(jnp.maximum(start - span, 0)), :].astype(BF16) if has_prev else None
                for hh in range(2):
                    own = first if hh == 0 else ~first
                    qh = jnp.where(own, qb, 0.0).astype(BF16)
                    lc = _nt_dot(qh, kcur) + bias_ref[br, hh, :, ATT_STEPS:2 * ATT_STEPS]
                    lp = None
                    if has_prev:
                        lp = _nt_dot(qh, kprev) + bias_ref[br, hh, :, 0:ATT_STEPS]
                        lp = jnp.where(nb > 0, lp, NEG)
                    units.append((start, own, lc, lp))
            weights = []
            for start, own, lc, lp in units:
                mx = jnp.max(jnp.maximum(lc, lp) if has_prev else lc, axis=1, keepdims=True)
                pc = jnp.exp(lc - mx).astype(BF16)
                pp = jnp.exp(lp - mx).astype(BF16) if has_prev else None
                weights.append((mx, pc, pp))
            res = []
            for (start, own, _, _), (mx, pc, pp) in zip(units, weights):
                acc = jnp.dot(pc, jnp.where(own, v_ref[rows(start), :], 1.0).astype(BF16), preferred_element_type=F32)
                if has_prev:
                    vprev = v_ref[rows(jnp.maximum(start - span, 0)), :]
                    acc = acc + jnp.dot(pp, jnp.where(own, vprev, 1.0).astype(BF16), preferred_element_type=F32)
                res.append((acc, mx))
            for u in range(ATT_UNROLL):
                start = units[2 * u][0]
                (a0, m0), (a1, m1) = res[2 * u], res[2 * u + 1]
                num_s[br, rows(start), :] = jnp.where(first, a0, a1)
                dn_s[br, rows(start), :] = jnp.where(first, a1, a0)
                mx_s[br, rows(start), :] = jnp.where(first, m0, m1)
                mxs_s[br, rows(start), :] = jnp.where(first, m1, m0)
            return carry

        lax.fori_loop(0, nblk // ATT_UNROLL, blk, 0)

    def comb(i, carry):
        rs = pl.ds(pl.multiple_of(i * ATT_STEPS, ATT_STEPS), ATT_STEPS)
        m = jnp.maximum(jnp.maximum(mx_s[0, rs, :], mx_s[1, rs, :]), mx_s[2, rs, :])
        ms = jnp.maximum(jnp.maximum(mxs_s[0, rs, :], mxs_s[1, rs, :]), mxs_s[2, rs, :])
        num = jnp.zeros((ATT_STEPS, LANES), F32)
        den = jnp.zeros((ATT_STEPS, LANES), F32)
        for br in range(3):
            num = num + jnp.exp(mx_s[br, rs, :] - m) * num_s[br, rs, :]
            den = den + jnp.exp(mxs_s[br, rs, :] - ms) * dn_s[br, rs, :]
        o_ref[rs, :] = num / pltpu.roll(den, HD_B, axis=1)
        return carry

    lax.fori_loop(0, nblk, comb, 0)


def _att_prompt(q, k, v, bias, B, T):
    n = B * T
    blk = pl.BlockSpec((T, LANES), lambda b, p: (b, p))
    kern = functools.partial(_att_prompt_kernel, T=T)
    return pl.pallas_call(
        kern,
        grid=(B, H_B // 2),
        in_specs=[blk, blk, blk, pl.BlockSpec((3, 2, ATT_STEPS, 2 * ATT_STEPS), lambda b, p: (0, p, 0, 0))],
        out_specs=blk,
        out_shape=jax.ShapeDtypeStruct((n, W_B), F32),
        scratch_shapes=[pltpu.VMEM((3, T, LANES), F32)] * 4,
        compiler_params=_cparams(("arbitrary", "arbitrary")),
        name="att_prompt",
    )(q, k, v, bias)


def _prompt_bias(rel_bias):
    tab = _distance_bias(rel_bias, ATT_STEPS * DIL_PATTERNS[-1][1] + 1)
    pad = jnp.full((H_B, ATT_STEPS - 1), NEG, F32)
    tabs = []
    for _, dil in DIL_PATTERNS:
        w = jnp.concatenate([pad, tab[:, 0:ATT_STEPS * dil + 1:dil], pad], axis=1)[:, ::-1]
        tabs.append(jnp.stack([w[:, ATT_STEPS - 1 - i:3 * ATT_STEPS - 1 - i] for i in range(ATT_STEPS)], axis=1))
    return jnp.stack(tabs, axis=0)


def _distance_bias(rel_bias, n):
    onehot = np.zeros((n, N_BUCKETS), np.float32)
    onehot[np.arange(n), _t5_bucket(np.arange(n))] = 1.0
    return jnp.dot(jnp.asarray(onehot), rel_bias.astype(F32), precision=HIGHEST).T


def _att_sample_kernel(q_ref, kn_ref, vn_ref, ck_ref, cv_ref, bc_ref, bn_ref, o_ref, ko_ref, vo_ref, *, S, WBUF):
    lane = lax.broadcasted_iota(jnp.int32, (S, LANES), 1)
    first = lane < HD_B
    for hp in range(H_B // 2):
        cs = slice(hp * LANES, (hp + 1) * LANES)
        kc = ck_ref[:, cs].astype(BF16)
        vc = cv_ref[:, cs].astype(BF16)
        kn = kn_ref[:, cs].astype(BF16)
        vn = vn_ref[:, cs].astype(BF16)
        qp = q_ref[:, cs]
        res = []
        for hh in range(2):
            h = 2 * hp + hh
            qh = jnp.where(first if hh == 0 else ~first, qp, 0.0).astype(BF16)
            lc = _nt_dot(qh, kc) + bc_ref[h]
            ln = _nt_dot(qh, kn) + bn_ref[h]
            mx = jnp.maximum(jnp.max(lc, axis=1, keepdims=True), jnp.max(ln, axis=1, keepdims=True))
            pc = jnp.exp(lc - mx)
            pn = jnp.exp(ln - mx)
            den = jnp.sum(pc, axis=1, keepdims=True) + jnp.sum(pn, axis=1, keepdims=True)
            num = jnp.dot(pc.astype(BF16), vc, preferred_element_type=F32) \
                + jnp.dot(pn.astype(BF16), vn, preferred_element_type=F32)
            res.append(num / den)
        o_ref[:, cs] = jnp.where(first, res[0], res[1])
    ko_ref[0:WBUF - S, :] = ck_ref[S:WBUF, :]
    ko_ref[WBUF - S:WBUF, :] = kn_ref[...]
    vo_ref[0:WBUF - S, :] = cv_ref[S:WBUF, :]
    vo_ref[WBUF - S:WBUF, :] = vn_ref[...]


def _att_sample(q, kn, vn, ck, cv, bias_c, bias_n, B, S, WBUF):
    tokb = pl.BlockSpec((S, W_B), lambda b: (b, 0))
    cache = pl.BlockSpec((None, WBUF, W_B), lambda b: (b, 0, 0))
    kern = functools.partial(_att_sample_kernel, S=S, WBUF=WBUF)
    return pl.pallas_call(
        kern,
        grid=(B,),
        in_specs=[tokb, tokb, tokb, cache, cache,
                  pl.BlockSpec((H_B, S, WBUF), lambda b: (0, 0, 0)), pl.BlockSpec((H_B, S, S), lambda b: (0, 0, 0))],
        out_specs=[tokb, cache, cache],
        out_shape=[jax.ShapeDtypeStruct((B * S, W_B), F32), jax.ShapeDtypeStruct((B, WBUF, W_B), F32),
                   jax.ShapeDtypeStruct((B, WBUF, W_B), F32)],
        compiler_params=_cparams(("arbitrary",)),
        name="att_sample",
    )(q, kn, vn, ck, cv, bias_c, bias_n)


def _sample_bias(rel_bias, S, WBUF):
    dist = np.arange(WBUF + S)
    mult = np.zeros(WBUF + S, np.int64)
    for w, dil in DIL_PATTERNS:
        mult += ((dist % dil == 0) & (dist <= w)).astype(np.int64)
    logm = np.where(mult > 0, np.log(np.maximum(mult, 1)), 0.0).astype(np.float32)
    tab = _distance_bias(rel_bias, WBUF + S) + logm[None, :]
    tab = jnp.where((mult > 0)[None, :], tab, NEG)
    rev = jnp.concatenate([tab[:, ::-1], jnp.full((H_B, S), NEG, F32)], axis=1)
    last = WBUF + S - 1
    bias_c = jnp.stack([rev[:, S - 1 - s:S - 1 - s + WBUF] for s in range(S)], axis=1)
    bias_n = jnp.stack([rev[:, last - s:last - s + S] for s in range(S)], axis=1)
    return bias_c, bias_n


def _out_kernel(ha_ref, mo_ref, att_ref, x_ref, g1_ref, sc_ref, sh_ref, mg_ref, ag_ref, g2_ref, bd_ref, wo_ref,
                y_ref, h2_ref):
    ha = ha_ref[...]
    parts = []
    for hd in range(H_A):
        a = ha[:, hd * DK_A:(hd + 1) * DK_A]
        parts.append(a * lax.rsqrt(jnp.mean(a * a, axis=-1, keepdims=True) + EPS))
    hn = jnp.concatenate(parts, axis=1) * mg_ref[...] * jax.nn.sigmoid(mo_ref[...])
    att = att_ref[...]
    an = att * lax.rsqrt(_group_mean_sq(att, bd_ref[...]) + EPS) * ag_ref[...]
    mix = jnp.dot(hn.astype(BF16), wo_ref[0:W_A, :], preferred_element_type=F32) \
        + jnp.dot(an.astype(BF16), wo_ref[W_A:W_A + W_B, :], preferred_element_type=F32)
    y = x_ref[...] + g1_ref[...] * mix
    y_ref[...] = y
    h2 = y * lax.rsqrt(jnp.mean(y * y, axis=-1, keepdims=True) + EPS) * g2_ref[...]
    h2_ref[...] = h2 * (1.0 + sc_ref[...]) + sh_ref[...]


def _out_proj(ha, mo, att, x2, gate1, scale2, shift2, per_token_mod, toks_per_seq, tm, mg, ag, g2, bd, wo):
    n = x2.shape[0]
    if per_token_mod:
        mod_spec = pl.BlockSpec((tm, D_MODEL), lambda i: (i, 0))
    else:
        tiles_per_seq = toks_per_seq // tm
        mod_spec = pl.BlockSpec((None, 1, D_MODEL), lambda i: (i // tiles_per_seq, 0, 0))
    const = lambda shape: pl.BlockSpec(shape, lambda i: (0,) * len(shape), pipeline_mode=pl.Buffered(1))
    tok = lambda w: pl.BlockSpec((tm, w), lambda i: (i, 0))
    return pl.pallas_call(
        _out_kernel,
        grid=(n // tm,),
        in_specs=[tok(W_A), tok(W_A), tok(W_B), tok(D_MODEL), mod_spec, mod_spec, mod_spec,
                  const((1, W_A)), const((1, W_B)), const((1, D_MODEL)), const((W_B, W_B)),
                  const((W_A + W_B, D_MODEL))],
        out_specs=[tok(D_MODEL), tok(D_MODEL)],
        out_shape=[jax.ShapeDtypeStruct((n, D_MODEL), F32)] * 2,
        compiler_params=_cparams(("arbitrary",)),
        name="out_proj",
    )(ha, mo, att, x2, gate1, scale2, shift2, mg, ag, g2, bd, wo)


def _top16(s, nrows):
    iota = lax.broadcasted_iota(jnp.int32, s.shape, 0).astype(F32)
    vals, idxs = [], []
    for _ in range(PEER_TOPK):
        m = jnp.max(s, axis=0, keepdims=True)
        pos = jnp.min(jnp.where(s == m, iota, float(nrows)), axis=0, keepdims=True)
        vals.append(m)
        idxs.append(pos)
        s = jnp.where(iota == pos, -jnp.inf, s)
    return jnp.concatenate(vals, axis=0), jnp.concatenate(idxs, axis=0)


def _peer_route_kernel(h2_ref, wq_ref, keys_ref, eidx_ref, gate_ref, sv_s, si_s, gt_s, et_s, *, tm):
    qh = jnp.dot(h2_ref[...].astype(BF16), wq_ref[...], preferred_element_type=F32)
    for hp in range(2 * PEER_HEADS):
        qs = qh[:, hp * N_SUBKEYS:(hp + 1) * N_SUBKEYS]
        s = _nt_dot(keys_ref[hp % 2], qs, precision=HIGHEST)
        v, i = _top16(s, N_SUBKEYS)
        sv_s[hp] = v
        si_s[hp] = i

    k = PEER_TOPK
    half = k // 2
    sub = lax.broadcasted_iota(jnp.int32, (half, tm), 0).astype(F32)
    pos = jnp.concatenate(
        [lax.broadcasted_iota(jnp.int32, (k, tm), 0).astype(F32)]
        + [float(a * k) + sub for a in range(1, half)] + [(sub + float(half)) * float(k)], axis=0)
    n_cand = float(k * k)

    def head(h, carry):
        sv0 = sv_s[2 * h]
        sv1 = sv_s[2 * h + 1]
        si0 = si_s[2 * h] * float(N_SUBKEYS)
        si1 = si_s[2 * h + 1]
        cands = [sv0[0:1, :] + sv1]
        cidxs = [si0[0:1, :] + si1]
        for a in range(1, half):
            c = sv0[a:a + 1, :] + sv1[0:half, :]
            nb = k // (a + 1)
            cands.append(c if nb >= half else jnp.where(sub < float(nb), c, -jnp.inf))
            cidxs.append(si0[a:a + 1, :] + si1[0:half, :])
        cands.append(sv0[half:k, :] + sv1[0:1, :])
        cidxs.append(si0[half:k, :] + si1[0:1, :])
        cand = jnp.concatenate(cands, axis=0)
        cidx = jnp.concatenate(cidxs, axis=0)
        fv, ev = [], []
        for _ in range(k):
            m = jnp.max(cand, axis=0, keepdims=True)
            first = jnp.min(jnp.where(cand == m, pos, n_cand), axis=0, keepdims=True)
            sel = pos == first
            ev.append(jnp.max(jnp.where(sel, cidx, -1.0), axis=0, keepdims=True))
            fv.append(m)
            cand = jnp.where(sel, -jnp.inf, cand)
        fvs = jnp.concatenate(fv, axis=0)
        e = jnp.exp(fvs - fv[0])
        rs = pl.ds(pl.multiple_of(h * k, k), k)
        gt_s[rs, :] = e / jnp.sum(e, axis=0, keepdims=True)
        et_s[rs, :] = jnp.concatenate(ev, axis=0)
        return carry

    lax.fori_loop(0, PEER_HEADS, head, 0)
    gate_ref[...] = gt_s[...]
    for j in range(tm // LANES):
        cs = slice(j * LANES, (j + 1) * LANES)
        eidx_ref[cs, :] = et_s[:, cs].T.astype(jnp.int32)


def _peer_route(h2, wq, keys, tm):
    n = h2.shape[0]
    kern = functools.partial(_peer_route_kernel, tm=tm)
    return pl.pallas_call(
        kern,
        grid=(n // tm,),
        in_specs=[pl.BlockSpec((tm, D_MODEL), lambda i: (i, 0)),
                  pl.BlockSpec((D_MODEL, 2 * PEER_HEADS * N_SUBKEYS), lambda i: (0, 0)),
                  pl.BlockSpec((2, N_SUBKEYS, N_SUBKEYS), lambda i: (0, 0, 0))],
        out_specs=[pl.BlockSpec((tm, PEER_PAIRS), lambda i: (i, 0)), pl.BlockSpec((PEER_PAIRS, tm), lambda i: (0, i))],
        out_shape=[jax.ShapeDtypeStruct((n, PEER_PAIRS), jnp.int32), jax.ShapeDtypeStruct((PEER_PAIRS, n), F32)],
        scratch_shapes=[pltpu.VMEM((2 * PEER_HEADS, PEER_TOPK, tm), F32)] * 2 + [pltpu.VMEM((PEER_PAIRS, tm), F32)] * 2,
        compiler_params=_cparams(("arbitrary",)),
        name="peer_route",
    )(h2, wq, keys)


ROW_CHUNKS = D_MODEL // LANES


def _pack_table(tab):
    return tab.astype(BF16).reshape(tab.shape[0], ROW_CHUNKS, LANES)


def _gather_rows(idx_ref, t, tab_ref, g_s):
    for p in range(PEER_PAIRS):
        g_s[ROW_CHUNKS * p:ROW_CHUNKS * (p + 1), :] = tab_ref[idx_ref[t, p]]


def _pair_rows(g_s, j):
    both = g_s[2 * ROW_CHUNKS * j:2 * ROW_CHUNKS * (j + 1), :].astype(F32)
    return both[0:ROW_CHUNKS, :], both[ROW_CHUNKS:2 * ROW_CHUNKS, :]


def _token_loop(tb, stage, compute):
    stage(0, 0)

    def step(i, carry):
        t0 = 2 * i
        stage(t0 + 1, 1)
        compute(t0, 0)
        stage(jnp.minimum(t0 + 2, tb - 1), 0)
        compute(t0 + 1, 1)
        return carry

    lax.fori_loop(0, tb // 2, step, 0)


def _fold_rows(a, sub):
    t = [x + pltpu.roll(x, 4, axis=0) for x in a]
    b = [jnp.where(sub < 4, t[j], t[j + 4]) for j in range(4)]
    c = [jnp.where(sub % 4 < 2, b[j] + pltpu.roll(b[j], 6, axis=0), b[j + 2] + pltpu.roll(b[j + 2], 2, axis=0))
         for j in range(2)]
    return jnp.where(sub % 2 == 0, c[0] + pltpu.roll(c[0], 7, axis=0), c[1] + pltpu.roll(c[1], 1, axis=0))


def _peer_act_kernel(idx_ref, x_ref, gate_ref, tab_ref, w_ref, g0_s, g1_s, m_s, a_s, *, tb):
    sub = lax.broadcasted_iota(jnp.int32, (ROW_CHUNKS, LANES), 0)
    lane = lax.broadcasted_iota(jnp.int32, (PEER_PAIRS, tb), 1)
    g_s = (g0_s, g1_s)
    a_s[...] = jnp.zeros((PEER_PAIRS, tb), F32)

    def compute(t, slot):
        row = x_ref[pl.ds(t, 1), :]
        x8 = jnp.concatenate([row[:, r * LANES:(r + 1) * LANES] for r in range(ROW_CHUNKS)], axis=0)
        for blk in range(PEER_PAIRS // 8):
            prods = []
            for j in range(4):
                u0, u1 = _pair_rows(g_s[slot], 4 * blk + j)
                prods += [u0 * x8, u1 * x8]
            m_s[8 * blk:8 * (blk + 1), :] = _fold_rows(prods, sub)
        dots = jnp.sum(m_s[...], axis=1, keepdims=True)
        a_s[...] = jnp.where(lane == t, dots, a_s[...])

    _token_loop(tb, lambda t, slot: _gather_rows(idx_ref, t, tab_ref, g_s[slot]), compute)
    act = a_s[...]
    gelu = 0.5 * act * (1.0 + lax.erf(act * (2.0 ** -0.5)))
    w_ref[...] = gate_ref[...] * gelu


def _table_spec(tab):
    return pl.BlockSpec(tab.shape, lambda i: (0, 0, 0), pipeline_mode=pl.Buffered(1))


def _peer_act(eidx, x3, gate, tab, tb):
    n = x3.shape[0]
    kern = functools.partial(_peer_act_kernel, tb=tb)
    return pl.pallas_call(
        kern,
        grid=(n // tb,),
        in_specs=[pl.BlockSpec((tb, PEER_PAIRS), lambda i: (i, 0), memory_space=pltpu.SMEM),
                  pl.BlockSpec((tb, D_MODEL), lambda i: (i, 0)),
                  pl.BlockSpec((PEER_PAIRS, tb), lambda i: (0, i)),
                  _table_spec(tab)],
        out_specs=pl.BlockSpec((PEER_PAIRS, tb), lambda i: (0, i)),
        out_shape=jax.ShapeDtypeStruct((PEER_PAIRS, n), F32),
        scratch_shapes=[pltpu.VMEM((PEER_PAIRS * ROW_CHUNKS, LANES), BF16)] * 2 + [
                        pltpu.VMEM((PEER_PAIRS, LANES), F32), pltpu.VMEM((PEER_PAIRS, tb), F32)],
        compiler_params=_cparams(("arbitrary",)),
        name="peer_act",
    )(eidx, x3, gate, tab)


def _peer_mix_kernel(idx_ref, w_ref, y_ref, g2_ref, tab_ref, o_ref, g0_s, g1_s, wb0_s, wb1_s, *, tb, per_token_mod):
    lane = lax.broadcasted_iota(jnp.int32, (PEER_PAIRS, tb), 1)
    g_s = (g0_s, g1_s)
    wb_s = (wb0_s, wb1_s)

    def stage(t, slot):
        _gather_rows(idx_ref, t, tab_ref, g_s[slot])
        col = jnp.sum(jnp.where(lane == t, w_ref[...], 0.0), axis=1, keepdims=True)
        wb_s[slot][...] = jnp.broadcast_to(col, (PEER_PAIRS, LANES))

    def compute(t, slot):
        accs = [jnp.zeros((ROW_CHUNKS, LANES), F32) for _ in range(4)]
        for j in range(PEER_PAIRS // 2):
            v0, v1 = _pair_rows(g_s[slot], j)
            w0 = jnp.broadcast_to(wb_s[slot][2 * j:2 * j + 1, :], (ROW_CHUNKS, LANES))
            w1 = jnp.broadcast_to(wb_s[slot][2 * j + 1:2 * j + 2, :], (ROW_CHUNKS, LANES))
            k = 2 * (j % 2)
            accs[k] = accs[k] + v0 * w0
            accs[k + 1] = accs[k + 1] + v1 * w1
        out = (accs[0] + accs[1]) + (accs[2] + accs[3])
        g2 = g2_ref[pl.ds(t, 1), :] if per_token_mod else g2_ref[...]
        out_row = jnp.concatenate([out[r:r + 1, :] for r in range(ROW_CHUNKS)], axis=1)
        o_ref[pl.ds(t, 1), :] = y_ref[pl.ds(t, 1), :] + g2 * out_row

    _token_loop(tb, stage, compute)


def _peer_mix(eidx, w, y3, gate2, per_token_mod, toks_per_seq, tab, tb):
    n = y3.shape[0]
    if per_token_mod:
        g_spec = pl.BlockSpec((tb, D_MODEL), lambda i: (i, 0))
    else:
        blocks_per_seq = toks_per_seq // tb
        g_spec = pl.BlockSpec((None, 1, D_MODEL), lambda i: (i // blocks_per_seq, 0, 0))
    kern = functools.partial(_peer_mix_kernel, tb=tb, per_token_mod=per_token_mod)
    return pl.pallas_call(
        kern,
        grid=(n // tb,),
        in_specs=[pl.BlockSpec((tb, PEER_PAIRS), lambda i: (i, 0), memory_space=pltpu.SMEM),
                  pl.BlockSpec((PEER_PAIRS, tb), lambda i: (0, i)),
                  pl.BlockSpec((tb, D_MODEL), lambda i: (i, 0)), g_spec, _table_spec(tab)],
        out_specs=pl.BlockSpec((tb, D_MODEL), lambda i: (i, 0)),
        out_shape=jax.ShapeDtypeStruct((n, D_MODEL), F32),
        scratch_shapes=[pltpu.VMEM((PEER_PAIRS * ROW_CHUNKS, LANES), BF16)] * 2 + [
                        pltpu.VMEM((PEER_PAIRS, LANES), F32)] * 2,
        compiler_params=_cparams(("arbitrary",)),
        name="peer_mix",
    )(eidx, w, y3, gate2, tab)


def _layer(x, mod, conv_buf, C0, n0, m0, k_buf, v_buf, rel_bias, wts):
    B, T, _ = x.shape
    n = B * T
    tm = min(PROJ_TILE, n)
    tb = PEER_BLOCK
    x2 = x.reshape(n, D_MODEL)
    shift1, scale1, gate1, shift2, scale2, gate2 = jnp.split(mod, 6, axis=-1)
    per_token = T % tm != 0
    if per_token:
        expand = lambda a: jnp.repeat(a, T, axis=0)
    else:
        expand = lambda a: a.reshape(B, 1, D_MODEL)

    q, k, v, mqk, mv, mo, gcol, grow = _in_proj(
        x2, expand(scale1), expand(shift1), per_token, T, tm, wts["g1"], wts["wm"], wts["wgc"], wts["wgr"],
        wts["bd"], wts["qg"], wts["kg"])

    if k_buf is None:
        att = _att_prompt(q, k, v, _prompt_bias(rel_bias), B, T)
        k_new = k.reshape(B, T, H_B, HD_B)
        v_new = v.reshape(B, T, H_B, HD_B)
    else:
        wbuf = k_buf.shape[1]
        bias_c, bias_n = _sample_bias(rel_bias, T, wbuf)
        att, k_new, v_new = _att_sample(q, k, v, k_buf.reshape(B, wbuf, W_B), v_buf.reshape(B, wbuf, W_B),
                                        bias_c, bias_n, B, T, wbuf)
        k_new = k_new.reshape(B, wbuf, H_B, HD_B)
        v_new = v_new.reshape(B, wbuf, H_B, HD_B)

    ha, C, nn, m = _mlstm(mqk, mv, gcol, grow, wts["conv_w"], wts["conv_b"], conv_buf, wts["b_gate"], C0, n0, m0, B, T)
    mqk3 = mqk.reshape(B, T, 2 * W_A)
    if T >= CONV_W - 1:
        conv_new = mqk3[:, T - (CONV_W - 1):]
    else:
        conv_new = jnp.concatenate([conv_buf, mqk3], axis=1)[:, -(CONV_W - 1):]

    y1, h2 = _out_proj(ha, mo, att, x2, expand(gate1), expand(scale2), expand(shift2), per_token, T, tm,
                       wts["mg"], wts["ag"], wts["g2"], wts["bd"], wts["wo"])

    eidx, gate = _peer_route(h2, wts["wq"], wts["keys"], min(ROUTE_TILE, n))
    w = _peer_act(eidx, h2, gate, wts["u_tab"], tb)
    y = _peer_mix(eidx, w, y1, expand(gate2), per_token, T, wts["v_tab"], tb)
    return (y.reshape(B, T, D_MODEL), k_new, v_new, conv_new, C, nn.reshape(B, H_A, DK_A), m[:, :, 0, 0])


def _prep_weights(l, norm1_g, norm2_g, w_in, b_gate, conv_w, conv_b, q_norm_g, k_norm_g, att_out_g, mlstm_out_g,
                  w_out, peer_wq, peer_keys, peer_u, peer_v):
    w = w_in[l]
    wg = w[:, W_MAIN:]
    grp = np.arange(W_B) // HD_B
    bd = jnp.asarray((grp[:, None] == grp[None, :]).astype(np.float32) / HD_B, BF16)
    return dict(
        g1=norm1_g[l].reshape(1, -1), g2=norm2_g[l].reshape(1, -1),
        wm=w[:, :W_MAIN].astype(BF16),
        wgc=jnp.pad(wg, ((0, 0), (0, LANES - 2 * H_A))), wgr=wg.T,
        bd=bd, qg=jnp.tile(q_norm_g[l], H_B).reshape(1, -1), kg=jnp.tile(k_norm_g[l], H_B).reshape(1, -1),
        conv_w=conv_w[l], conv_b=conv_b[l], b_gate=b_gate[l],
        mg=mlstm_out_g[l].reshape(1, -1), ag=att_out_g[l].reshape(1, -1),
        wo=w_out[l].astype(BF16), wq=peer_wq[l].astype(BF16), keys=peer_keys[l],
        u_tab=_pack_table(peer_u[l]), v_tab=_pack_table(peer_v[l]),
    )


def kernel(x_prompt, x_sample, cache_k, cache_v, state_conv, state_C, state_n, state_m, c_prompt, c_sample,
           rel_bias, w_ada, b_ada, norm1_g, norm2_g, w_in, b_gate, conv_w, conv_b, q_norm_g, k_norm_g,
           att_out_g, mlstm_out_g, w_out, peer_wq, peer_keys, peer_u, peer_v):
    depth = w_ada.shape[0]
    bp = x_prompt.shape[0]
    bs = x_sample.shape[0]
    yp, ys = x_prompt, x_sample
    sp, ss = [], []
    for l in range(depth):
        wts = _prep_weights(l, norm1_g, norm2_g, w_in, b_gate, conv_w, conv_b, q_norm_g, k_norm_g, att_out_g,
                            mlstm_out_g, w_out, peer_wq, peer_keys, peer_u, peer_v)
        mod = _ada(jnp.concatenate([c_prompt, c_sample], axis=0), w_ada[l], b_ada[l])
        zc = jnp.zeros((bp, CONV_W - 1, 2 * W_A), F32)
        zC = jnp.zeros((bp, H_A, DK_A, DK_A), F32)
        zn = jnp.zeros((bp, H_A, DK_A), F32)
        zm = jnp.zeros((bp, H_A), F32)
        outp = _layer(yp, mod[:bp], zc, zC, zn, zm, None, None, rel_bias, wts)
        outs = _layer(ys, mod[bp:], state_conv[l], state_C[l], state_n[l], state_m[l], cache_k[l], cache_v[l],
                      rel_bias, wts)
        yp, ys = outp[0], outs[0]
        sp.append(outp[1:])
        ss.append(outs[1:])
    k_p, v_p, conv_p, C_p, n_p, m_p = [jnp.stack([s[i] for s in sp], axis=0) for i in range(6)]
    k_s, v_s, conv_s, C_s, n_s, m_s = [jnp.stack([s[i] for s in ss], axis=0) for i in range(6)]
    return (yp, ys, k_p, v_p, conv_p, C_p, n_p, m_p, k_s, v_s, conv_s, C_s, n_s, m_s)
```

```python
import functools
import math

import numpy as np
import jax
import jax.numpy as jnp
from jax import lax
from jax.experimental import pallas as pl
from jax.experimental.pallas import tpu as pltpu

F32 = jnp.float32
BF16 = jnp.bfloat16
HIGHEST = lax.Precision.HIGHEST

D_MODEL = 1024
H_A = 4
DK_A = 128
W_A = H_A * DK_A
H_B = 8
HD_B = 64
W_B = H_B * HD_B
CONV_W = 4
MLSTM_CHUNK = 64
MLSTM_HP = 4
DIL_PATTERNS = ((128, 1), (512, 4), (2048, 16))
ATT_STEPS = 128
ATT_UNROLL = 4
N_BUCKETS = 32
MAX_DIST = 2048
PEER_HEADS = 8
N_SUBKEYS = 128
PEER_TOPK = 16
PEER_PAIRS = PEER_HEADS * PEER_TOPK
EPS = 1e-6
NEG = -1e30
W_MAIN = 3 * W_B + 4 * W_A
LANES = 128
PROJ_TILE = 512
ROUTE_TILE = 256
PEER_BLOCK = LANES
VMEM_LIMIT = 56 * 1024 * 1024


def _cparams(sem):
    return pltpu.CompilerParams(dimension_semantics=sem, vmem_limit_bytes=VMEM_LIMIT)


def _nt_dot(a, b, precision=None):
    return lax.dot_general(a, b, (((1,), (1,)), ((), ())), preferred_element_type=F32, precision=precision)


def _t5_bucket(dist):
    max_exact = N_BUCKETS // 2
    d = np.maximum(dist, 1).astype(np.float32)
    large = max_exact + (np.log(d / max_exact) / math.log(MAX_DIST / max_exact) * (N_BUCKETS - max_exact)).astype(np.int32)
    large = np.minimum(large, N_BUCKETS - 1)
    return np.where(dist < max_exact, dist, large).astype(np.int32)


def _ada_kernel(c_ref, w_ref, b_ref, o_ref):
    c = c_ref[...]
    s = c * jax.nn.sigmoid(c)
    o_ref[...] = jnp.dot(s, w_ref[...], preferred_element_type=F32, precision=HIGHEST) + b_ref[...]


def _ada(c_all, w_ada, b_ada):
    n = c_all.shape[0]
    return pl.pallas_call(
        _ada_kernel,
        grid=(6,),
        in_specs=[pl.BlockSpec((n, D_MODEL), lambda j: (0, 0)),
                  pl.BlockSpec((D_MODEL, D_MODEL), lambda j: (0, j)),
                  pl.BlockSpec((1, D_MODEL), lambda j: (0, j))],
        out_specs=pl.BlockSpec((n, D_MODEL), lambda j: (0, j)),
        out_shape=jax.ShapeDtypeStruct((n, 6 * D_MODEL), F32),
        compiler_params=_cparams(("arbitrary",)),
        name="ada",
    )(c_all, w_ada, b_ada.reshape(1, -1))


def _group_mean_sq(a, bd):
    sq = a * a
    hi = sq.astype(BF16)
    lo = (sq - hi.astype(F32)).astype(BF16)
    return jnp.dot(hi, bd, preferred_element_type=F32) + jnp.dot(lo, bd, preferred_element_type=F32)


def _in_kernel(x_ref, sc_ref, sh_ref, g1_ref, wm_ref, wgc_ref, wgr_ref, bd_ref, qg_ref, kg_ref,
               q_ref, k_ref, v_ref, mqk_ref, mv_ref, mo_ref, gcol_ref, grow_ref):
    x = x_ref[...]
    ms = jnp.mean(x * x, axis=-1, keepdims=True)
    h = x * lax.rsqrt(ms + EPS) * g1_ref[...]
    h = h * (1.0 + sc_ref[...]) + sh_ref[...]
    y = jnp.dot(h.astype(BF16), wm_ref[...], preferred_element_type=F32)
    bd = bd_ref[...]
    aq = y[:, 0:W_B]
    ak = y[:, W_B:2 * W_B]
    q_ref[...] = aq * lax.rsqrt(_group_mean_sq(aq, bd) + EPS) * qg_ref[...] * (HD_B ** -0.5)
    k_ref[...] = ak * lax.rsqrt(_group_mean_sq(ak, bd) + EPS) * kg_ref[...]
    v_ref[...] = y[:, 2 * W_B:3 * W_B]
    o = 3 * W_B
    mqk_ref[...] = y[:, o:o + 2 * W_A]
    mv_ref[...] = y[:, o + 2 * W_A:o + 3 * W_A]
    mo_ref[...] = y[:, o + 3 * W_A:o + 4 * W_A]
    gcol_ref[...] = jnp.dot(h, wgc_ref[...], preferred_element_type=F32, precision=HIGHEST)
    grow_ref[...] = _nt_dot(wgr_ref[...], h, precision=HIGHEST)


def _in_proj(x2, scale, shift, per_token_mod, toks_per_seq, tm, g1, wm, wgc, wgr, bd, qg, kg):
    n = x2.shape[0]
    nt = n // tm
    if per_token_mod:
        mod_spec = pl.BlockSpec((tm, D_MODEL), lambda i: (i, 0))
    else:
        tiles_per_seq = toks_per_seq // tm
        mod_spec = pl.BlockSpec((None, 1, D_MODEL), lambda i: (i // tiles_per_seq, 0, 0))
    const = lambda shape: pl.BlockSpec(shape, lambda i: (0,) * len(shape), pipeline_mode=pl.Buffered(1))
    tok = lambda w: pl.BlockSpec((tm, w), lambda i: (i, 0))
    outs = pl.pallas_call(
        _in_kernel,
        grid=(nt,),
        in_specs=[tok(D_MODEL), mod_spec, mod_spec, const((1, D_MODEL)), const((D_MODEL, W_MAIN)),
                  const((D_MODEL, LANES)), const((8, D_MODEL)), const((W_B, W_B)), const((1, W_B)), const((1, W_B))],
        out_specs=[tok(W_B), tok(W_B), tok(W_B), tok(2 * W_A), tok(W_A), tok(W_A), tok(LANES),
                   pl.BlockSpec((8, tm), lambda i: (0, i))],
        out_shape=[jax.ShapeDtypeStruct((n, W_B), F32)] * 3 + [jax.ShapeDtypeStruct((n, 2 * W_A), F32)]
                  + [jax.ShapeDtypeStruct((n, W_A), F32)] * 2 + [jax.ShapeDtypeStruct((n, LANES), F32),
                                                                 jax.ShapeDtypeStruct((8, n), F32)],
        compiler_params=_cparams(("arbitrary",)),
        name="in_proj",
    )(x2, scale, shift, g1, wm, wgc, wgr, bd, qg, kg)
    return outs


def _log_sigmoid(x):
    return jnp.minimum(x, 0.0) - jnp.log1p(jnp.exp(-jnp.abs(x)))


def _mlstm_kernel(bg_ref, m0_ref, mq_ref, mk_ref, cwq_ref, cwk_ref, cbq_ref, cbk_ref, bufq_ref, bufk_ref,
                  v_ref, gcol_ref, grow_ref, c0_ref, n0_ref,
                  h_ref, c_ref, n_ref, m_ref, sq, sk, *, T, L, rows_whole):
    b = pl.program_id(0)
    hg = pl.program_id(1)
    nc = T // L

    def conv(u_ref, buf_ref, w_ref, cb_ref, s_ref, out_ref):
        s_ref[0:8, :] = jnp.zeros((8, MLSTM_HP * LANES), F32)
        s_ref[5:8, :] = buf_ref[...]
        s_ref[8:8 + T, :] = u_ref[...]
        y = cb_ref[...]
        for j in range(CONV_W):
            y = y + s_ref[5 + j:5 + j + T, :] * w_ref[j:j + 1, :]
        out_ref[...] = y * jax.nn.sigmoid(y)

    conv(mq_ref, bufq_ref, cwq_ref, cbq_ref, sq, h_ref)
    conv(mk_ref, bufk_ref, cwk_ref, cbk_ref, sk, sq.at[8:8 + T, :])

    lane = lax.broadcasted_iota(jnp.int32, (L, LANES), 1)
    ri = lax.broadcasted_iota(jnp.int32, (L, L), 0)
    ci = lax.broadcasted_iota(jnp.int32, (L, L), 1)
    causal = ri >= ci
    tri = causal.astype(F32)
    tri_t = (ri <= ci).astype(F32)

    def chunk(c, carry):
        r0 = pl.multiple_of(c * L, L)
        rc = c + (b * nc if rows_whole else 0)
        g = gcol_ref[pl.ds(r0, L), :]
        st = []
        for j in range(MLSTM_HP):
            C, n, m = carry[j]
            hd = hg * MLSTM_HP + j
            ig_b = bg_ref[hd]
            f_b = bg_ref[H_A + hd]
            cs = slice(j * LANES, (j + 1) * LANES)
            qc = h_ref[pl.ds(r0, L), cs]
            kc = sq[pl.ds(r0 + 8, L), cs] * (DK_A ** -0.5)
            vb = v_ref[pl.ds(r0, L), cs].astype(BF16)
            ig_col = jnp.sum(jnp.where(lane == hd, g, 0.0), axis=1, keepdims=True) + ig_b
            f_col = jnp.sum(jnp.where(lane == hd + H_A, g, 0.0), axis=1, keepdims=True) + f_b
            ig_row = grow_ref[hd, pl.ds(rc, 1), :] + ig_b
            lf_row = _log_sigmoid(grow_ref[hd + H_A, pl.ds(rc, 1), :] + f_b)
            b_col = jnp.dot(tri, jnp.broadcast_to(_log_sigmoid(f_col), (L, LANES)), preferred_element_type=F32,
                            precision=HIGHEST)[:, 0:1]
            b_row = jnp.dot(jnp.broadcast_to(lf_row, (8, L)), tri_t, preferred_element_type=F32,
                            precision=HIGHEST)[0:1, :]
            qb = qc.astype(BF16)
            qk = _nt_dot(qb, kc.astype(BF16))
            qC = jnp.dot(qb, C.astype(BF16), preferred_element_type=F32)
            st.append(dict(C=C, n=n, m=m, cs=cs, qc=qc, kc=kc, vb=vb, ig_col=ig_col, ig_row=ig_row,
                           b_col=b_col, b_row=b_row, qk=qk, qC=qC))
        for d in st:
            b_col, b_row, m = d["b_col"], d["b_row"], d["m"]
            g_col = b_col + m
            dm = jnp.where(causal, b_col - b_row + d["ig_row"], -jnp.inf)
            mt = jnp.maximum(g_col, jnp.max(dm, axis=1, keepdims=True))
            s = d["qk"] * jnp.exp(dm - mt)
            wg = jnp.exp(g_col - mt)
            bl = b_col[L - 1:L, :]
            m_new = jnp.maximum(bl + m, jnp.max(bl - b_row + d["ig_row"], axis=1, keepdims=True))
            kw = jnp.exp(bl - b_col + d["ig_col"] - m_new) * d["kc"]
            d.update(mt=mt, s=s, wg=wg, m_new=m_new, kw=kw, wc=jnp.exp(bl + m - m_new))
        out = []
        for d in st:
            s, wg = d["s"], d["wg"]
            num = wg * d["qC"] + jnp.dot(s.astype(BF16), d["vb"], preferred_element_type=F32)
            den = wg * jnp.sum(d["qc"] * d["n"], axis=1, keepdims=True) + jnp.sum(s, axis=1, keepdims=True)
            h_ref[pl.ds(r0, L), d["cs"]] = num / jnp.maximum(jnp.abs(den), jnp.exp(-d["mt"]))
            upd = lax.dot_general(d["kw"].astype(BF16), d["vb"], (((0,), (0,)), ((), ())), preferred_element_type=F32)
            out.append((d["wc"] * d["C"] + upd, d["wc"] * d["n"] + jnp.sum(d["kw"], axis=0, keepdims=True), d["m_new"]))
        return tuple(out)

    init = tuple((c0_ref[j], n0_ref[j], jnp.full((1, 1), m0_ref[b, hg * MLSTM_HP + j], F32))
                 for j in range(MLSTM_HP))
    final = lax.fori_loop(0, nc, chunk, init)
    for j in range(MLSTM_HP):
        c_ref[j] = final[j][0]
        n_ref[j] = final[j][1]
        m_ref[j] = jnp.broadcast_to(final[j][2], (1, LANES))


def _mlstm(mqk, mv, gcol, grow, conv_w, conv_b, conv_buf, b_gate, C0, n0, m0, B, T):
    L = math.gcd(T, MLSTM_CHUNK)
    nc = T // L
    n = B * T
    rows_whole = nc % 8 != 0
    grow3 = grow.reshape(8, n // L, L)
    if rows_whole:
        grow_spec = pl.BlockSpec((8, n // L, L), lambda b, h: (0, 0, 0))
    else:
        grow_spec = pl.BlockSpec((8, nc, L), lambda b, h: (0, b, 0))
    smem = pl.BlockSpec(memory_space=pltpu.SMEM)
    wide = MLSTM_HP * LANES
    ng = H_A // MLSTM_HP
    colq = lambda rows: pl.BlockSpec((rows, wide), lambda b, h: (0, h))
    colk = lambda rows: pl.BlockSpec((rows, wide), lambda b, h: (0, h + ng))
    st = lambda r, c: pl.BlockSpec((None, MLSTM_HP, r, c), lambda b, h: (b, h, 0, 0))
    kern = functools.partial(_mlstm_kernel, T=T, L=L, rows_whole=rows_whole)
    return pl.pallas_call(
        kern,
        grid=(B, ng),
        in_specs=[smem, smem,
                  pl.BlockSpec((T, wide), lambda b, h: (b, h)), pl.BlockSpec((T, wide), lambda b, h: (b, h + ng)),
                  colq(CONV_W), colk(CONV_W), colq(1), colk(1),
                  pl.BlockSpec((None, CONV_W - 1, wide), lambda b, h: (b, 0, h)),
                  pl.BlockSpec((None, CONV_W - 1, wide), lambda b, h: (b, 0, h + ng)),
                  pl.BlockSpec((T, wide), lambda b, h: (b, h)),
                  pl.BlockSpec((T, LANES), lambda b, h: (b, 0)),
                  grow_spec, st(DK_A, DK_A), st(1, DK_A)],
        out_specs=[pl.BlockSpec((T, wide), lambda b, h: (b, h)), st(DK_A, DK_A), st(1, DK_A), st(1, LANES)],
        out_shape=[jax.ShapeDtypeStruct((n, W_A), F32), jax.ShapeDtypeStruct((B, H_A, DK_A, DK_A), F32),
                   jax.ShapeDtypeStruct((B, H_A, 1, DK_A), F32), jax.ShapeDtypeStruct((B, H_A, 1, LANES), F32)],
        scratch_shapes=[pltpu.VMEM((T + 8, wide), F32), pltpu.VMEM((T + 8, wide), F32)],
        compiler_params=_cparams(("arbitrary", "arbitrary")),
        name="mlstm",
    )(b_gate, m0, mqk, mqk, conv_w, conv_w, conv_b.reshape(1, -1), conv_b.reshape(1, -1), conv_buf, conv_buf,
      mv, gcol, grow3, C0, n0.reshape(B, H_A, 1, DK_A))


def _att_prompt_kernel(q_ref, k_ref, v_ref, bias_ref, o_ref, num_s, mx_s, dn_s, mxs_s, *, T):
    lane = lax.broadcasted_iota(jnp.int32, (ATT_STEPS, LANES), 1)
    first = lane < HD_B
    nblk = T // ATT_STEPS

    for br, (_, dil) in enumerate(DIL_PATTERNS):
        span = ATT_STEPS * dil
        has_prev = T > span

        def rows(start, dil=dil):
            return pl.ds(start, ATT_STEPS, stride=dil) if dil > 1 else pl.ds(start, ATT_STEPS)

        def blk(i0, carry, br=br, dil=dil, span=span, has_prev=has_prev, rows=rows):
            units = []
            for u in range(ATT_UNROLL):
                i = i0 + u * (nblk // ATT_UNROLL)
                nb = i // dil
                start = nb * span + i % dil
                qb = q_ref[rows(start), :]
                kcur = k_ref[rows(start), :].astype(BF16)
                kprev = k_ref[rows(jnp.maximum(start - span, 0)), :].astype(BF16) if has_prev else None
                for hh in range(2):
                    own = first if hh == 0 else ~first
                    qh = jnp.where(own, qb, 0.0).astype(BF16)
                    lc = _nt_dot(qh, kcur) + bias_ref[br, hh, :, ATT_STEPS:2 * ATT_STEPS]
                    lp = None
                    if has_prev:
                        lp = _nt_dot(qh, kprev) + bias_ref[br, hh, :, 0:ATT_STEPS]
                        lp = jnp.where(nb > 0, lp, NEG)
                    units.append((start, own, lc, lp))
            weights = []
            for start, own, lc, lp in units:
                mx = jnp.max(jnp.maximum(lc, lp) if has_prev else lc, axis=1, keepdims=True)
                pc = jnp.exp(lc - mx).astype(BF16)
                pp = jnp.exp(lp - mx).astype(BF16) if has_prev else None
                weights.append((mx, pc, pp))
            res = []
            for (start, own, _, _), (mx, pc, pp) in zip(units, weights):
                acc = jnp.dot(pc, jnp.where(own, v_ref[rows(start), :], 1.0).astype(BF16), preferred_element_type=F32)
                if has_prev:
                    vprev = v_ref[rows(jnp.maximum(start - span, 0)), :]
                    acc = acc + jnp.dot(pp, jnp.where(own, vprev, 1.0).astype(BF16), preferred_element_type=F32)
                res.append((acc, mx))
            for u in range(ATT_UNROLL):
                start = units[2 * u][0]
                (a0, m0), (a1, m1) = res[2 * u], res[2 * u + 1]
                num_s[br, rows(start), :] = jnp.where(first, a0, a1)
                dn_s[br, rows(start), :] = jnp.where(first, a1, a0)
                mx_s[br, rows(start), :] = jnp.where(first, m0, m1)
                mxs_s[br, rows(start), :] = jnp.where(first, m1, m0)
            return carry

        lax.fori_loop(0, nblk // ATT_UNROLL, blk, 0)

    def comb(i, carry):
        rs = pl.ds(pl.multiple_of(i * ATT_STEPS, ATT_STEPS), ATT_STEPS)
        m = jnp.maximum(jnp.maximum(mx_s[0, rs, :], mx_s[1, rs, :]), mx_s[2, rs, :])
        ms = jnp.maximum(jnp.maximum(mxs_s[0, rs, :], mxs_s[1, rs, :]), mxs_s[2, rs, :])
        num = jnp.zeros((ATT_STEPS, LANES), F32)
        den = jnp.zeros((ATT_STEPS, LANES), F32)
        for br in range(3):
            num = num + jnp.exp(mx_s[br, rs, :] - m) * num_s[br, rs, :]
            den = den + jnp.exp(mxs_s[br, rs, :] - ms) * dn_s[br, rs, :]
        o_ref[rs, :] = num / pltpu.roll(den, HD_B, axis=1)
        return carry

    lax.fori_loop(0, nblk, comb, 0)


def _att_prompt(q, k, v, bias, B, T):
    n = B * T
    blk = pl.BlockSpec((T, LANES), lambda b, p: (b, p))
    kern = functools.partial(_att_prompt_kernel, T=T)
    return pl.pallas_call(
        kern,
        grid=(B, H_B // 2),
        in_specs=[blk, blk, blk, pl.BlockSpec((3, 2, ATT_STEPS, 2 * ATT_STEPS), lambda b, p: (0, p, 0, 0))],
        out_specs=blk,
        out_shape=jax.ShapeDtypeStruct((n, W_B), F32),
        scratch_shapes=[pltpu.VMEM((3, T, LANES), F32)] * 4,
        compiler_params=_cparams(("arbitrary", "arbitrary")),
        name="att_prompt",
    )(q, k, v, bias)


def _prompt_bias(rel_bias):
    tab = _distance_bias(rel_bias, ATT_STEPS * DIL_PATTERNS[-1][1] + 1)
    pad = jnp.full((H_B, ATT_STEPS - 1), NEG, F32)
    period = 3 * ATT_STEPS - 1
    tabs = []
    for _, dil in DIL_PATTERNS:
        w = jnp.concatenate([pad, tab[:, 0:ATT_STEPS * dil + 1:dil], pad], axis=1)[:, ::-1]
        w = jnp.concatenate([w[:, ATT_STEPS - 1:], w[:, :ATT_STEPS - 1]], axis=1)
        flat = jnp.tile(w, (1, ATT_STEPS))[:, :ATT_STEPS * (period - 1)]
        tabs.append(flat.reshape(H_B, ATT_STEPS, period - 1)[:, :, :2 * ATT_STEPS])
    return jnp.stack(tabs, axis=0)


def _distance_bias(rel_bias, n):
    onehot = np.zeros((n, N_BUCKETS), np.float32)
    onehot[np.arange(n), _t5_bucket(np.arange(n))] = 1.0
    return jnp.dot(jnp.asarray(onehot), rel_bias.astype(F32), precision=HIGHEST).T


def _att_sample_kernel(q_ref, kn_ref, vn_ref, ck_ref, cv_ref, bc_ref, bn_ref, o_ref, ko_ref, vo_ref, *, S, WBUF):
    lane = lax.broadcasted_iota(jnp.int32, (S, LANES), 1)
    first = lane < HD_B
    for hp in range(H_B // 2):
        cs = slice(hp * LANES, (hp + 1) * LANES)
        kc = ck_ref[:, cs].astype(BF16)
        vc = cv_ref[:, cs].astype(BF16)
        kn = kn_ref[:, cs].astype(BF16)
        vn = vn_ref[:, cs].astype(BF16)
        qp = q_ref[:, cs]
        res = []
        for hh in range(2):
            h = 2 * hp + hh
            qh = jnp.where(first if hh == 0 else ~first, qp, 0.0).astype(BF16)
            lc = _nt_dot(qh, kc) + bc_ref[h]
            ln = _nt_dot(qh, kn) + bn_ref[h]
            mx = jnp.maximum(jnp.max(lc, axis=1, keepdims=True), jnp.max(ln, axis=1, keepdims=True))
            pc = jnp.exp(lc - mx)
            pn = jnp.exp(ln - mx)
            den = jnp.sum(pc, axis=1, keepdims=True) + jnp.sum(pn, axis=1, keepdims=True)
            num = jnp.dot(pc.astype(BF16), vc, preferred_element_type=F32) \
                + jnp.dot(pn.astype(BF16), vn, preferred_element_type=F32)
            res.append(num / den)
        o_ref[:, cs] = jnp.where(first, res[0], res[1])
    ko_ref[0:WBUF - S, :] = ck_ref[S:WBUF, :]
    ko_ref[WBUF - S:WBUF, :] = kn_ref[...]
    vo_ref[0:WBUF - S, :] = cv_ref[S:WBUF, :]
    vo_ref[WBUF - S:WBUF, :] = vn_ref[...]


def _att_sample(q, kn, vn, ck, cv, bias_c, bias_n, B, S, WBUF):
    tokb = pl.BlockSpec((S, W_B), lambda b: (b, 0))
    cache = pl.BlockSpec((None, WBUF, W_B), lambda b: (b, 0, 0))
    kern = functools.partial(_att_sample_kernel, S=S, WBUF=WBUF)
    return pl.pallas_call(
        kern,
        grid=(B,),
        in_specs=[tokb, tokb, tokb, cache, cache,
                  pl.BlockSpec((H_B, S, WBUF), lambda b: (0, 0, 0)), pl.BlockSpec((H_B, S, S), lambda b: (0, 0, 0))],
        out_specs=[tokb, cache, cache],
        out_shape=[jax.ShapeDtypeStruct((B * S, W_B), F32), jax.ShapeDtypeStruct((B, WBUF, W_B), F32),
                   jax.ShapeDtypeStruct((B, WBUF, W_B), F32)],
        compiler_params=_cparams(("arbitrary",)),
        name="att_sample",
    )(q, kn, vn, ck, cv, bias_c, bias_n)


def _sample_bias(rel_bias, S, WBUF):
    dist = np.arange(WBUF + S)
    mult = np.zeros(WBUF + S, np.int64)
    for w, dil in DIL_PATTERNS:
        mult += ((dist % dil == 0) & (dist <= w)).astype(np.int64)
    logm = np.where(mult > 0, np.log(np.maximum(mult, 1)), 0.0).astype(np.float32)
    tab = _distance_bias(rel_bias, WBUF + S) + logm[None, :]
    tab = jnp.where((mult > 0)[None, :], tab, NEG)
    rev = jnp.concatenate([tab[:, ::-1], jnp.full((H_B, S), NEG, F32)], axis=1)
    last = WBUF + S - 1
    bias_c = jnp.stack([rev[:, S - 1 - s:S - 1 - s + WBUF] for s in range(S)], axis=1)
    bias_n = jnp.stack([rev[:, last - s:last - s + S] for s in range(S)], axis=1)
    return bias_c, bias_n


def _out_kernel(ha_ref, mo_ref, att_ref, x_ref, g1_ref, sc_ref, sh_ref, mg_ref, ag_ref, g2_ref, bd_ref, wo_ref,
                y_ref, h2_ref):
    ha = ha_ref[...]
    parts = []
    for hd in range(H_A):
        a = ha[:, hd * DK_A:(hd + 1) * DK_A]
        parts.append(a * lax.rsqrt(jnp.mean(a * a, axis=-1, keepdims=True) + EPS))
    hn = jnp.concatenate(parts, axis=1) * mg_ref[...] * jax.nn.sigmoid(mo_ref[...])
    att = att_ref[...]
    an = att * lax.rsqrt(_group_mean_sq(att, bd_ref[...]) + EPS) * ag_ref[...]
    mix = jnp.dot(hn.astype(BF16), wo_ref[0:W_A, :], preferred_element_type=F32) \
        + jnp.dot(an.astype(BF16), wo_ref[W_A:W_A + W_B, :], preferred_element_type=F32)
    y = x_ref[...] + g1_ref[...] * mix
    y_ref[...] = y
    h2 = y * lax.rsqrt(jnp.mean(y * y, axis=-1, keepdims=True) + EPS) * g2_ref[...]
    h2_ref[...] = h2 * (1.0 + sc_ref[...]) + sh_ref[...]


def _out_proj(ha, mo, att, x2, gate1, scale2, shift2, per_token_mod, toks_per_seq, tm, mg, ag, g2, bd, wo):
    n = x2.shape[0]
    if per_token_mod:
        mod_spec = pl.BlockSpec((tm, D_MODEL), lambda i: (i, 0))
    else:
        tiles_per_seq = toks_per_seq // tm
        mod_spec = pl.BlockSpec((None, 1, D_MODEL), lambda i: (i // tiles_per_seq, 0, 0))
    const = lambda shape: pl.BlockSpec(shape, lambda i: (0,) * len(shape), pipeline_mode=pl.Buffered(1))
    tok = lambda w: pl.BlockSpec((tm, w), lambda i: (i, 0))
    return pl.pallas_call(
        _out_kernel,
        grid=(n // tm,),
        in_specs=[tok(W_A), tok(W_A), tok(W_B), tok(D_MODEL), mod_spec, mod_spec, mod_spec,
                  const((1, W_A)), const((1, W_B)), const((1, D_MODEL)), const((W_B, W_B)),
                  const((W_A + W_B, D_MODEL))],
        out_specs=[tok(D_MODEL), tok(D_MODEL)],
        out_shape=[jax.ShapeDtypeStruct((n, D_MODEL), F32)] * 2,
        compiler_params=_cparams(("arbitrary",)),
        name="out_proj",
    )(ha, mo, att, x2, gate1, scale2, shift2, mg, ag, g2, bd, wo)


def _top16(s, nrows):
    iota = lax.broadcasted_iota(jnp.int32, s.shape, 0).astype(F32)
    vals, idxs = [], []
    for _ in range(PEER_TOPK):
        m = jnp.max(s, axis=0, keepdims=True)
        pos = jnp.min(jnp.where(s == m, iota, float(nrows)), axis=0, keepdims=True)
        vals.append(m)
        idxs.append(pos)
        s = jnp.where(iota == pos, -jnp.inf, s)
    return jnp.concatenate(vals, axis=0), jnp.concatenate(idxs, axis=0)


def _peer_route_kernel(h2_ref, wq_ref, keys_ref, eidx_ref, gate_ref, sv_s, si_s, gt_s, et_s, *, tm):
    qh = jnp.dot(h2_ref[...].astype(BF16), wq_ref[...], preferred_element_type=F32)
    for hp in range(2 * PEER_HEADS):
        qs = qh[:, hp * N_SUBKEYS:(hp + 1) * N_SUBKEYS]
        s = _nt_dot(keys_ref[hp % 2], qs, precision=HIGHEST)
        v, i = _top16(s, N_SUBKEYS)
        sv_s[hp] = v
        si_s[hp] = i

    k = PEER_TOPK
    half = k // 2
    sub = lax.broadcasted_iota(jnp.int32, (half, tm), 0).astype(F32)
    pos = jnp.concatenate(
        [lax.broadcasted_iota(jnp.int32, (k, tm), 0).astype(F32)]
        + [float(a * k) + sub for a in range(1, half)] + [(sub + float(half)) * float(k)], axis=0)
    n_cand = float(k * k)

    def head(h, carry):
        sv0 = sv_s[2 * h]
        sv1 = sv_s[2 * h + 1]
        si0 = si_s[2 * h] * float(N_SUBKEYS)
        si1 = si_s[2 * h + 1]
        cands = [sv0[0:1, :] + sv1]
        cidxs = [si0[0:1, :] + si1]
        for a in range(1, half):
            c = sv0[a:a + 1, :] + sv1[0:half, :]
            nb = k // (a + 1)
            cands.append(c if nb >= half else jnp.where(sub < float(nb), c, -jnp.inf))
            cidxs.append(si0[a:a + 1, :] + si1[0:half, :])
        cands.append(sv0[half:k, :] + sv1[0:1, :])
        cidxs.append(si0[half:k, :] + si1[0:1, :])
        cand = jnp.concatenate(cands, axis=0)
        cidx = jnp.concatenate(cidxs, axis=0)
        fv, ev = [], []
        for _ in range(k):
            m = jnp.max(cand, axis=0, keepdims=True)
            first = jnp.min(jnp.where(cand == m, pos, n_cand), axis=0, keepdims=True)
            sel = pos == first
            ev.append(jnp.max(jnp.where(sel, cidx, -1.0), axis=0, keepdims=True))
            fv.append(m)
            cand = jnp.where(sel, -jnp.inf, cand)
        fvs = jnp.concatenate(fv, axis=0)
        e = jnp.exp(fvs - fv[0])
        rs = pl.ds(pl.multiple_of(h * k, k), k)
        gt_s[rs, :] = e / jnp.sum(e, axis=0, keepdims=True)
        et_s[rs, :] = jnp.concatenate(ev, axis=0)
        return carry

    lax.fori_loop(0, PEER_HEADS, head, 0)
    gate_ref[...] = gt_s[...]
    for j in range(tm // LANES):
        cs = slice(j * LANES, (j + 1) * LANES)
        eidx_ref[cs, :] = et_s[:, cs].T.astype(jnp.int32)


def _peer_route(h2, wq, keys, tm):
    n = h2.shape[0]
    kern = functools.partial(_peer_route_kernel, tm=tm)
    return pl.pallas_call(
        kern,
        grid=(n // tm,),
        in_specs=[pl.BlockSpec((tm, D_MODEL), lambda i: (i, 0)),
                  pl.BlockSpec((D_MODEL, 2 * PEER_HEADS * N_SUBKEYS), lambda i: (0, 0)),
                  pl.BlockSpec((2, N_SUBKEYS, N_SUBKEYS), lambda i: (0, 0, 0))],
        out_specs=[pl.BlockSpec((tm, PEER_PAIRS), lambda i: (i, 0)), pl.BlockSpec((PEER_PAIRS, tm), lambda i: (0, i))],
        out_shape=[jax.ShapeDtypeStruct((n, PEER_PAIRS), jnp.int32), jax.ShapeDtypeStruct((PEER_PAIRS, n), F32)],
        scratch_shapes=[pltpu.VMEM((2 * PEER_HEADS, PEER_TOPK, tm), F32)] * 2 + [pltpu.VMEM((PEER_PAIRS, tm), F32)] * 2,
        compiler_params=_cparams(("arbitrary",)),
        name="peer_route",
    )(h2, wq, keys)


ROW_CHUNKS = D_MODEL // LANES


def _pack_table(tab):
    return tab.astype(BF16).reshape(tab.shape[0], ROW_CHUNKS, LANES)


def _gather_rows(idx_ref, t, tab_ref, g_s):
    for p in range(PEER_PAIRS):
        g_s[ROW_CHUNKS * p:ROW_CHUNKS * (p + 1), :] = tab_ref[idx_ref[t, p]]


def _pair_rows(g_s, j):
    both = g_s[2 * ROW_CHUNKS * j:2 * ROW_CHUNKS * (j + 1), :].astype(F32)
    return both[0:ROW_CHUNKS, :], both[ROW_CHUNKS:2 * ROW_CHUNKS, :]


def _token_loop(tb, stage, compute, tokens_per_step=2):
    stage(0, 0)

    def step(i, carry):
        t0 = tokens_per_step * i
        for u in range(tokens_per_step):
            stage(jnp.minimum(t0 + u + 1, tb - 1), (u + 1) % 2)
            compute(t0 + u, u % 2)
        return carry

    lax.fori_loop(0, tb // tokens_per_step, step, 0)


def _fold_rows(a, sub):
    t = [x + pltpu.roll(x, 4, axis=0) for x in a]
    b = [jnp.where(sub < 4, t[j], t[j + 4]) for j in range(4)]
    c = [jnp.where(sub % 4 < 2, b[j] + pltpu.roll(b[j], 6, axis=0), b[j + 2] + pltpu.roll(b[j + 2], 2, axis=0))
         for j in range(2)]
    return jnp.where(sub % 2 == 0, c[0] + pltpu.roll(c[0], 7, axis=0), c[1] + pltpu.roll(c[1], 1, axis=0))


def _peer_act_kernel(idx_ref, x_ref, gate_ref, tab_ref, w_ref, g0_s, g1_s, m_s, a_s, *, tb):
    sub = lax.broadcasted_iota(jnp.int32, (ROW_CHUNKS, LANES), 0)
    lane = lax.broadcasted_iota(jnp.int32, (PEER_PAIRS, tb), 1)
    g_s = (g0_s, g1_s)
    a_s[...] = jnp.zeros((PEER_PAIRS, tb), F32)

    def compute(t, slot):
        row = x_ref[pl.ds(t, 1), :]
        x8 = jnp.concatenate([row[:, r * LANES:(r + 1) * LANES] for r in range(ROW_CHUNKS)], axis=0)
        for blk in range(PEER_PAIRS // 8):
            prods = []
            for j in range(4):
                u0, u1 = _pair_rows(g_s[slot], 4 * blk + j)
                prods += [u0 * x8, u1 * x8]
            m_s[8 * blk:8 * (blk + 1), :] = _fold_rows(prods, sub)
        dots = jnp.sum(m_s[...], axis=1, keepdims=True)
        a_s[...] = jnp.where(lane == t, dots, a_s[...])

    _token_loop(tb, lambda t, slot: _gather_rows(idx_ref, t, tab_ref, g_s[slot]), compute)
    act = a_s[...]
    gelu = 0.5 * act * (1.0 + lax.erf(act * (2.0 ** -0.5)))
    w_ref[...] = gate_ref[...] * gelu


def _table_spec(tab):
    return pl.BlockSpec(tab.shape, lambda i: (0, 0, 0), pipeline_mode=pl.Buffered(1))


def _peer_act(eidx, x3, gate, tab, tb):
    n = x3.shape[0]
    kern = functools.partial(_peer_act_kernel, tb=tb)
    return pl.pallas_call(
        kern,
        grid=(n // tb,),
        in_specs=[pl.BlockSpec((tb, PEER_PAIRS), lambda i: (i, 0), memory_space=pltpu.SMEM),
                  pl.BlockSpec((tb, D_MODEL), lambda i: (i, 0)),
                  pl.BlockSpec((PEER_PAIRS, tb), lambda i: (0, i)),
                  _table_spec(tab)],
        out_specs=pl.BlockSpec((PEER_PAIRS, tb), lambda i: (0, i)),
        out_shape=jax.ShapeDtypeStruct((PEER_PAIRS, n), F32),
        scratch_shapes=[pltpu.VMEM((PEER_PAIRS * ROW_CHUNKS, LANES), BF16)] * 2 + [
                        pltpu.VMEM((PEER_PAIRS, LANES), F32), pltpu.VMEM((PEER_PAIRS, tb), F32)],
        compiler_params=_cparams(("arbitrary",)),
        name="peer_act",
    )(eidx, x3, gate, tab)


def _peer_mix_kernel(idx_ref, w_ref, y_ref, g2_ref, tab_ref, o_ref, g0_s, g1_s, wb0_s, wb1_s, *, tb, per_token_mod):
    lane = lax.broadcasted_iota(jnp.int32, (PEER_PAIRS, tb), 1)
    g_s = (g0_s, g1_s)
    wb_s = (wb0_s, wb1_s)

    def stage(t, slot):
        _gather_rows(idx_ref, t, tab_ref, g_s[slot])
        col = jnp.sum(jnp.where(lane == t, w_ref[...], 0.0), axis=1, keepdims=True)
        wb_s[slot][...] = jnp.broadcast_to(col, (PEER_PAIRS, LANES))

    def compute(t, slot):
        accs = [jnp.zeros((ROW_CHUNKS, LANES), F32) for _ in range(4)]
        for j in range(PEER_PAIRS // 2):
            v0, v1 = _pair_rows(g_s[slot], j)
            w0 = jnp.broadcast_to(wb_s[slot][2 * j:2 * j + 1, :], (ROW_CHUNKS, LANES))
            w1 = jnp.broadcast_to(wb_s[slot][2 * j + 1:2 * j + 2, :], (ROW_CHUNKS, LANES))
            k = 2 * (j % 2)
            accs[k] = accs[k] + v0 * w0
            accs[k + 1] = accs[k + 1] + v1 * w1
        out = (accs[0] + accs[1]) + (accs[2] + accs[3])
        g2 = g2_ref[pl.ds(t, 1), :] if per_token_mod else g2_ref[...]
        out_row = jnp.concatenate([out[r:r + 1, :] for r in range(ROW_CHUNKS)], axis=1)
        o_ref[pl.ds(t, 1), :] = y_ref[pl.ds(t, 1), :] + g2 * out_row

    _token_loop(tb, stage, compute)


def _peer_mix(eidx, w, y3, gate2, per_token_mod, toks_per_seq, tab, tb):
    n = y3.shape[0]
    if per_token_mod:
        g_spec = pl.BlockSpec((tb, D_MODEL), lambda i: (i, 0))
    else:
        blocks_per_seq = toks_per_seq // tb
        g_spec = pl.BlockSpec((None, 1, D_MODEL), lambda i: (i // blocks_per_seq, 0, 0))
    kern = functools.partial(_peer_mix_kernel, tb=tb, per_token_mod=per_token_mod)
    return pl.pallas_call(
        kern,
        grid=(n // tb,),
        in_specs=[pl.BlockSpec((tb, PEER_PAIRS), lambda i: (i, 0), memory_space=pltpu.SMEM),
                  pl.BlockSpec((PEER_PAIRS, tb), lambda i: (0, i)),
                  pl.BlockSpec((tb, D_MODEL), lambda i: (i, 0)), g_spec, _table_spec(tab)],
        out_specs=pl.BlockSpec((tb, D_MODEL), lambda i: (i, 0)),
        out_shape=jax.ShapeDtypeStruct((n, D_MODEL), F32),
        scratch_shapes=[pltpu.VMEM((PEER_PAIRS * ROW_CHUNKS, LANES), BF16)] * 2 + [
                        pltpu.VMEM((PEER_PAIRS, LANES), F32)] * 2,
        compiler_params=_cparams(("arbitrary",)),
        name="peer_mix",
    )(eidx, w, y3, gate2, tab)


def _layer(x, mod, conv_buf, C0, n0, m0, k_buf, v_buf, rel_bias, wts):
    B, T, _ = x.shape
    n = B * T
    tm = min(PROJ_TILE, n)
    tb = PEER_BLOCK
    x2 = x.reshape(n, D_MODEL)
    shift1, scale1, gate1, shift2, scale2, gate2 = jnp.split(mod, 6, axis=-1)
    per_token = T % tm != 0
    if per_token:
        expand = lambda a: jnp.repeat(a, T, axis=0)
    else:
        expand = lambda a: a.reshape(B, 1, D_MODEL)

    q, k, v, mqk, mv, mo, gcol, grow = _in_proj(
        x2, expand(scale1), expand(shift1), per_token, T, tm, wts["g1"], wts["wm"], wts["wgc"], wts["wgr"],
        wts["bd"], wts["qg"], wts["kg"])

    if k_buf is None:
        att = _att_prompt(q, k, v, _prompt_bias(rel_bias), B, T)
        k_new = k.reshape(B, T, H_B, HD_B)
        v_new = v.reshape(B, T, H_B, HD_B)
    else:
        wbuf = k_buf.shape[1]
        bias_c, bias_n = _sample_bias(rel_bias, T, wbuf)
        att, k_new, v_new = _att_sample(q, k, v, k_buf.reshape(B, wbuf, W_B), v_buf.reshape(B, wbuf, W_B),
                                        bias_c, bias_n, B, T, wbuf)
        k_new = k_new.reshape(B, wbuf, H_B, HD_B)
        v_new = v_new.reshape(B, wbuf, H_B, HD_B)

    ha, C, nn, m = _mlstm(mqk, mv, gcol, grow, wts["conv_w"], wts["conv_b"], conv_buf, wts["b_gate"], C0, n0, m0, B, T)
    mqk3 = mqk.reshape(B, T, 2 * W_A)
    if T >= CONV_W - 1:
        conv_new = mqk3[:, T - (CONV_W - 1):]
    else:
        conv_new = jnp.concatenate([conv_buf, mqk3], axis=1)[:, -(CONV_W - 1):]

    y1, h2 = _out_proj(ha, mo, att, x2, expand(gate1), expand(scale2), expand(shift2), per_token, T, tm,
                       wts["mg"], wts["ag"], wts["g2"], wts["bd"], wts["wo"])

    eidx, gate = _peer_route(h2, wts["wq"], wts["keys"], min(ROUTE_TILE, n))
    w = _peer_act(eidx, h2, gate, wts["u_tab"], tb)
    y = _peer_mix(eidx, w, y1, expand(gate2), per_token, T, wts["v_tab"], tb)
    return (y.reshape(B, T, D_MODEL), k_new, v_new, conv_new, C, nn.reshape(B, H_A, DK_A), m[:, :, 0, 0])


def _prep_weights(l, norm1_g, norm2_g, w_in, b_gate, conv_w, conv_b, q_norm_g, k_norm_g, att_out_g, mlstm_out_g,
                  w_out, peer_wq, peer_keys, peer_u, peer_v):
    w = w_in[l]
    wg = w[:, W_MAIN:]
    grp = np.arange(W_B) // HD_B
    bd = jnp.asarray((grp[:, None] == grp[None, :]).astype(np.float32) / HD_B, BF16)
    return dict(
        g1=norm1_g[l].reshape(1, -1), g2=norm2_g[l].reshape(1, -1),
        wm=w[:, :W_MAIN].astype(BF16),
        wgc=jnp.pad(wg, ((0, 0), (0, LANES - 2 * H_A))), wgr=wg.T,
        bd=bd, qg=jnp.tile(q_norm_g[l], H_B).reshape(1, -1), kg=jnp.tile(k_norm_g[l], H_B).reshape(1, -1),
        conv_w=conv_w[l], conv_b=conv_b[l], b_gate=b_gate[l],
        mg=mlstm_out_g[l].reshape(1, -1), ag=att_out_g[l].reshape(1, -1),
        wo=w_out[l].astype(BF16), wq=peer_wq[l].astype(BF16), keys=peer_keys[l],
        u_tab=_pack_table(peer_u[l]), v_tab=_pack_table(peer_v[l]),
    )


def kernel(x_prompt, x_sample, cache_k, cache_v, state_conv, state_C, state_n, state_m, c_prompt, c_sample,
           rel_bias, w_ada, b_ada, norm1_g, norm2_g, w_in, b_gate, conv_w, conv_b, q_norm_g, k_norm_g,
           att_out_g, mlstm_out_g, w_out, peer_wq, peer_keys, peer_u, peer_v):
    depth = w_ada.shape[0]
    bp = x_prompt.shape[0]
    bs = x_sample.shape[0]
    yp, ys = x_prompt, x_sample
    sp, ss = [], []
    for l in range(depth):
        wts = _prep_weights(l, norm1_g, norm2_g, w_in, b_gate, conv_w, conv_b, q_norm_g, k_norm_g, att_out_g,
                            mlstm_out_g, w_out, peer_wq, peer_keys, peer_u, peer_v)
        mod = _ada(jnp.concatenate([c_prompt, c_sample], axis=0), w_ada[l], b_ada[l])
        zc = jnp.zeros((bp, CONV_W - 1, 2 * W_A), F32)
        zC = jnp.zeros((bp, H_A, DK_A, DK_A), F32)
        zn = jnp.zeros((bp, H_A, DK_A), F32)
        zm = jnp.zeros((bp, H_A), F32)
        outp = _layer(yp, mod[:bp], zc, zC, zn, zm, None, None, rel_bias, wts)
        outs = _layer(ys, mod[bp:], state_conv[l], state_C[l], state_n[l], state_m[l], cache_k[l], cache_v[l],
                      rel_bias, wts)
        yp, ys = outp[0], outs[0]
        sp.append(outp[1:])
        ss.append(outs[1:])
    k_p, v_p, conv_p, C_p, n_p, m_p = [jnp.stack([s[i] for s in sp], axis=0) for i in range(6)]
    k_s, v_s, conv_s, C_s, n_s, m_s = [jnp.stack([s[i] for s in ss], axis=0) for i in range(6)]
    return (yp, ys, k_p, v_p, conv_p, C_p, n_p, m_p, k_s, v_s, conv_s, C_s, n_s, m_s)
```

```python
import functools
import math

import numpy as np
import jax
import jax.numpy as jnp
from jax import lax
from jax.experimental import pallas as pl
from jax.experimental.pallas import tpu as pltpu

F32 = jnp.float32
BF16 = jnp.bfloat16
HIGHEST = lax.Precision.HIGHEST

D_MODEL = 1024
H_A = 4
DK_A = 128
W_A = H_A * DK_A
H_B = 8
HD_B = 64
W_B = H_B * HD_B
CONV_W = 4
MLSTM_CHUNK = 64
MLSTM_HP = 4
DIL_PATTERNS = ((128, 1), (512, 4), (2048, 16))
ATT_STEPS = 128
ATT_UNROLL = 8
N_BUCKETS = 32
MAX_DIST = 2048
PEER_HEADS = 8
N_SUBKEYS = 128
PEER_TOPK = 16
PEER_PAIRS = PEER_HEADS * PEER_TOPK
EPS = 1e-6
NEG = -1e30
W_MAIN = 3 * W_B + 4 * W_A
LANES = 128
PROJ_TILE = 512
ROUTE_TILE = 256
PEER_BLOCK = LANES
VMEM_LIMIT = 56 * 1024 * 1024


def _cparams(sem):
    return pltpu.CompilerParams(dimension_semantics=sem, vmem_limit_bytes=VMEM_LIMIT)


def _nt_dot(a, b, precision=None):
    return lax.dot_general(a, b, (((1,), (1,)), ((), ())), preferred_element_type=F32, precision=precision)


def _t5_bucket(dist):
    max_exact = N_BUCKETS // 2
    d = np.maximum(dist, 1).astype(np.float32)
    large = max_exact + (np.log(d / max_exact) / math.log(MAX_DIST / max_exact) * (N_BUCKETS - max_exact)).astype(np.int32)
    large = np.minimum(large, N_BUCKETS - 1)
    return np.where(dist < max_exact, dist, large).astype(np.int32)


def _ada_kernel(c_ref, w_ref, b_ref, o_ref):
    c = c_ref[...]
    s = c * jax.nn.sigmoid(c)
    o_ref[...] = jnp.dot(s, w_ref[...], preferred_element_type=F32, precision=HIGHEST) + b_ref[...]


def _ada(c_all, w_ada, b_ada):
    n = c_all.shape[0]
    return pl.pallas_call(
        _ada_kernel,
        grid=(6,),
        in_specs=[pl.BlockSpec((n, D_MODEL), lambda j: (0, 0)),
                  pl.BlockSpec((D_MODEL, D_MODEL), lambda j: (0, j)),
                  pl.BlockSpec((1, D_MODEL), lambda j: (0, j))],
        out_specs=pl.BlockSpec((n, D_MODEL), lambda j: (0, j)),
        out_shape=jax.ShapeDtypeStruct((n, 6 * D_MODEL), F32),
        compiler_params=_cparams(("arbitrary",)),
        name="ada",
    )(c_all, w_ada, b_ada.reshape(1, -1))


def _group_mean_sq(a, bd):
    sq = a * a
    hi = sq.astype(BF16)
    lo = (sq - hi.astype(F32)).astype(BF16)
    return jnp.dot(hi, bd, preferred_element_type=F32) + jnp.dot(lo, bd, preferred_element_type=F32)


def _in_kernel(x_ref, sc_ref, sh_ref, g1_ref, wm_ref, wgc_ref, wgr_ref, bd_ref, qg_ref, kg_ref,
               q_ref, k_ref, v_ref, mqk_ref, mv_ref, mo_ref, gcol_ref, grow_ref):
    x = x_ref[...]
    ms = jnp.mean(x * x, axis=-1, keepdims=True)
    h = x * lax.rsqrt(ms + EPS) * g1_ref[...]
    h = h * (1.0 + sc_ref[...]) + sh_ref[...]
    y = jnp.dot(h.astype(BF16), wm_ref[...], preferred_element_type=F32)
    bd = bd_ref[...]
    aq = y[:, 0:W_B]
    ak = y[:, W_B:2 * W_B]
    q_ref[...] = aq * lax.rsqrt(_group_mean_sq(aq, bd) + EPS) * qg_ref[...] * (HD_B ** -0.5)
    k_ref[...] = ak * lax.rsqrt(_group_mean_sq(ak, bd) + EPS) * kg_ref[...]
    v_ref[...] = y[:, 2 * W_B:3 * W_B]
    o = 3 * W_B
    mqk_ref[...] = y[:, o:o + 2 * W_A]
    mv_ref[...] = y[:, o + 2 * W_A:o + 3 * W_A]
    mo_ref[...] = y[:, o + 3 * W_A:o + 4 * W_A]
    gcol_ref[...] = jnp.dot(h, wgc_ref[...], preferred_element_type=F32, precision=HIGHEST)
    grow_ref[...] = _nt_dot(wgr_ref[...], h, precision=HIGHEST)


def _in_proj(x2, scale, shift, per_token_mod, toks_per_seq, tm, g1, wm, wgc, wgr, bd, qg, kg):
    n = x2.shape[0]
    nt = n // tm
    if per_token_mod:
        mod_spec = pl.BlockSpec((tm, D_MODEL), lambda i: (i, 0))
    else:
        tiles_per_seq = toks_per_seq // tm
        mod_spec = pl.BlockSpec((None, 1, D_MODEL), lambda i: (i // tiles_per_seq, 0, 0))
    const = lambda shape: pl.BlockSpec(shape, lambda i: (0,) * len(shape), pipeline_mode=pl.Buffered(1))
    tok = lambda w: pl.BlockSpec((tm, w), lambda i: (i, 0))
    outs = pl.pallas_call(
        _in_kernel,
        grid=(nt,),
        in_specs=[tok(D_MODEL), mod_spec, mod_spec, const((1, D_MODEL)), const((D_MODEL, W_MAIN)),
                  const((D_MODEL, LANES)), const((8, D_MODEL)), const((W_B, W_B)), const((1, W_B)), const((1, W_B))],
        out_specs=[tok(W_B), tok(W_B), tok(W_B), tok(2 * W_A), tok(W_A), tok(W_A), tok(LANES),
                   pl.BlockSpec((8, tm), lambda i: (0, i))],
        out_shape=[jax.ShapeDtypeStruct((n, W_B), F32)] * 3 + [jax.ShapeDtypeStruct((n, 2 * W_A), F32)]
                  + [jax.ShapeDtypeStruct((n, W_A), F32)] * 2 + [jax.ShapeDtypeStruct((n, LANES), F32),
                                                                 jax.ShapeDtypeStruct((8, n), F32)],
        compiler_params=_cparams(("arbitrary",)),
        name="in_proj",
    )(x2, scale, shift, g1, wm, wgc, wgr, bd, qg, kg)
    return outs


def _log_sigmoid(x):
    return jnp.minimum(x, 0.0) - jnp.log1p(jnp.exp(-jnp.abs(x)))


def _mlstm_kernel(bg_ref, m0_ref, mq_ref, mk_ref, cwq_ref, cwk_ref, cbq_ref, cbk_ref, bufq_ref, bufk_ref,
                  v_ref, gcol_ref, grow_ref, c0_ref, n0_ref,
                  h_ref, c_ref, n_ref, m_ref, sq, sk, *, T, L, rows_whole):
    b = pl.program_id(0)
    hg = pl.program_id(1)
    nc = T // L

    def conv(u_ref, buf_ref, w_ref, cb_ref, s_ref, out_ref):
        s_ref[0:8, :] = jnp.zeros((8, MLSTM_HP * LANES), F32)
        s_ref[5:8, :] = buf_ref[...]
        s_ref[8:8 + T, :] = u_ref[...]
        y = cb_ref[...]
        for j in range(CONV_W):
            y = y + s_ref[5 + j:5 + j + T, :] * w_ref[j:j + 1, :]
        out_ref[...] = y * jax.nn.sigmoid(y)

    conv(mq_ref, bufq_ref, cwq_ref, cbq_ref, sq, h_ref)
    conv(mk_ref, bufk_ref, cwk_ref, cbk_ref, sk, sq.at[8:8 + T, :])

    lane = lax.broadcasted_iota(jnp.int32, (L, LANES), 1)
    ri = lax.broadcasted_iota(jnp.int32, (L, L), 0)
    ci = lax.broadcasted_iota(jnp.int32, (L, L), 1)
    causal = ri >= ci
    tri = causal.astype(F32)
    tri_t = (ri <= ci).astype(F32)

    def chunk(c, carry):
        r0 = pl.multiple_of(c * L, L)
        rc = c + (b * nc if rows_whole else 0)
        g = gcol_ref[pl.ds(r0, L), :]
        st = []
        for j in range(MLSTM_HP):
            C, n, m = carry[j]
            hd = hg * MLSTM_HP + j
            ig_b = bg_ref[hd]
            f_b = bg_ref[H_A + hd]
            cs = slice(j * LANES, (j + 1) * LANES)
            qc = h_ref[pl.ds(r0, L), cs]
            kc = sq[pl.ds(r0 + 8, L), cs] * (DK_A ** -0.5)
            vb = v_ref[pl.ds(r0, L), cs].astype(BF16)
            ig_col = jnp.sum(jnp.where(lane == hd, g, 0.0), axis=1, keepdims=True) + ig_b
            f_col = jnp.sum(jnp.where(lane == hd + H_A, g, 0.0), axis=1, keepdims=True) + f_b
            ig_row = grow_ref[hd, pl.ds(rc, 1), :] + ig_b
            lf_row = _log_sigmoid(grow_ref[hd + H_A, pl.ds(rc, 1), :] + f_b)
            b_col = jnp.dot(tri, jnp.broadcast_to(_log_sigmoid(f_col), (L, LANES)), preferred_element_type=F32,
                            precision=HIGHEST)[:, 0:1]
            b_row = jnp.dot(jnp.broadcast_to(lf_row, (8, L)), tri_t, preferred_element_type=F32,
                            precision=HIGHEST)[0:1, :]
            qb = qc.astype(BF16)
            qk = _nt_dot(qb, kc.astype(BF16))
            qC = jnp.dot(qb, C.astype(BF16), preferred_element_type=F32)
            st.append(dict(C=C, n=n, m=m, cs=cs, qc=qc, kc=kc, vb=vb, ig_col=ig_col, ig_row=ig_row,
                           b_col=b_col, b_row=b_row, qk=qk, qC=qC))
        for d in st:
            b_col, b_row, m = d["b_col"], d["b_row"], d["m"]
            g_col = b_col + m
            dm = jnp.where(causal, b_col - b_row + d["ig_row"], -jnp.inf)
            mt = jnp.maximum(g_col, jnp.max(dm, axis=1, keepdims=True))
            s = d["qk"] * jnp.exp(dm - mt)
            wg = jnp.exp(g_col - mt)
            bl = b_col[L - 1:L, :]
            m_new = jnp.maximum(bl + m, jnp.max(bl - b_row + d["ig_row"], axis=1, keepdims=True))
            kw = jnp.exp(bl - b_col + d["ig_col"] - m_new) * d["kc"]
            d.update(mt=mt, s=s, wg=wg, m_new=m_new, kw=kw, wc=jnp.exp(bl + m - m_new))
        out = []
        for d in st:
            s, wg = d["s"], d["wg"]
            num = wg * d["qC"] + jnp.dot(s.astype(BF16), d["vb"], preferred_element_type=F32)
            den = wg * jnp.sum(d["qc"] * d["n"], axis=1, keepdims=True) + jnp.sum(s, axis=1, keepdims=True)
            h_ref[pl.ds(r0, L), d["cs"]] = num / jnp.maximum(jnp.abs(den), jnp.exp(-d["mt"]))
            upd = lax.dot_general(d["kw"].astype(BF16), d["vb"], (((0,), (0,)), ((), ())), preferred_element_type=F32)
            out.append((d["wc"] * d["C"] + upd, d["wc"] * d["n"] + jnp.sum(d["kw"], axis=0, keepdims=True), d["m_new"]))
        return tuple(out)

    init = tuple((c0_ref[j], n0_ref[j], jnp.full((1, 1), m0_ref[b, hg * MLSTM_HP + j], F32))
                 for j in range(MLSTM_HP))
    final = lax.fori_loop(0, nc, chunk, init)
    for j in range(MLSTM_HP):
        c_ref[j] = final[j][0]
        n_ref[j] = final[j][1]
        m_ref[j] = jnp.broadcast_to(final[j][2], (1, LANES))


def _mlstm(mqk, mv, gcol, grow, conv_w, conv_b, conv_buf, b_gate, C0, n0, m0, B, T):
    L = math.gcd(T, MLSTM_CHUNK)
    nc = T // L
    n = B * T
    rows_whole = nc % 8 != 0
    grow3 = grow.reshape(8, n // L, L)
    if rows_whole:
        grow_spec = pl.BlockSpec((8, n // L, L), lambda b, h: (0, 0, 0))
    else:
        grow_spec = pl.BlockSpec((8, nc, L), lambda b, h: (0, b, 0))
    smem = pl.BlockSpec(memory_space=pltpu.SMEM)
    wide = MLSTM_HP * LANES
    ng = H_A // MLSTM_HP
    colq = lambda rows: pl.BlockSpec((rows, wide), lambda b, h: (0, h))
    colk = lambda rows: pl.BlockSpec((rows, wide), lambda b, h: (0, h + ng))
    st = lambda r, c: pl.BlockSpec((None, MLSTM_HP, r, c), lambda b, h: (b, h, 0, 0))
    kern = functools.partial(_mlstm_kernel, T=T, L=L, rows_whole=rows_whole)
    return pl.pallas_call(
        kern,
        grid=(B, ng),
        in_specs=[smem, smem,
                  pl.BlockSpec((T, wide), lambda b, h: (b, h)), pl.BlockSpec((T, wide), lambda b, h: (b, h + ng)),
                  colq(CONV_W), colk(CONV_W), colq(1), colk(1),
                  pl.BlockSpec((None, CONV_W - 1, wide), lambda b, h: (b, 0, h)),
                  pl.BlockSpec((None, CONV_W - 1, wide), lambda b, h: (b, 0, h + ng)),
                  pl.BlockSpec((T, wide), lambda b, h: (b, h)),
                  pl.BlockSpec((T, LANES), lambda b, h: (b, 0)),
                  grow_spec, st(DK_A, DK_A), st(1, DK_A)],
        out_specs=[pl.BlockSpec((T, wide), lambda b, h: (b, h)), st(DK_A, DK_A), st(1, DK_A), st(1, LANES)],
        out_shape=[jax.ShapeDtypeStruct((n, W_A), F32), jax.ShapeDtypeStruct((B, H_A, DK_A, DK_A), F32),
                   jax.ShapeDtypeStruct((B, H_A, 1, DK_A), F32), jax.ShapeDtypeStruct((B, H_A, 1, LANES), F32)],
        scratch_shapes=[pltpu.VMEM((T + 8, wide), F32), pltpu.VMEM((T + 8, wide), F32)],
        compiler_params=_cparams(("arbitrary", "arbitrary")),
        name="mlstm",
    )(b_gate, m0, mqk, mqk, conv_w, conv_w, conv_b.reshape(1, -1), conv_b.reshape(1, -1), conv_buf, conv_buf,
      mv, gcol, grow3, C0, n0.reshape(B, H_A, 1, DK_A))


def _att_prompt_kernel(q_ref, k_ref, v_ref, bias_ref, o_ref, num_s, mx_s, dn_s, mxs_s, *, T):
    lane = lax.broadcasted_iota(jnp.int32, (ATT_STEPS, LANES), 1)
    first = lane < HD_B
    nblk = T // ATT_STEPS

    for br, (_, dil) in enumerate(DIL_PATTERNS):
        span = ATT_STEPS * dil
        has_prev = T > span

        def rows(start, dil=dil):
            return pl.ds(start, ATT_STEPS, stride=dil) if dil > 1 else pl.ds(start, ATT_STEPS)

        def blk(i0, carry, br=br, dil=dil, span=span, has_prev=has_prev, rows=rows):
            units = []
            for u in range(ATT_UNROLL):
                i = i0 + u * (nblk // ATT_UNROLL)
                nb = i // dil
                start = nb * span + i % dil
                qb = q_ref[rows(start), :]
                kcur = k_ref[rows(start), :].astype(BF16)
                kprev = k_ref[rows(jnp.maximum(start - span, 0)), :].astype(BF16) if has_prev else None
                for hh in range(2):
                    own = first if hh == 0 else ~first
                    qh = jnp.where(own, qb, 0.0).astype(BF16)
                    lc = _nt_dot(qh, kcur) + bias_ref[br, hh, :, ATT_STEPS:2 * ATT_STEPS]
                    lp = None
                    if has_prev:
                        lp = _nt_dot(qh, kprev) + bias_ref[br, hh, :, 0:ATT_STEPS]
                        lp = jnp.where(nb > 0, lp, NEG)
                    units.append((start, own, lc, lp))
            weights = []
            for start, own, lc, lp in units:
                mx = jnp.max(jnp.maximum(lc, lp) if has_prev else lc, axis=1, keepdims=True)
                pc = jnp.exp(lc - mx).astype(BF16)
                pp = jnp.exp(lp - mx).astype(BF16) if has_prev else None
                weights.append((mx, pc, pp))
            res = []
            for (start, own, _, _), (mx, pc, pp) in zip(units, weights):
                acc = jnp.dot(pc, jnp.where(own, v_ref[rows(start), :], 1.0).astype(BF16), preferred_element_type=F32)
                if has_prev:
                    vprev = v_ref[rows(jnp.maximum(start - span, 0)), :]
                    acc = acc + jnp.dot(pp, jnp.where(own, vprev, 1.0).astype(BF16), preferred_element_type=F32)
                res.append((acc, mx))
            for u in range(ATT_UNROLL):
                start = units[2 * u][0]
                (a0, m0), (a1, m1) = res[2 * u], res[2 * u + 1]
                num_s[br, rows(start), :] = jnp.where(first, a0, a1)
                dn_s[br, rows(start), :] = jnp.where(first, a1, a0)
                mx_s[br, rows(start), :] = jnp.where(first, m0, m1)
                mxs_s[br, rows(start), :] = jnp.where(first, m1, m0)
            return carry

        lax.fori_loop(0, nblk // ATT_UNROLL, blk, 0)

    def comb(i, carry):
        rs = pl.ds(pl.multiple_of(i * ATT_STEPS, ATT_STEPS), ATT_STEPS)
        m = jnp.maximum(jnp.maximum(mx_s[0, rs, :], mx_s[1, rs, :]), mx_s[2, rs, :])
        ms = jnp.maximum(jnp.maximum(mxs_s[0, rs, :], mxs_s[1, rs, :]), mxs_s[2, rs, :])
        num = jnp.zeros((ATT_STEPS, LANES), F32)
        den = jnp.zeros((ATT_STEPS, LANES), F32)
        for br in range(3):
            num = num + jnp.exp(mx_s[br, rs, :] - m) * num_s[br, rs, :]
            den = den + jnp.exp(mxs_s[br, rs, :] - ms) * dn_s[br, rs, :]
        o_ref[rs, :] = num / pltpu.roll(den, HD_B, axis=1)
        return carry

    lax.fori_loop(0, nblk, comb, 0)


def _att_prompt(q, k, v, bias, B, T):
    n = B * T
    blk = pl.BlockSpec((T, LANES), lambda b, p: (b, p))
    kern = functools.partial(_att_prompt_kernel, T=T)
    return pl.pallas_call(
        kern,
        grid=(B, H_B // 2),
        in_specs=[blk, blk, blk, pl.BlockSpec((3, 2, ATT_STEPS, 2 * ATT_STEPS), lambda b, p: (0, p, 0, 0))],
        out_specs=blk,
        out_shape=jax.ShapeDtypeStruct((n, W_B), F32),
        scratch_shapes=[pltpu.VMEM((3, T, LANES), F32)] * 4,
        compiler_params=_cparams(("arbitrary", "arbitrary")),
        name="att_prompt",
    )(q, k, v, bias)


def _prompt_bias(rel_bias):
    tab = _distance_bias(rel_bias, ATT_STEPS * DIL_PATTERNS[-1][1] + 1)
    pad = jnp.full((H_B, ATT_STEPS - 1), NEG, F32)
    period = 3 * ATT_STEPS - 1
    tabs = []
    for _, dil in DIL_PATTERNS:
        w = jnp.concatenate([pad, tab[:, 0:ATT_STEPS * dil + 1:dil], pad], axis=1)[:, ::-1]
        w = jnp.concatenate([w[:, ATT_STEPS - 1:], w[:, :ATT_STEPS - 1]], axis=1)
        flat = jnp.tile(w, (1, ATT_STEPS))[:, :ATT_STEPS * (period - 1)]
        tabs.append(flat.reshape(H_B, ATT_STEPS, period - 1)[:, :, :2 * ATT_STEPS])
    return jnp.stack(tabs, axis=0)


def _distance_bias(rel_bias, n):
    onehot = np.zeros((n, N_BUCKETS), np.float32)
    onehot[np.arange(n), _t5_bucket(np.arange(n))] = 1.0
    return jnp.dot(jnp.asarray(onehot), rel_bias.astype(F32), precision=HIGHEST).T


def _att_sample_kernel(q_ref, kn_ref, vn_ref, ck_ref, cv_ref, bc_ref, bn_ref, o_ref, ko_ref, vo_ref, *, S, WBUF):
    lane = lax.broadcasted_iota(jnp.int32, (S, LANES), 1)
    first = lane < HD_B
    for hp in range(H_B // 2):
        cs = slice(hp * LANES, (hp + 1) * LANES)
        kc = ck_ref[:, cs].astype(BF16)
        vc = cv_ref[:, cs].astype(BF16)
        kn = kn_ref[:, cs].astype(BF16)
        vn = vn_ref[:, cs].astype(BF16)
        qp = q_ref[:, cs]
        res = []
        for hh in range(2):
            h = 2 * hp + hh
            qh = jnp.where(first if hh == 0 else ~first, qp, 0.0).astype(BF16)
            lc = _nt_dot(qh, kc) + bc_ref[h]
            ln = _nt_dot(qh, kn) + bn_ref[h]
            mx = jnp.maximum(jnp.max(lc, axis=1, keepdims=True), jnp.max(ln, axis=1, keepdims=True))
            pc = jnp.exp(lc - mx)
            pn = jnp.exp(ln - mx)
            den = jnp.sum(pc, axis=1, keepdims=True) + jnp.sum(pn, axis=1, keepdims=True)
            num = jnp.dot(pc.astype(BF16), vc, preferred_element_type=F32) \
                + jnp.dot(pn.astype(BF16), vn, preferred_element_type=F32)
            res.append(num / den)
        o_ref[:, cs] = jnp.where(first, res[0], res[1])
    ko_ref[0:WBUF - S, :] = ck_ref[S:WBUF, :]
    ko_ref[WBUF - S:WBUF, :] = kn_ref[...]
    vo_ref[0:WBUF - S, :] = cv_ref[S:WBUF, :]
    vo_ref[WBUF - S:WBUF, :] = vn_ref[...]


def _att_sample(q, kn, vn, ck, cv, bias_c, bias_n, B, S, WBUF):
    tokb = pl.BlockSpec((S, W_B), lambda b: (b, 0))
    cache = pl.BlockSpec((None, WBUF, W_B), lambda b: (b, 0, 0))
    kern = functools.partial(_att_sample_kernel, S=S, WBUF=WBUF)
    return pl.pallas_call(
        kern,
        grid=(B,),
        in_specs=[tokb, tokb, tokb, cache, cache,
                  pl.BlockSpec((H_B, S, WBUF), lambda b: (0, 0, 0)), pl.BlockSpec((H_B, S, S), lambda b: (0, 0, 0))],
        out_specs=[tokb, cache, cache],
        out_shape=[jax.ShapeDtypeStruct((B * S, W_B), F32), jax.ShapeDtypeStruct((B, WBUF, W_B), F32),
                   jax.ShapeDtypeStruct((B, WBUF, W_B), F32)],
        compiler_params=_cparams(("arbitrary",)),
        name="att_sample",
    )(q, kn, vn, ck, cv, bias_c, bias_n)


def _sample_bias(rel_bias, S, WBUF):
    dist = np.arange(WBUF + S)
    mult = np.zeros(WBUF + S, np.int64)
    for w, dil in DIL_PATTERNS:
        mult += ((dist % dil == 0) & (dist <= w)).astype(np.int64)
    logm = np.where(mult > 0, np.log(np.maximum(mult, 1)), 0.0).astype(np.float32)
    tab = _distance_bias(rel_bias, WBUF + S) + logm[None, :]
    tab = jnp.where((mult > 0)[None, :], tab, NEG)
    rev = jnp.concatenate([tab[:, ::-1], jnp.full((H_B, S), NEG, F32)], axis=1)
    last = WBUF + S - 1
    bias_c = jnp.stack([rev[:, S - 1 - s:S - 1 - s + WBUF] for s in range(S)], axis=1)
    bias_n = jnp.stack([rev[:, last - s:last - s + S] for s in range(S)], axis=1)
    return bias_c, bias_n


def _out_kernel(ha_ref, mo_ref, att_ref, x_ref, g1_ref, sc_ref, sh_ref, mg_ref, ag_ref, g2_ref, bd_ref, wo_ref,
                y_ref, h2_ref):
    ha = ha_ref[...]
    parts = []
    for hd in range(H_A):
        a = ha[:, hd * DK_A:(hd + 1) * DK_A]
        parts.append(a * lax.rsqrt(jnp.mean(a * a, axis=-1, keepdims=True) + EPS))
    hn = jnp.concatenate(parts, axis=1) * mg_ref[...] * jax.nn.sigmoid(mo_ref[...])
    att = att_ref[...]
    an = att * lax.rsqrt(_group_mean_sq(att, bd_ref[...]) + EPS) * ag_ref[...]
    mix = jnp.dot(hn.astype(BF16), wo_ref[0:W_A, :], preferred_element_type=F32) \
        + jnp.dot(an.astype(BF16), wo_ref[W_A:W_A + W_B, :], preferred_element_type=F32)
    y = x_ref[...] + g1_ref[...] * mix
    y_ref[...] = y
    h2 = y * lax.rsqrt(jnp.mean(y * y, axis=-1, keepdims=True) + EPS) * g2_ref[...]
    h2_ref[...] = h2 * (1.0 + sc_ref[...]) + sh_ref[...]


def _out_proj(ha, mo, att, x2, gate1, scale2, shift2, per_token_mod, toks_per_seq, tm, mg, ag, g2, bd, wo):
    n = x2.shape[0]
    if per_token_mod:
        mod_spec = pl.BlockSpec((tm, D_MODEL), lambda i: (i, 0))
    else:
        tiles_per_seq = toks_per_seq // tm
        mod_spec = pl.BlockSpec((None, 1, D_MODEL), lambda i: (i // tiles_per_seq, 0, 0))
    const = lambda shape: pl.BlockSpec(shape, lambda i: (0,) * len(shape), pipeline_mode=pl.Buffered(1))
    tok = lambda w: pl.BlockSpec((tm, w), lambda i: (i, 0))
    return pl.pallas_call(
        _out_kernel,
        grid=(n // tm,),
        in_specs=[tok(W_A), tok(W_A), tok(W_B), tok(D_MODEL), mod_spec, mod_spec, mod_spec,
                  const((1, W_A)), const((1, W_B)), const((1, D_MODEL)), const((W_B, W_B)),
                  const((W_A + W_B, D_MODEL))],
        out_specs=[tok(D_MODEL), tok(D_MODEL)],
        out_shape=[jax.ShapeDtypeStruct((n, D_MODEL), F32)] * 2,
        compiler_params=_cparams(("arbitrary",)),
        name="out_proj",
    )(ha, mo, att, x2, gate1, scale2, shift2, mg, ag, g2, bd, wo)


def _top16(s, nrows):
    iota = lax.broadcasted_iota(jnp.int32, s.shape, 0).astype(F32)
    vals, idxs = [], []
    for _ in range(PEER_TOPK):
        m = jnp.max(s, axis=0, keepdims=True)
        pos = jnp.min(jnp.where(s == m, iota, float(nrows)), axis=0, keepdims=True)
        vals.append(m)
        idxs.append(pos)
        s = jnp.where(iota == pos, -jnp.inf, s)
    return jnp.concatenate(vals, axis=0), jnp.concatenate(idxs, axis=0)


def _peer_route_kernel(h2_ref, wq_ref, keys_ref, eidx_ref, gate_ref, sv_s, si_s, gt_s, et_s, *, tm):
    qh = jnp.dot(h2_ref[...].astype(BF16), wq_ref[...], preferred_element_type=F32)
    for hp in range(2 * PEER_HEADS):
        qs = qh[:, hp * N_SUBKEYS:(hp + 1) * N_SUBKEYS]
        s = _nt_dot(keys_ref[hp % 2], qs, precision=HIGHEST)
        v, i = _top16(s, N_SUBKEYS)
        sv_s[hp] = v
        si_s[hp] = i

    k = PEER_TOPK
    half = k // 2
    sub = lax.broadcasted_iota(jnp.int32, (half, tm), 0).astype(F32)
    pos = jnp.concatenate(
        [lax.broadcasted_iota(jnp.int32, (k, tm), 0).astype(F32)]
        + [float(a * k) + sub for a in range(1, half)] + [(sub + float(half)) * float(k)], axis=0)
    n_cand = float(k * k)

    def head(h, carry):
        sv0 = sv_s[2 * h]
        sv1 = sv_s[2 * h + 1]
        si0 = si_s[2 * h] * float(N_SUBKEYS)
        si1 = si_s[2 * h + 1]
        cands = [sv0[0:1, :] + sv1]
        cidxs = [si0[0:1, :] + si1]
        for a in range(1, half):
            c = sv0[a:a + 1, :] + sv1[0:half, :]
            nb = k // (a + 1)
            cands.append(c if nb >= half else jnp.where(sub < float(nb), c, -jnp.inf))
            cidxs.append(si0[a:a + 1, :] + si1[0:half, :])
        cands.append(sv0[half:k, :] + sv1[0:1, :])
        cidxs.append(si0[half:k, :] + si1[0:1, :])
        cand = jnp.concatenate(cands, axis=0)
        cidx = jnp.concatenate(cidxs, axis=0)
        fv, ev = [], []
        for _ in range(k):
            m = jnp.max(cand, axis=0, keepdims=True)
            first = jnp.min(jnp.where(cand == m, pos, n_cand), axis=0, keepdims=True)
            sel = pos == first
            ev.append(jnp.max(jnp.where(sel, cidx, -1.0), axis=0, keepdims=True))
            fv.append(m)
            cand = jnp.where(sel, -jnp.inf, cand)
        fvs = jnp.concatenate(fv, axis=0)
        e = jnp.exp(fvs - fv[0])
        rs = pl.ds(pl.multiple_of(h * k, k), k)
        gt_s[rs, :] = e / jnp.sum(e, axis=0, keepdims=True)
        et_s[rs, :] = jnp.concatenate(ev, axis=0)
        return carry

    lax.fori_loop(0, PEER_HEADS, head, 0)
    gate_ref[...] = gt_s[...]
    for j in range(tm // LANES):
        cs = slice(j * LANES, (j + 1) * LANES)
        eidx_ref[cs, :] = et_s[:, cs].T.astype(jnp.int32)


def _peer_route(h2, wq, keys, tm):
    n = h2.shape[0]
    kern = functools.partial(_peer_route_kernel, tm=tm)
    return pl.pallas_call(
        kern,
        grid=(n // tm,),
        in_specs=[pl.BlockSpec((tm, D_MODEL), lambda i: (i, 0)),
                  pl.BlockSpec((D_MODEL, 2 * PEER_HEADS * N_SUBKEYS), lambda i: (0, 0)),
                  pl.BlockSpec((2, N_SUBKEYS, N_SUBKEYS), lambda i: (0, 0, 0))],
        out_specs=[pl.BlockSpec((tm, PEER_PAIRS), lambda i: (i, 0)), pl.BlockSpec((PEER_PAIRS, tm), lambda i: (0, i))],
        out_shape=[jax.ShapeDtypeStruct((n, PEER_PAIRS), jnp.int32), jax.ShapeDtypeStruct((PEER_PAIRS, n), F32)],
        scratch_shapes=[pltpu.VMEM((2 * PEER_HEADS, PEER_TOPK, tm), F32)] * 2 + [pltpu.VMEM((PEER_PAIRS, tm), F32)] * 2,
        compiler_params=_cparams(("arbitrary",)),
        name="peer_route",
    )(h2, wq, keys)


ROW_CHUNKS = D_MODEL // LANES


def _pack_table(tab):
    return tab.astype(BF16).reshape(tab.shape[0], ROW_CHUNKS, LANES)


def _gather_rows(idx_ref, t, tab_ref, g_s):
    for p in range(PEER_PAIRS):
        g_s[ROW_CHUNKS * p:ROW_CHUNKS * (p + 1), :] = tab_ref[idx_ref[t, p]]


def _pair_rows(g_s, j):
    both = g_s[2 * ROW_CHUNKS * j:2 * ROW_CHUNKS * (j + 1), :].astype(F32)
    return both[0:ROW_CHUNKS, :], both[ROW_CHUNKS:2 * ROW_CHUNKS, :]


def _token_loop(tb, stage, compute, tokens_per_step=2):
    stage(0, 0)

    def step(i, carry):
        t0 = tokens_per_step * i
        for u in range(tokens_per_step):
            stage(jnp.minimum(t0 + u + 1, tb - 1), (u + 1) % 2)
            compute(t0 + u, u % 2)
        return carry

    lax.fori_loop(0, tb // tokens_per_step, step, 0)


def _fold_rows(a, sub):
    t = [x + pltpu.roll(x, 4, axis=0) for x in a]
    b = [jnp.where(sub < 4, t[j], t[j + 4]) for j in range(4)]
    c = [jnp.where(sub % 4 < 2, b[j] + pltpu.roll(b[j], 6, axis=0), b[j + 2] + pltpu.roll(b[j + 2], 2, axis=0))
         for j in range(2)]
    return jnp.where(sub % 2 == 0, c[0] + pltpu.roll(c[0], 7, axis=0), c[1] + pltpu.roll(c[1], 1, axis=0))


def _peer_act_kernel(idx_ref, x_ref, gate_ref, tab_ref, w_ref, g0_s, g1_s, m_s, a_s, *, tb):
    sub = lax.broadcasted_iota(jnp.int32, (ROW_CHUNKS, LANES), 0)
    lane = lax.broadcasted_iota(jnp.int32, (PEER_PAIRS, tb), 1)
    g_s = (g0_s, g1_s)
    a_s[...] = jnp.zeros((PEER_PAIRS, tb), F32)

    def compute(t, slot):
        row = x_ref[pl.ds(t, 1), :]
        x8 = jnp.concatenate([row[:, r * LANES:(r + 1) * LANES] for r in range(ROW_CHUNKS)], axis=0)
        for blk in range(PEER_PAIRS // 8):
            prods = []
            for j in range(4):
                u0, u1 = _pair_rows(g_s[slot], 4 * blk + j)
                prods += [u0 * x8, u1 * x8]
            m_s[8 * blk:8 * (blk + 1), :] = _fold_rows(prods, sub)
        dots = jnp.sum(m_s[...], axis=1, keepdims=True)
        a_s[...] = jnp.where(lane == t, dots, a_s[...])

    _token_loop(tb, lambda t, slot: _gather_rows(idx_ref, t, tab_ref, g_s[slot]), compute)
    act = a_s[...]
    gelu = 0.5 * act * (1.0 + lax.erf(act * (2.0 ** -0.5)))
    w_ref[...] = gate_ref[...] * gelu


def _table_spec(tab):
    return pl.BlockSpec(tab.shape, lambda i: (0, 0, 0), pipeline_mode=pl.Buffered(1))


def _peer_act(eidx, x3, gate, tab, tb):
    n = x3.shape[0]
    kern = functools.partial(_peer_act_kernel, tb=tb)
    return pl.pallas_call(
        kern,
        grid=(n // tb,),
        in_specs=[pl.BlockSpec((tb, PEER_PAIRS), lambda i: (i, 0), memory_space=pltpu.SMEM),
                  pl.BlockSpec((tb, D_MODEL), lambda i: (i, 0)),
                  pl.BlockSpec((PEER_PAIRS, tb), lambda i: (0, i)),
                  _table_spec(tab)],
        out_specs=pl.BlockSpec((PEER_PAIRS, tb), lambda i: (0, i)),
        out_shape=jax.ShapeDtypeStruct((PEER_PAIRS, n), F32),
        scratch_shapes=[pltpu.VMEM((PEER_PAIRS * ROW_CHUNKS, LANES), BF16)] * 2 + [
                        pltpu.VMEM((PEER_PAIRS, LANES), F32), pltpu.VMEM((PEER_PAIRS, tb), F32)],
        compiler_params=_cparams(("arbitrary",)),
        name="peer_act",
    )(eidx, x3, gate, tab)


def _peer_mix_kernel(idx_ref, w_ref, y_ref, g2_ref, tab_ref, o_ref, g0_s, g1_s, wb0_s, wb1_s, *, tb, per_token_mod):
    lane = lax.broadcasted_iota(jnp.int32, (PEER_PAIRS, tb), 1)
    g_s = (g0_s, g1_s)
    wb_s = (wb0_s, wb1_s)

    def stage(t, slot):
        _gather_rows(idx_ref, t, tab_ref, g_s[slot])
        col = jnp.sum(jnp.where(lane == t, w_ref[...], 0.0), axis=1, keepdims=True)
        wb_s[slot][...] = jnp.broadcast_to(col, (PEER_PAIRS, LANES))

    def compute(t, slot):
        accs = [jnp.zeros((ROW_CHUNKS, LANES), F32) for _ in range(4)]
        for j in range(PEER_PAIRS // 2):
            v0, v1 = _pair_rows(g_s[slot], j)
            w0 = jnp.broadcast_to(wb_s[slot][2 * j:2 * j + 1, :], (ROW_CHUNKS, LANES))
            w1 = jnp.broadcast_to(wb_s[slot][2 * j + 1:2 * j + 2, :], (ROW_CHUNKS, LANES))
            k = 2 * (j % 2)
            accs[k] = accs[k] + v0 * w0
            accs[k + 1] = accs[k + 1] + v1 * w1
        out = (accs[0] + accs[1]) + (accs[2] + accs[3])
        g2 = g2_ref[pl.ds(t, 1), :] if per_token_mod else g2_ref[...]
        out_row = jnp.concatenate([out[r:r + 1, :] for r in range(ROW_CHUNKS)], axis=1)
        o_ref[pl.ds(t, 1), :] = y_ref[pl.ds(t, 1), :] + g2 * out_row

    _token_loop(tb, stage, compute)


def _peer_mix(eidx, w, y3, gate2, per_token_mod, toks_per_seq, tab, tb):
    n = y3.shape[0]
    if per_token_mod:
        g_spec = pl.BlockSpec((tb, D_MODEL), lambda i: (i, 0))
    else:
        blocks_per_seq = toks_per_seq // tb
        g_spec = pl.BlockSpec((None, 1, D_MODEL), lambda i: (i // blocks_per_seq, 0, 0))
    kern = functools.partial(_peer_mix_kernel, tb=tb, per_token_mod=per_token_mod)
    return pl.pallas_call(
        kern,
        grid=(n // tb,),
        in_specs=[pl.BlockSpec((tb, PEER_PAIRS), lambda i: (i, 0), memory_space=pltpu.SMEM),
                  pl.BlockSpec((PEER_PAIRS, tb), lambda i: (0, i)),
                  pl.BlockSpec((tb, D_MODEL), lambda i: (i, 0)), g_spec, _table_spec(tab)],
        out_specs=pl.BlockSpec((tb, D_MODEL), lambda i: (i, 0)),
        out_shape=jax.ShapeDtypeStruct((n, D_MODEL), F32),
        scratch_shapes=[pltpu.VMEM((PEER_PAIRS * ROW_CHUNKS, LANES), BF16)] * 2 + [
                        pltpu.VMEM((PEER_PAIRS, LANES), F32)] * 2,
        compiler_params=_cparams(("arbitrary",)),
        name="peer_mix",
    )(eidx, w, y3, gate2, tab)


def _layer(x, mod, conv_buf, C0, n0, m0, k_buf, v_buf, rel_bias, wts):
    B, T, _ = x.shape
    n = B * T
    tm = min(PROJ_TILE, n)
    tb = PEER_BLOCK
    x2 = x.reshape(n, D_MODEL)
    shift1, scale1, gate1, shift2, scale2, gate2 = jnp.split(mod, 6, axis=-1)
    per_token = T % tm != 0
    if per_token:
        expand = lambda a: jnp.repeat(a, T, axis=0)
    else:
        expand = lambda a: a.reshape(B, 1, D_MODEL)

    q, k, v, mqk, mv, mo, gcol, grow = _in_proj(
        x2, expand(scale1), expand(shift1), per_token, T, tm, wts["g1"], wts["wm"], wts["wgc"], wts["wgr"],
        wts["bd"], wts["qg"], wts["kg"])

    if k_buf is None:
        att = _att_prompt(q, k, v, _prompt_bias(rel_bias), B, T)
        k_new = k.reshape(B, T, H_B, HD_B)
        v_new = v.reshape(B, T, H_B, HD_B)
    else:
        wbuf = k_buf.shape[1]
        bias_c, bias_n = _sample_bias(rel_bias, T, wbuf)
        att, k_new, v_new = _att_sample(q, k, v, k_buf.reshape(B, wbuf, W_B), v_buf.reshape(B, wbuf, W_B),
                                        bias_c, bias_n, B, T, wbuf)
        k_new = k_new.reshape(B, wbuf, H_B, HD_B)
        v_new = v_new.reshape(B, wbuf, H_B, HD_B)

    ha, C, nn, m = _mlstm(mqk, mv, gcol, grow, wts["conv_w"], wts["conv_b"], conv_buf, wts["b_gate"], C0, n0, m0, B, T)
    mqk3 = mqk.reshape(B, T, 2 * W_A)
    if T >= CONV_W - 1:
        conv_new = mqk3[:, T - (CONV_W - 1):]
    else:
        conv_new = jnp.concatenate([conv_buf, mqk3], axis=1)[:, -(CONV_W - 1):]

    y1, h2 = _out_proj(ha, mo, att, x2, expand(gate1), expand(scale2), expand(shift2), per_token, T, tm,
                       wts["mg"], wts["ag"], wts["g2"], wts["bd"], wts["wo"])

    eidx, gate = _peer_route(h2, wts["wq"], wts["keys"], min(ROUTE_TILE, n))
    w = _peer_act(eidx, h2, gate, wts["u_tab"], tb)
    y = _peer_mix(eidx, w, y1, expand(gate2), per_token, T, wts["v_tab"], tb)
    return (y.reshape(B, T, D_MODEL), k_new, v_new, conv_new, C, nn.reshape(B, H_A, DK_A), m[:, :, 0, 0])


def _prep_weights(l, norm1_g, norm2_g, w_in, b_gate, conv_w, conv_b, q_norm_g, k_norm_g, att_out_g, mlstm_out_g,
                  w_out, peer_wq, peer_keys, peer_u, peer_v):
    w = w_in[l]
    wg = w[:, W_MAIN:]
    grp = np.arange(W_B) // HD_B
    bd = jnp.asarray((grp[:, None] == grp[None, :]).astype(np.float32) / HD_B, BF16)
    return dict(
        g1=norm1_g[l].reshape(1, -1), g2=norm2_g[l].reshape(1, -1),
        wm=w[:, :W_MAIN].astype(BF16),
        wgc=jnp.pad(wg, ((0, 0), (0, LANES - 2 * H_A))), wgr=wg.T,
        bd=bd, qg=jnp.tile(q_norm_g[l], H_B).reshape(1, -1), kg=jnp.tile(k_norm_g[l], H_B).reshape(1, -1),
        conv_w=conv_w[l], conv_b=conv_b[l], b_gate=b_gate[l],
        mg=mlstm_out_g[l].reshape(1, -1), ag=att_out_g[l].reshape(1, -1),
        wo=w_out[l].astype(BF16), wq=peer_wq[l].astype(BF16), keys=peer_keys[l],
        u_tab=_pack_table(peer_u[l]), v_tab=_pack_table(peer_v[l]),
    )


def kernel(x_prompt, x_sample, cache_k, cache_v, state_conv, state_C, state_n, state_m, c_prompt, c_sample,
           rel_bias, w_ada, b_ada, norm1_g, norm2_g, w_in, b_gate, conv_w, conv_b, q_norm_g, k_norm_g,
           att_out_g, mlstm_out_g, w_out, peer_wq, peer_keys, peer_u, peer_v):
    depth = w_ada.shape[0]
    bp = x_prompt.shape[0]
    bs = x_sample.shape[0]
    yp, ys = x_prompt, x_sample
    sp, ss = [], []
    for l in range(depth):
        wts = _prep_weights(l, norm1_g, norm2_g, w_in, b_gate, conv_w, conv_b, q_norm_g, k_norm_g, att_out_g,
                            mlstm_out_g, w_out, peer_wq, peer_keys, peer_u, peer_v)
        mod = _ada(jnp.concatenate([c_prompt, c_sample], axis=0), w_ada[l], b_ada[l])
        zc = jnp.zeros((bp, CONV_W - 1, 2 * W_A), F32)
        zC = jnp.zeros((bp, H_A, DK_A, DK_A), F32)
        zn = jnp.zeros((bp, H_A, DK_A), F32)
        zm = jnp.zeros((bp, H_A), F32)
        outp = _layer(yp, mod[:bp], zc, zC, zn, zm, None, None, rel_bias, wts)
        outs = _layer(ys, mod[bp:], state_conv[l], state_C[l], state_n[l], state_m[l], cache_k[l], cache_v[l],
                      rel_bias, wts)
        yp, ys = outp[0], outs[0]
        sp.append(outp[1:])
        ss.append(outs[1:])
    k_p, v_p, conv_p, C_p, n_p, m_p = [jnp.stack([s[i] for s in sp], axis=0) for i in range(6)]
    k_s, v_s, conv_s, C_s, n_s, m_s = [jnp.stack([s[i] for s in ss], axis=0) for i in range(6)]
    return (yp, ys, k_p, v_p, conv_p, C_p, n_p, m_p, k_s, v_s, conv_s, C_s, n_s, m_s)
```

```python
import functools
import math

import numpy as np
import jax
import jax.numpy as jnp
from jax import lax
from jax.experimental import pallas as pl
from jax.experimental.pallas import tpu as pltpu

F32 = jnp.float32
BF16 = jnp.bfloat16
HIGHEST = lax.Precision.HIGHEST

D_MODEL = 1024
H_A = 4
DK_A = 128
W_A = H_A * DK_A
H_B = 8
HD_B = 64
W_B = H_B * HD_B
CONV_W = 4
MLSTM_CHUNK = 64
MLSTM_HP = 4
DIL_PATTERNS = ((128, 1), (512, 4), (2048, 16))
ATT_STEPS = 128
ATT_UNROLL = 8
N_BUCKETS = 32
MAX_DIST = 2048
PEER_HEADS = 8
N_SUBKEYS = 128
PEER_TOPK = 16
PEER_PAIRS = PEER_HEADS * PEER_TOPK
EPS = 1e-6
NEG = -1e30
W_MAIN = 3 * W_B + 4 * W_A
LANES = 128
PROJ_TILE = 512
ROUTE_TILE = 256
PEER_BLOCK = LANES
VMEM_LIMIT = 56 * 1024 * 1024


def _cparams(sem):
    return pltpu.CompilerParams(dimension_semantics=sem, vmem_limit_bytes=VMEM_LIMIT)


def _nt_dot(a, b, precision=None):
    return lax.dot_general(a, b, (((1,), (1,)), ((), ())), preferred_element_type=F32, precision=precision)


def _split_bf16(a):
    hi = a.astype(BF16)
    return hi, (a - hi.astype(F32)).astype(BF16)


def _nt_dot3(a, b):
    a_hi, a_lo = _split_bf16(a)
    b_hi, b_lo = _split_bf16(b)
    return _nt_dot(a_hi, b_hi) + (_nt_dot(a_hi, b_lo) + _nt_dot(a_lo, b_hi))


def _t5_bucket(dist):
    max_exact = N_BUCKETS // 2
    d = np.maximum(dist, 1).astype(np.float32)
    large = max_exact + (np.log(d / max_exact) / math.log(MAX_DIST / max_exact) * (N_BUCKETS - max_exact)).astype(np.int32)
    large = np.minimum(large, N_BUCKETS - 1)
    return np.where(dist < max_exact, dist, large).astype(np.int32)


def _ada_kernel(c_ref, w_ref, b_ref, o_ref):
    c = c_ref[...]
    s = c * jax.nn.sigmoid(c)
    o_ref[...] = jnp.dot(s, w_ref[...], preferred_element_type=F32, precision=HIGHEST) + b_ref[...]


def _ada(c_all, w_ada, b_ada):
    n = c_all.shape[0]
    return pl.pallas_call(
        _ada_kernel,
        grid=(6,),
        in_specs=[pl.BlockSpec((n, D_MODEL), lambda j: (0, 0)),
                  pl.BlockSpec((D_MODEL, D_MODEL), lambda j: (0, j)),
                  pl.BlockSpec((1, D_MODEL), lambda j: (0, j))],
        out_specs=pl.BlockSpec((n, D_MODEL), lambda j: (0, j)),
        out_shape=jax.ShapeDtypeStruct((n, 6 * D_MODEL), F32),
        compiler_params=_cparams(("arbitrary",)),
        name="ada",
    )(c_all, w_ada, b_ada.reshape(1, -1))


def _group_mean_sq(a, bd):
    sq = a * a
    hi = sq.astype(BF16)
    lo = (sq - hi.astype(F32)).astype(BF16)
    return jnp.dot(hi, bd, preferred_element_type=F32) + jnp.dot(lo, bd, preferred_element_type=F32)


def _in_kernel(x_ref, sc_ref, sh_ref, g1_ref, wm_ref, wgc_ref, wgr_ref, bd_ref, qg_ref, kg_ref,
               q_ref, k_ref, v_ref, mqk_ref, mv_ref, mo_ref, gcol_ref, grow_ref):
    x = x_ref[...]
    ms = jnp.mean(x * x, axis=-1, keepdims=True)
    h = x * lax.rsqrt(ms + EPS) * g1_ref[...]
    h = h * (1.0 + sc_ref[...]) + sh_ref[...]
    y = jnp.dot(h.astype(BF16), wm_ref[...], preferred_element_type=F32)
    bd = bd_ref[...]
    aq = y[:, 0:W_B]
    ak = y[:, W_B:2 * W_B]
    q_ref[...] = aq * lax.rsqrt(_group_mean_sq(aq, bd) + EPS) * qg_ref[...] * (HD_B ** -0.5)
    k_ref[...] = ak * lax.rsqrt(_group_mean_sq(ak, bd) + EPS) * kg_ref[...]
    v_ref[...] = y[:, 2 * W_B:3 * W_B]
    o = 3 * W_B
    mqk_ref[...] = y[:, o:o + 2 * W_A]
    mv_ref[...] = y[:, o + 2 * W_A:o + 3 * W_A]
    mo_ref[...] = y[:, o + 3 * W_A:o + 4 * W_A]
    h_hi, h_lo = _split_bf16(h)
    w_hi, w_lo = _split_bf16(wgc_ref[...])
    gcol_ref[...] = jnp.dot(h_hi, w_hi, preferred_element_type=F32) \
        + (jnp.dot(h_hi, w_lo, preferred_element_type=F32) + jnp.dot(h_lo, w_hi, preferred_element_type=F32))
    r_hi, r_lo = _split_bf16(wgr_ref[...])
    grow_ref[...] = _nt_dot(r_hi, h_hi) + (_nt_dot(r_hi, h_lo) + _nt_dot(r_lo, h_hi))


def _in_proj(x2, scale, shift, per_token_mod, toks_per_seq, tm, g1, wm, wgc, wgr, bd, qg, kg):
    n = x2.shape[0]
    nt = n // tm
    if per_token_mod:
        mod_spec = pl.BlockSpec((tm, D_MODEL), lambda i: (i, 0))
    else:
        tiles_per_seq = toks_per_seq // tm
        mod_spec = pl.BlockSpec((None, 1, D_MODEL), lambda i: (i // tiles_per_seq, 0, 0))
    const = lambda shape: pl.BlockSpec(shape, lambda i: (0,) * len(shape), pipeline_mode=pl.Buffered(1))
    tok = lambda w: pl.BlockSpec((tm, w), lambda i: (i, 0))
    outs = pl.pallas_call(
        _in_kernel,
        grid=(nt,),
        in_specs=[tok(D_MODEL), mod_spec, mod_spec, const((1, D_MODEL)), const((D_MODEL, W_MAIN)),
                  const((D_MODEL, LANES)), const((8, D_MODEL)), const((W_B, W_B)), const((1, W_B)), const((1, W_B))],
        out_specs=[tok(W_B), tok(W_B), tok(W_B), tok(2 * W_A), tok(W_A), tok(W_A), tok(LANES),
                   pl.BlockSpec((8, tm), lambda i: (0, i))],
        out_shape=[jax.ShapeDtypeStruct((n, W_B), F32)] * 3 + [jax.ShapeDtypeStruct((n, 2 * W_A), F32)]
                  + [jax.ShapeDtypeStruct((n, W_A), F32)] * 2 + [jax.ShapeDtypeStruct((n, LANES), F32),
                                                                 jax.ShapeDtypeStruct((8, n), F32)],
        compiler_params=_cparams(("arbitrary",)),
        name="in_proj",
    )(x2, scale, shift, g1, wm, wgc, wgr, bd, qg, kg)
    return outs


def _log_sigmoid(x):
    return jnp.minimum(x, 0.0) - jnp.log1p(jnp.exp(-jnp.abs(x)))


def _mlstm_kernel(bg_ref, m0_ref, mq_ref, mk_ref, cwq_ref, cwk_ref, cbq_ref, cbk_ref, bufq_ref, bufk_ref,
                  v_ref, gcol_ref, grow_ref, c0_ref, n0_ref,
                  h_ref, c_ref, n_ref, m_ref, sq, sk, *, T, L, rows_whole):
    b = pl.program_id(0)
    hg = pl.program_id(1)
    nc = T // L

    def conv(u_ref, buf_ref, w_ref, cb_ref, s_ref, out_ref):
        s_ref[0:8, :] = jnp.zeros((8, MLSTM_HP * LANES), F32)
        s_ref[5:8, :] = buf_ref[...]
        s_ref[8:8 + T, :] = u_ref[...]
        y = cb_ref[...]
        for j in range(CONV_W):
            y = y + s_ref[5 + j:5 + j + T, :] * w_ref[j:j + 1, :]
        out_ref[...] = y * jax.nn.sigmoid(y)

    conv(mq_ref, bufq_ref, cwq_ref, cbq_ref, sq, h_ref)
    conv(mk_ref, bufk_ref, cwk_ref, cbk_ref, sk, sq.at[8:8 + T, :])

    lane = lax.broadcasted_iota(jnp.int32, (L, LANES), 1)
    ri = lax.broadcasted_iota(jnp.int32, (L, L), 0)
    ci = lax.broadcasted_iota(jnp.int32, (L, L), 1)
    causal = ri >= ci
    tri = causal.astype(F32)
    tri_t = (ri <= ci).astype(F32)

    def chunk(c, carry):
        r0 = pl.multiple_of(c * L, L)
        rc = c + (b * nc if rows_whole else 0)
        g = gcol_ref[pl.ds(r0, L), :]
        st = []
        for j in range(MLSTM_HP):
            C, n, m = carry[j]
            hd = hg * MLSTM_HP + j
            ig_b = bg_ref[hd]
            f_b = bg_ref[H_A + hd]
            cs = slice(j * LANES, (j + 1) * LANES)
            qc = h_ref[pl.ds(r0, L), cs]
            kc = sq[pl.ds(r0 + 8, L), cs] * (DK_A ** -0.5)
            vb = v_ref[pl.ds(r0, L), cs].astype(BF16)
            ig_col = jnp.sum(jnp.where(lane == hd, g, 0.0), axis=1, keepdims=True) + ig_b
            f_col = jnp.sum(jnp.where(lane == hd + H_A, g, 0.0), axis=1, keepdims=True) + f_b
            ig_row = grow_ref[hd, pl.ds(rc, 1), :] + ig_b
            lf_row = _log_sigmoid(grow_ref[hd + H_A, pl.ds(rc, 1), :] + f_b)
            b_col = jnp.dot(tri, jnp.broadcast_to(_log_sigmoid(f_col), (L, LANES)), preferred_element_type=F32,
                            precision=HIGHEST)[:, 0:1]
            b_row = jnp.dot(jnp.broadcast_to(lf_row, (8, L)), tri_t, preferred_element_type=F32,
                            precision=HIGHEST)[0:1, :]
            qb = qc.astype(BF16)
            qk = _nt_dot(qb, kc.astype(BF16))
            qC = jnp.dot(qb, C.astype(BF16), preferred_element_type=F32)
            st.append(dict(C=C, n=n, m=m, cs=cs, qc=qc, kc=kc, vb=vb, ig_col=ig_col, ig_row=ig_row,
                           b_col=b_col, b_row=b_row, qk=qk, qC=qC))
        for d in st:
            b_col, b_row, m = d["b_col"], d["b_row"], d["m"]
            g_col = b_col + m
            dm = jnp.where(causal, b_col - b_row + d["ig_row"], -jnp.inf)
            mt = jnp.maximum(g_col, jnp.max(dm, axis=1, keepdims=True))
            s = d["qk"] * jnp.exp(dm - mt)
            wg = jnp.exp(g_col - mt)
            bl = b_col[L - 1:L, :]
            m_new = jnp.maximum(bl + m, jnp.max(bl - b_row + d["ig_row"], axis=1, keepdims=True))
            kw = jnp.exp(bl - b_col + d["ig_col"] - m_new) * d["kc"]
            d.update(mt=mt, s=s, wg=wg, m_new=m_new, kw=kw, wc=jnp.exp(bl + m - m_new))
        out = []
        for d in st:
            s, wg = d["s"], d["wg"]
            num = wg * d["qC"] + jnp.dot(s.astype(BF16), d["vb"], preferred_element_type=F32)
            den = wg * jnp.sum(d["qc"] * d["n"], axis=1, keepdims=True) + jnp.sum(s, axis=1, keepdims=True)
            h_ref[pl.ds(r0, L), d["cs"]] = num / jnp.maximum(jnp.abs(den), jnp.exp(-d["mt"]))
            upd = lax.dot_general(d["kw"].astype(BF16), d["vb"], (((0,), (0,)), ((), ())), preferred_element_type=F32)
            out.append((d["wc"] * d["C"] + upd, d["wc"] * d["n"] + jnp.sum(d["kw"], axis=0, keepdims=True), d["m_new"]))
        return tuple(out)

    init = tuple((c0_ref[j], n0_ref[j], jnp.full((1, 1), m0_ref[b, hg * MLSTM_HP + j], F32))
                 for j in range(MLSTM_HP))
    final = lax.fori_loop(0, nc, chunk, init)
    for j in range(MLSTM_HP):
        c_ref[j] = final[j][0]
        n_ref[j] = final[j][1]
        m_ref[j] = jnp.broadcast_to(final[j][2], (1, LANES))


def _mlstm(mqk, mv, gcol, grow, conv_w, conv_b, conv_buf, b_gate, C0, n0, m0, B, T):
    L = math.gcd(T, MLSTM_CHUNK)
    nc = T // L
    n = B * T
    rows_whole = nc % 8 != 0
    grow3 = grow.reshape(8, n // L, L)
    if rows_whole:
        grow_spec = pl.BlockSpec((8, n // L, L), lambda b, h: (0, 0, 0))
    else:
        grow_spec = pl.BlockSpec((8, nc, L), lambda b, h: (0, b, 0))
    smem = pl.BlockSpec(memory_space=pltpu.SMEM)
    wide = MLSTM_HP * LANES
    ng = H_A // MLSTM_HP
    colq = lambda rows: pl.BlockSpec((rows, wide), lambda b, h: (0, h))
    colk = lambda rows: pl.BlockSpec((rows, wide), lambda b, h: (0, h + ng))
    st = lambda r, c: pl.BlockSpec((None, MLSTM_HP, r, c), lambda b, h: (b, h, 0, 0))
    kern = functools.partial(_mlstm_kernel, T=T, L=L, rows_whole=rows_whole)
    return pl.pallas_call(
        kern,
        grid=(B, ng),
        in_specs=[smem, smem,
                  pl.BlockSpec((T, wide), lambda b, h: (b, h)), pl.BlockSpec((T, wide), lambda b, h: (b, h + ng)),
                  colq(CONV_W), colk(CONV_W), colq(1), colk(1),
                  pl.BlockSpec((None, CONV_W - 1, wide), lambda b, h: (b, 0, h)),
                  pl.BlockSpec((None, CONV_W - 1, wide), lambda b, h: (b, 0, h + ng)),
                  pl.BlockSpec((T, wide), lambda b, h: (b, h)),
                  pl.BlockSpec((T, LANES), lambda b, h: (b, 0)),
                  grow_spec, st(DK_A, DK_A), st(1, DK_A)],
        out_specs=[pl.BlockSpec((T, wide), lambda b, h: (b, h)), st(DK_A, DK_A), st(1, DK_A), st(1, LANES)],
        out_shape=[jax.ShapeDtypeStruct((n, W_A), F32), jax.ShapeDtypeStruct((B, H_A, DK_A, DK_A), F32),
                   jax.ShapeDtypeStruct((B, H_A, 1, DK_A), F32), jax.ShapeDtypeStruct((B, H_A, 1, LANES), F32)],
        scratch_shapes=[pltpu.VMEM((T + 8, wide), F32), pltpu.VMEM((T + 8, wide), F32)],
        compiler_params=_cparams(("arbitrary", "arbitrary")),
        name="mlstm",
    )(b_gate, m0, mqk, mqk, conv_w, conv_w, conv_b.reshape(1, -1), conv_b.reshape(1, -1), conv_buf, conv_buf,
      mv, gcol, grow3, C0, n0.reshape(B, H_A, 1, DK_A))


def _att_prompt_kernel(q_ref, k_ref, v_ref, bias_ref, o_ref, num_s, mx_s, dn_s, mxs_s, *, T):
    lane = lax.broadcasted_iota(jnp.int32, (ATT_STEPS, LANES), 1)
    first = lane < HD_B
    nblk = T // ATT_STEPS

    for br, (_, dil) in enumerate(DIL_PATTERNS):
        span = ATT_STEPS * dil
        has_prev = T > span

        def rows(start, dil=dil):
            return pl.ds(start, ATT_STEPS, stride=dil) if dil > 1 else pl.ds(start, ATT_STEPS)

        def blk(i0, carry, br=br, dil=dil, span=span, has_prev=has_prev, rows=rows):
            units = []
            for u in range(ATT_UNROLL):
                i = i0 + u * (nblk // ATT_UNROLL)
                nb = i // dil
                start = nb * span + i % dil
                qb = q_ref[rows(start), :]
                kcur = k_ref[rows(start), :].astype(BF16)
                kprev = k_ref[rows(jnp.maximum(start - span, 0)), :].astype(BF16) if has_prev else None
                for hh in range(2):
                    own = first if hh == 0 else ~first
                    qh = jnp.where(own, qb, 0.0).astype(BF16)
                    lc = _nt_dot(qh, kcur) + bias_ref[br, hh, :, ATT_STEPS:2 * ATT_STEPS]
                    lp = None
                    if has_prev:
                        lp = _nt_dot(qh, kprev) + bias_ref[br, hh, :, 0:ATT_STEPS]
                        lp = jnp.where(nb > 0, lp, NEG)
                    units.append((start, own, lc, lp))
            weights = []
            for start, own, lc, lp in units:
                mx = jnp.max(jnp.maximum(lc, lp) if has_prev else lc, axis=1, keepdims=True)
                pc = jnp.exp(lc - mx).astype(BF16)
                pp = jnp.exp(lp - mx).astype(BF16) if has_prev else None
                weights.append((mx, pc, pp))
            res = []
            for (start, own, _, _), (mx, pc, pp) in zip(units, weights):
                acc = jnp.dot(pc, jnp.where(own, v_ref[rows(start), :], 1.0).astype(BF16), preferred_element_type=F32)
                if has_prev:
                    vprev = v_ref[rows(jnp.maximum(start - span, 0)), :]
                    acc = acc + jnp.dot(pp, jnp.where(own, vprev, 1.0).astype(BF16), preferred_element_type=F32)
                res.append((acc, mx))
            for u in range(ATT_UNROLL):
                start = units[2 * u][0]
                (a0, m0), (a1, m1) = res[2 * u], res[2 * u + 1]
                num_s[br, rows(start), :] = jnp.where(first, a0, a1)
                dn_s[br, rows(start), :] = jnp.where(first, a1, a0)
                mx_s[br, rows(start), :] = jnp.where(first, m0, m1)
                mxs_s[br, rows(start), :] = jnp.where(first, m1, m0)
            return carry

        lax.fori_loop(0, nblk // ATT_UNROLL, blk, 0)

    def comb(i, carry):
        rs = pl.ds(pl.multiple_of(i * ATT_STEPS, ATT_STEPS), ATT_STEPS)
        m = jnp.maximum(jnp.maximum(mx_s[0, rs, :], mx_s[1, rs, :]), mx_s[2, rs, :])
        ms = jnp.maximum(jnp.maximum(mxs_s[0, rs, :], mxs_s[1, rs, :]), mxs_s[2, rs, :])
        num = jnp.zeros((ATT_STEPS, LANES), F32)
        den = jnp.zeros((ATT_STEPS, LANES), F32)
        for br in range(3):
            num = num + jnp.exp(mx_s[br, rs, :] - m) * num_s[br, rs, :]
            den = den + jnp.exp(mxs_s[br, rs, :] - ms) * dn_s[br, rs, :]
        o_ref[rs, :] = num / pltpu.roll(den, HD_B, axis=1)
        return carry

    lax.fori_loop(0, nblk, comb, 0)


def _att_prompt(q, k, v, bias, B, T):
    n = B * T
    blk = pl.BlockSpec((T, LANES), lambda b, p: (b, p))
    kern = functools.partial(_att_prompt_kernel, T=T)
    return pl.pallas_call(
        kern,
        grid=(B, H_B // 2),
        in_specs=[blk, blk, blk, pl.BlockSpec((3, 2, ATT_STEPS, 2 * ATT_STEPS), lambda b, p: (0, p, 0, 0))],
        out_specs=blk,
        out_shape=jax.ShapeDtypeStruct((n, W_B), F32),
        scratch_shapes=[pltpu.VMEM((3, T, LANES), F32)] * 4,
        compiler_params=_cparams(("arbitrary", "arbitrary")),
        name="att_prompt",
    )(q, k, v, bias)


def _prompt_bias(rel_bias):
    tab = _distance_bias(rel_bias, ATT_STEPS * DIL_PATTERNS[-1][1] + 1)
    pad = jnp.full((H_B, ATT_STEPS - 1), NEG, F32)
    period = 3 * ATT_STEPS - 1
    tabs = []
    for _, dil in DIL_PATTERNS:
        w = jnp.concatenate([pad, tab[:, 0:ATT_STEPS * dil + 1:dil], pad], axis=1)[:, ::-1]
        w = jnp.concatenate([w[:, ATT_STEPS - 1:], w[:, :ATT_STEPS - 1]], axis=1)
        flat = jnp.tile(w, (1, ATT_STEPS))[:, :ATT_STEPS * (period - 1)]
        tabs.append(flat.reshape(H_B, ATT_STEPS, period - 1)[:, :, :2 * ATT_STEPS])
    return jnp.stack(tabs, axis=0)


def _distance_bias(rel_bias, n):
    onehot = np.zeros((n, N_BUCKETS), np.float32)
    onehot[np.arange(n), _t5_bucket(np.arange(n))] = 1.0
    return jnp.dot(jnp.asarray(onehot), rel_bias.astype(F32), precision=HIGHEST).T


def _att_sample_kernel(q_ref, kn_ref, vn_ref, ck_ref, cv_ref, bc_ref, bn_ref, o_ref, ko_ref, vo_ref, *, S, WBUF):
    lane = lax.broadcasted_iota(jnp.int32, (S, LANES), 1)
    first = lane < HD_B
    for hp in range(H_B // 2):
        cs = slice(hp * LANES, (hp + 1) * LANES)
        kc = ck_ref[:, cs].astype(BF16)
        vc = cv_ref[:, cs].astype(BF16)
        kn = kn_ref[:, cs].astype(BF16)
        vn = vn_ref[:, cs].astype(BF16)
        qp = q_ref[:, cs]
        res = []
        for hh in range(2):
            h = 2 * hp + hh
            qh = jnp.where(first if hh == 0 else ~first, qp, 0.0).astype(BF16)
            lc = _nt_dot(qh, kc) + bc_ref[h]
            ln = _nt_dot(qh, kn) + bn_ref[h]
            mx = jnp.maximum(jnp.max(lc, axis=1, keepdims=True), jnp.max(ln, axis=1, keepdims=True))
            pc = jnp.exp(lc - mx)
            pn = jnp.exp(ln - mx)
            den = jnp.sum(pc, axis=1, keepdims=True) + jnp.sum(pn, axis=1, keepdims=True)
            num = jnp.dot(pc.astype(BF16), vc, preferred_element_type=F32) \
                + jnp.dot(pn.astype(BF16), vn, preferred_element_type=F32)
            res.append(num / den)
        o_ref[:, cs] = jnp.where(first, res[0], res[1])
    ko_ref[0:WBUF - S, :] = ck_ref[S:WBUF, :]
    ko_ref[WBUF - S:WBUF, :] = kn_ref[...]
    vo_ref[0:WBUF - S, :] = cv_ref[S:WBUF, :]
    vo_ref[WBUF - S:WBUF, :] = vn_ref[...]


def _att_sample(q, kn, vn, ck, cv, bias_c, bias_n, B, S, WBUF):
    tokb = pl.BlockSpec((S, W_B), lambda b: (b, 0))
    cache = pl.BlockSpec((None, WBUF, W_B), lambda b: (b, 0, 0))
    kern = functools.partial(_att_sample_kernel, S=S, WBUF=WBUF)
    return pl.pallas_call(
        kern,
        grid=(B,),
        in_specs=[tokb, tokb, tokb, cache, cache,
                  pl.BlockSpec((H_B, S, WBUF), lambda b: (0, 0, 0)), pl.BlockSpec((H_B, S, S), lambda b: (0, 0, 0))],
        out_specs=[tokb, cache, cache],
        out_shape=[jax.ShapeDtypeStruct((B * S, W_B), F32), jax.ShapeDtypeStruct((B, WBUF, W_B), F32),
                   jax.ShapeDtypeStruct((B, WBUF, W_B), F32)],
        compiler_params=_cparams(("arbitrary",)),
        name="att_sample",
    )(q, kn, vn, ck, cv, bias_c, bias_n)


def _sample_bias(rel_bias, S, WBUF):
    dist = np.arange(WBUF + S)
    mult = np.zeros(WBUF + S, np.int64)
    for w, dil in DIL_PATTERNS:
        mult += ((dist % dil == 0) & (dist <= w)).astype(np.int64)
    logm = np.where(mult > 0, np.log(np.maximum(mult, 1)), 0.0).astype(np.float32)
    tab = _distance_bias(rel_bias, WBUF + S) + logm[None, :]
    tab = jnp.where((mult > 0)[None, :], tab, NEG)
    rev = jnp.concatenate([tab[:, ::-1], jnp.full((H_B, S), NEG, F32)], axis=1)
    last = WBUF + S - 1
    bias_c = jnp.stack([rev[:, S - 1 - s:S - 1 - s + WBUF] for s in range(S)], axis=1)
    bias_n = jnp.stack([rev[:, last - s:last - s + S] for s in range(S)], axis=1)
    return bias_c, bias_n


def _out_kernel(ha_ref, mo_ref, att_ref, x_ref, g1_ref, sc_ref, sh_ref, mg_ref, ag_ref, g2_ref, bd_ref, wo_ref,
                y_ref, h2_ref):
    ha = ha_ref[...]
    parts = []
    for hd in range(H_A):
        a = ha[:, hd * DK_A:(hd + 1) * DK_A]
        parts.append(a * lax.rsqrt(jnp.mean(a * a, axis=-1, keepdims=True) + EPS))
    hn = jnp.concatenate(parts, axis=1) * mg_ref[...] * jax.nn.sigmoid(mo_ref[...])
    att = att_ref[...]
    an = att * lax.rsqrt(_group_mean_sq(att, bd_ref[...]) + EPS) * ag_ref[...]
    mix = jnp.dot(hn.astype(BF16), wo_ref[0:W_A, :], preferred_element_type=F32) \
        + jnp.dot(an.astype(BF16), wo_ref[W_A:W_A + W_B, :], preferred_element_type=F32)
    y = x_ref[...] + g1_ref[...] * mix
    y_ref[...] = y
    h2 = y * lax.rsqrt(jnp.mean(y * y, axis=-1, keepdims=True) + EPS) * g2_ref[...]
    h2_ref[...] = h2 * (1.0 + sc_ref[...]) + sh_ref[...]


def _out_proj(ha, mo, att, x2, gate1, scale2, shift2, per_token_mod, toks_per_seq, tm, mg, ag, g2, bd, wo):
    n = x2.shape[0]
    if per_token_mod:
        mod_spec = pl.BlockSpec((tm, D_MODEL), lambda i: (i, 0))
    else:
        tiles_per_seq = toks_per_seq // tm
        mod_spec = pl.BlockSpec((None, 1, D_MODEL), lambda i: (i // tiles_per_seq, 0, 0))
    const = lambda shape: pl.BlockSpec(shape, lambda i: (0,) * len(shape), pipeline_mode=pl.Buffered(1))
    tok = lambda w: pl.BlockSpec((tm, w), lambda i: (i, 0))
    return pl.pallas_call(
        _out_kernel,
        grid=(n // tm,),
        in_specs=[tok(W_A), tok(W_A), tok(W_B), tok(D_MODEL), mod_spec, mod_spec, mod_spec,
                  const((1, W_A)), const((1, W_B)), const((1, D_MODEL)), const((W_B, W_B)),
                  const((W_A + W_B, D_MODEL))],
        out_specs=[tok(D_MODEL), tok(D_MODEL)],
        out_shape=[jax.ShapeDtypeStruct((n, D_MODEL), F32)] * 2,
        compiler_params=_cparams(("arbitrary",)),
        name="out_proj",
    )(ha, mo, att, x2, gate1, scale2, shift2, mg, ag, g2, bd, wo)


def _top16(s, nrows):
    iota = lax.broadcasted_iota(jnp.int32, s.shape, 0).astype(F32)
    vals, idxs = [], []
    for _ in range(PEER_TOPK):
        m = jnp.max(s, axis=0, keepdims=True)
        pos = jnp.min(jnp.where(s == m, iota, float(nrows)), axis=0, keepdims=True)
        vals.append(m)
        idxs.append(pos)
        s = jnp.where(iota == pos, -jnp.inf, s)
    return jnp.concatenate(vals, axis=0), jnp.concatenate(idxs, axis=0)


def _peer_route_kernel(h2_ref, wq_ref, keys_ref, eidx_ref, gate_ref, sv_s, si_s, gt_s, et_s, *, tm):
    qh = jnp.dot(h2_ref[...].astype(BF16), wq_ref[...], preferred_element_type=F32)
    for hp in range(2 * PEER_HEADS):
        qs = qh[:, hp * N_SUBKEYS:(hp + 1) * N_SUBKEYS]
        s = _nt_dot3(keys_ref[hp % 2], qs)
        v, i = _top16(s, N_SUBKEYS)
        sv_s[hp] = v
        si_s[hp] = i

    k = PEER_TOPK
    half = k // 2
    sub = lax.broadcasted_iota(jnp.int32, (half, tm), 0).astype(F32)
    pos = jnp.concatenate(
        [lax.broadcasted_iota(jnp.int32, (k, tm), 0).astype(F32)]
        + [float(a * k) + sub for a in range(1, half)] + [(sub + float(half)) * float(k)], axis=0)
    n_cand = float(k * k)

    def head(h, carry):
        sv0 = sv_s[2 * h]
        sv1 = sv_s[2 * h + 1]
        si0 = si_s[2 * h] * float(N_SUBKEYS)
        si1 = si_s[2 * h + 1]
        cands = [sv0[0:1, :] + sv1]
        cidxs = [si0[0:1, :] + si1]
        for a in range(1, half):
            c = sv0[a:a + 1, :] + sv1[0:half, :]
            nb = k // (a + 1)
            cands.append(c if nb >= half else jnp.where(sub < float(nb), c, -jnp.inf))
            cidxs.append(si0[a:a + 1, :] + si1[0:half, :])
        cands.append(sv0[half:k, :] + sv1[0:1, :])
        cidxs.append(si0[half:k, :] + si1[0:1, :])
        cand = jnp.concatenate(cands, axis=0)
        cidx = jnp.concatenate(cidxs, axis=0)
        fv, ev = [], []
        for _ in range(k):
            m = jnp.max(cand, axis=0, keepdims=True)
            first = jnp.min(jnp.where(cand == m, pos, n_cand), axis=0, keepdims=True)
            sel = pos == first
            ev.append(jnp.max(jnp.where(sel, cidx, -1.0), axis=0, keepdims=True))
            fv.append(m)
            cand = jnp.where(sel, -jnp.inf, cand)
        fvs = jnp.concatenate(fv, axis=0)
        e = jnp.exp(fvs - fv[0])
        rs = pl.ds(pl.multiple_of(h * k, k), k)
        gt_s[rs, :] = e / jnp.sum(e, axis=0, keepdims=True)
        et_s[rs, :] = jnp.concatenate(ev, axis=0)
        return carry

    lax.fori_loop(0, PEER_HEADS, head, 0)
    gate_ref[...] = gt_s[...]
    for j in range(tm // LANES):
        cs = slice(j * LANES, (j + 1) * LANES)
        eidx_ref[cs, :] = et_s[:, cs].T.astype(jnp.int32)


def _peer_route(h2, wq, keys, tm):
    n = h2.shape[0]
    kern = functools.partial(_peer_route_kernel, tm=tm)
    return pl.pallas_call(
        kern,
        grid=(n // tm,),
        in_specs=[pl.BlockSpec((tm, D_MODEL), lambda i: (i, 0)),
                  pl.BlockSpec((D_MODEL, 2 * PEER_HEADS * N_SUBKEYS), lambda i: (0, 0)),
                  pl.BlockSpec((2, N_SUBKEYS, N_SUBKEYS), lambda i: (0, 0, 0))],
        out_specs=[pl.BlockSpec((tm, PEER_PAIRS), lambda i: (i, 0)), pl.BlockSpec((PEER_PAIRS, tm), lambda i: (0, i))],
        out_shape=[jax.ShapeDtypeStruct((n, PEER_PAIRS), jnp.int32), jax.ShapeDtypeStruct((PEER_PAIRS, n), F32)],
        scratch_shapes=[pltpu.VMEM((2 * PEER_HEADS, PEER_TOPK, tm), F32)] * 2 + [pltpu.VMEM((PEER_PAIRS, tm), F32)] * 2,
        compiler_params=_cparams(("arbitrary",)),
        name="peer_route",
    )(h2, wq, keys)


ROW_CHUNKS = D_MODEL // LANES


def _pack_table(tab):
    return tab.astype(BF16).reshape(tab.shape[0], ROW_CHUNKS, LANES)


def _gather_rows(idx_ref, t, tab_ref, g_s):
    for p in range(PEER_PAIRS):
        g_s[ROW_CHUNKS * p:ROW_CHUNKS * (p + 1), :] = tab_ref[idx_ref[t, p]]


def _pair_rows(g_s, j):
    both = g_s[2 * ROW_CHUNKS * j:2 * ROW_CHUNKS * (j + 1), :].astype(F32)
    return both[0:ROW_CHUNKS, :], both[ROW_CHUNKS:2 * ROW_CHUNKS, :]


def _token_loop(tb, stage, compute, tokens_per_step=2):
    stage(0, 0)

    def step(i, carry):
        t0 = tokens_per_step * i
        for u in range(tokens_per_step):
            stage(jnp.minimum(t0 + u + 1, tb - 1), (u + 1) % 2)
            compute(t0 + u, u % 2)
        return carry

    lax.fori_loop(0, tb // tokens_per_step, step, 0)


def _fold_rows(a, sub):
    t = [x + pltpu.roll(x, 4, axis=0) for x in a]
    b = [jnp.where(sub < 4, t[j], t[j + 4]) for j in range(4)]
    c = [jnp.where(sub % 4 < 2, b[j] + pltpu.roll(b[j], 6, axis=0), b[j + 2] + pltpu.roll(b[j + 2], 2, axis=0))
         for j in range(2)]
    return jnp.where(sub % 2 == 0, c[0] + pltpu.roll(c[0], 7, axis=0), c[1] + pltpu.roll(c[1], 1, axis=0))


def _peer_act_kernel(idx_ref, x_ref, gate_ref, tab_ref, w_ref, g0_s, g1_s, m_s, a_s, *, tb):
    sub = lax.broadcasted_iota(jnp.int32, (ROW_CHUNKS, LANES), 0)
    lane = lax.broadcasted_iota(jnp.int32, (PEER_PAIRS, tb), 1)
    g_s = (g0_s, g1_s)
    a_s[...] = jnp.zeros((PEER_PAIRS, tb), F32)

    def compute(t, slot):
        row = x_ref[pl.ds(t, 1), :]
        x8 = jnp.concatenate([row[:, r * LANES:(r + 1) * LANES] for r in range(ROW_CHUNKS)], axis=0)
        for blk in range(PEER_PAIRS // 8):
            prods = []
            for j in range(4):
                u0, u1 = _pair_rows(g_s[slot], 4 * blk + j)
                prods += [u0 * x8, u1 * x8]
            m_s[8 * blk:8 * (blk + 1), :] = _fold_rows(prods, sub)
        dots = jnp.sum(m_s[...], axis=1, keepdims=True)
        a_s[...] = jnp.where(lane == t, dots, a_s[...])

    _token_loop(tb, lambda t, slot: _gather_rows(idx_ref, t, tab_ref, g_s[slot]), compute)
    act = a_s[...]
    gelu = 0.5 * act * (1.0 + lax.erf(act * (2.0 ** -0.5)))
    w_ref[...] = gate_ref[...] * gelu


def _table_spec(tab):
    return pl.BlockSpec(tab.shape, lambda i: (0, 0, 0), pipeline_mode=pl.Buffered(1))


def _peer_act(eidx, x3, gate, tab, tb):
    n = x3.shape[0]
    kern = functools.partial(_peer_act_kernel, tb=tb)
    return pl.pallas_call(
        kern,
        grid=(n // tb,),
        in_specs=[pl.BlockSpec((tb, PEER_PAIRS), lambda i: (i, 0), memory_space=pltpu.SMEM),
                  pl.BlockSpec((tb, D_MODEL), lambda i: (i, 0)),
                  pl.BlockSpec((PEER_PAIRS, tb), lambda i: (0, i)),
                  _table_spec(tab)],
        out_specs=pl.BlockSpec((PEER_PAIRS, tb), lambda i: (0, i)),
        out_shape=jax.ShapeDtypeStruct((PEER_PAIRS, n), F32),
        scratch_shapes=[pltpu.VMEM((PEER_PAIRS * ROW_CHUNKS, LANES), BF16)] * 2 + [
                        pltpu.VMEM((PEER_PAIRS, LANES), F32), pltpu.VMEM((PEER_PAIRS, tb), F32)],
        compiler_params=_cparams(("arbitrary",)),
        name="peer_act",
    )(eidx, x3, gate, tab)


def _peer_mix_kernel(idx_ref, w_ref, y_ref, g2_ref, tab_ref, o_ref, g0_s, g1_s, wb0_s, wb1_s, *, tb, per_token_mod):
    lane = lax.broadcasted_iota(jnp.int32, (PEER_PAIRS, tb), 1)
    g_s = (g0_s, g1_s)
    wb_s = (wb0_s, wb1_s)

    def stage(t, slot):
        _gather_rows(idx_ref, t, tab_ref, g_s[slot])
        col = jnp.sum(jnp.where(lane == t, w_ref[...], 0.0), axis=1, keepdims=True)
        wb_s[slot][...] = jnp.broadcast_to(col, (PEER_PAIRS, LANES))

    def compute(t, slot):
        accs = [jnp.zeros((ROW_CHUNKS, LANES), F32) for _ in range(4)]
        for j in range(PEER_PAIRS // 2):
            v0, v1 = _pair_rows(g_s[slot], j)
            w0 = jnp.broadcast_to(wb_s[slot][2 * j:2 * j + 1, :], (ROW_CHUNKS, LANES))
            w1 = jnp.broadcast_to(wb_s[slot][2 * j + 1:2 * j + 2, :], (ROW_CHUNKS, LANES))
            k = 2 * (j % 2)
            accs[k] = accs[k] + v0 * w0
            accs[k + 1] = accs[k + 1] + v1 * w1
        out = (accs[0] + accs[1]) + (accs[2] + accs[3])
        g2 = g2_ref[pl.ds(t, 1), :] if per_token_mod else g2_ref[...]
        out_row = jnp.concatenate([out[r:r + 1, :] for r in range(ROW_CHUNKS)], axis=1)
        o_ref[pl.ds(t, 1), :] = y_ref[pl.ds(t, 1), :] + g2 * out_row

    _token_loop(tb, stage, compute)


def _peer_mix(eidx, w, y3, gate2, per_token_mod, toks_per_seq, tab, tb):
    n = y3.shape[0]
    if per_token_mod:
        g_spec = pl.BlockSpec((tb, D_MODEL), lambda i: (i, 0))
    else:
        blocks_per_seq = toks_per_seq // tb
        g_spec = pl.BlockSpec((None, 1, D_MODEL), lambda i: (i // blocks_per_seq, 0, 0))
    kern = functools.partial(_peer_mix_kernel, tb=tb, per_token_mod=per_token_mod)
    return pl.pallas_call(
        kern,
        grid=(n // tb,),
        in_specs=[pl.BlockSpec((tb, PEER_PAIRS), lambda i: (i, 0), memory_space=pltpu.SMEM),
                  pl.BlockSpec((PEER_PAIRS, tb), lambda i: (0, i)),
                  pl.BlockSpec((tb, D_MODEL), lambda i: (i, 0)), g_spec, _table_spec(tab)],
        out_specs=pl.BlockSpec((tb, D_MODEL), lambda i: (i, 0)),
        out_shape=jax.ShapeDtypeStruct((n, D_MODEL), F32),
        scratch_shapes=[pltpu.VMEM((PEER_PAIRS * ROW_CHUNKS, LANES), BF16)] * 2 + [
                        pltpu.VMEM((PEER_PAIRS, LANES), F32)] * 2,
        compiler_params=_cparams(("arbitrary",)),
        name="peer_mix",
    )(eidx, w, y3, gate2, tab)


def _layer(x, mod, conv_buf, C0, n0, m0, k_buf, v_buf, rel_bias, wts):
    B, T, _ = x.shape
    n = B * T
    tm = min(PROJ_TILE, n)
    tb = PEER_BLOCK
    x2 = x.reshape(n, D_MODEL)
    shift1, scale1, gate1, shift2, scale2, gate2 = jnp.split(mod, 6, axis=-1)
    per_token = T % tm != 0
    if per_token:
        expand = lambda a: jnp.repeat(a, T, axis=0)
    else:
        expand = lambda a: a.reshape(B, 1, D_MODEL)

    q, k, v, mqk, mv, mo, gcol, grow = _in_proj(
        x2, expand(scale1), expand(shift1), per_token, T, tm, wts["g1"], wts["wm"], wts["wgc"], wts["wgr"],
        wts["bd"], wts["qg"], wts["kg"])

    if k_buf is None:
        att = _att_prompt(q, k, v, _prompt_bias(rel_bias), B, T)
        k_new = k.reshape(B, T, H_B, HD_B)
        v_new = v.reshape(B, T, H_B, HD_B)
    else:
        wbuf = k_buf.shape[1]
        bias_c, bias_n = _sample_bias(rel_bias, T, wbuf)
        att, k_new, v_new = _att_sample(q, k, v, k_buf.reshape(B, wbuf, W_B), v_buf.reshape(B, wbuf, W_B),
                                        bias_c, bias_n, B, T, wbuf)
        k_new = k_new.reshape(B, wbuf, H_B, HD_B)
        v_new = v_new.reshape(B, wbuf, H_B, HD_B)

    ha, C, nn, m = _mlstm(mqk, mv, gcol, grow, wts["conv_w"], wts["conv_b"], conv_buf, wts["b_gate"], C0, n0, m0, B, T)
    mqk3 = mqk.reshape(B, T, 2 * W_A)
    if T >= CONV_W - 1:
        conv_new = mqk3[:, T - (CONV_W - 1):]
    else:
        conv_new = jnp.concatenate([conv_buf, mqk3], axis=1)[:, -(CONV_W - 1):]

    y1, h2 = _out_proj(ha, mo, att, x2, expand(gate1), expand(scale2), expand(shift2), per_token, T, tm,
                       wts["mg"], wts["ag"], wts["g2"], wts["bd"], wts["wo"])

    eidx, gate = _peer_route(h2, wts["wq"], wts["keys"], min(ROUTE_TILE, n))
    w = _peer_act(eidx, h2, gate, wts["u_tab"], tb)
    y = _peer_mix(eidx, w, y1, expand(gate2), per_token, T, wts["v_tab"], tb)
    return (y.reshape(B, T, D_MODEL), k_new, v_new, conv_new, C, nn.reshape(B, H_A, DK_A), m[:, :, 0, 0])


def _prep_weights(l, norm1_g, norm2_g, w_in, b_gate, conv_w, conv_b, q_norm_g, k_norm_g, att_out_g, mlstm_out_g,
                  w_out, peer_wq, peer_keys, peer_u, peer_v):
    w = w_in[l]
    wg = w[:, W_MAIN:]
    grp = np.arange(W_B) // HD_B
    bd = jnp.asarray((grp[:, None] == grp[None, :]).astype(np.float32) / HD_B, BF16)
    return dict(
        g1=norm1_g[l].reshape(1, -1), g2=norm2_g[l].reshape(1, -1),
        wm=w[:, :W_MAIN].astype(BF16),
        wgc=jnp.pad(wg, ((0, 0), (0, LANES - 2 * H_A))), wgr=wg.T,
        bd=bd, qg=jnp.tile(q_norm_g[l], H_B).reshape(1, -1), kg=jnp.tile(k_norm_g[l], H_B).reshape(1, -1),
        conv_w=conv_w[l], conv_b=conv_b[l], b_gate=b_gate[l],
        mg=mlstm_out_g[l].reshape(1, -1), ag=att_out_g[l].reshape(1, -1),
        wo=w_out[l].astype(BF16), wq=peer_wq[l].astype(BF16), keys=peer_keys[l],
        u_tab=_pack_table(peer_u[l]), v_tab=_pack_table(peer_v[l]),
    )


def kernel(x_prompt, x_sample, cache_k, cache_v, state_conv, state_C, state_n, state_m, c_prompt, c_sample,
           rel_bias, w_ada, b_ada, norm1_g, norm2_g, w_in, b_gate, conv_w, conv_b, q_norm_g, k_norm_g,
           att_out_g, mlstm_out_g, w_out, peer_wq, peer_keys, peer_u, peer_v):
    depth = w_ada.shape[0]
    bp = x_prompt.shape[0]
    bs = x_sample.shape[0]
    yp, ys = x_prompt, x_sample
    sp, ss = [], []
    for l in range(depth):
        wts = _prep_weights(l, norm1_g, norm2_g, w_in, b_gate, conv_w, conv_b, q_norm_g, k_norm_g, att_out_g,
                            mlstm_out_g, w_out, peer_wq, peer_keys, peer_u, peer_v)
        mod = _ada(jnp.concatenate([c_prompt, c_sample], axis=0), w_ada[l], b_ada[l])
        zc = jnp.zeros((bp, CONV_W - 1, 2 * W_A), F32)
        zC = jnp.zeros((bp, H_A, DK_A, DK_A), F32)
        zn = jnp.zeros((bp, H_A, DK_A), F32)
        zm = jnp.zeros((bp, H_A), F32)
        outp = _layer(yp, mod[:bp], zc, zC, zn, zm, None, None, rel_bias, wts)
        outs = _layer(ys, mod[bp:], state_conv[l], state_C[l], state_n[l], state_m[l], cache_k[l], cache_v[l],
                      rel_bias, wts)
        yp, ys = outp[0], outs[0]
        sp.append(outp[1:])
        ss.append(outs[1:])
    k_p, v_p, conv_p, C_p, n_p, m_p = [jnp.stack([s[i] for s in sp], axis=0) for i in range(6)]
    k_s, v_s, conv_s, C_s, n_s, m_s = [jnp.stack([s[i] for s in ss], axis=0) for i in range(6)]
    return (yp, ys, k_p, v_p, conv_p, C_p, n_p, m_p, k_s, v_s, conv_s, C_s, n_s, m_s)
```
